```python
import math
import jax, jax.numpy as jnp
from jax import lax
import numpy as np

D_MODEL = 1024
BATCH = 2
SEQ = 16384
DEPTH = 2

HEAD_DIM = 64
SB_HEADS = 4
DIL_HEADS = 4
HGRN_HEADS = 4
HGRN_DK = 128
HGRN_DV = 128
SB_WIDTH = SB_HEADS * HEAD_DIM
DIL_WIDTH = DIL_HEADS * HEAD_DIM
HGRN_WIDTH = HGRN_HEADS * HGRN_DV
HGRN_KDIM = HGRN_HEADS * HGRN_DK
MIX_WIDTH = SB_WIDTH + DIL_WIDTH + HGRN_WIDTH
IN_SPLITS = (SB_WIDTH, SB_WIDTH, SB_WIDTH, DIL_WIDTH, DIL_WIDTH, DIL_WIDTH, HGRN_KDIM, HGRN_KDIM, HGRN_WIDTH, HGRN_WIDTH)
IN_WIDTH = 3 * SB_WIDTH + 3 * DIL_WIDTH + 2 * HGRN_KDIM + 2 * HGRN_WIDTH
D_FF = 2816
QBLK = 128
HGRN_CHUNK = 64
DIL_PATTERNS = ((128, 1), (512, 4), (2048, 16))
ROPE_THETA = 10000.0
EPS = 1e-6
LB_FLOOR = 1e-30
NEG_BIG = -1e30
N_MOD = 9
HALF_STEP = 0.5

kernel_name = "hymba_style_sb_dilated_hgrn2_macaron_block"


def _rmsnorm(x):
    xf = x.astype(jnp.float32)
    return (xf * lax.rsqrt(jnp.mean(xf * xf, axis=-1, keepdims=True) + EPS)).astype(x.dtype)


def _split_heads(x, n_heads):
    b, s, w = x.shape
    return x.reshape(b, s, n_heads, w // n_heads).transpose(0, 2, 1, 3)


def _merge_heads(x):
    b, h, s, d = x.shape
    return x.transpose(0, 2, 1, 3).reshape(b, s, h * d)


def _rope(x):
    s, d = x.shape[2], x.shape[3]
    half = d // 2
    inv_freq = ROPE_THETA ** (-jnp.arange(half, dtype=jnp.float32) * 2.0 / d)
    ang = jnp.arange(s, dtype=jnp.float32)[:, None] * inv_freq[None, :]
    cos, sin = jnp.cos(ang), jnp.sin(ang)
    xf = x.astype(jnp.float32)
    x1, x2 = xf[..., :half], xf[..., half:]
    return jnp.concatenate([x1 * cos - x2 * sin, x2 * cos + x1 * sin], axis=-1).astype(x.dtype)


def _swiglu(h, w_gate, w_up, w_down):
    return (jax.nn.silu(h @ w_gate) * (h @ w_up)) @ w_down


def _stick_breaking(q, k, v):
    b, h, s, d = q.shape
    nb = s // QBLK
    qf = q.astype(jnp.float32) * (d ** -0.5)
    kf = k.astype(jnp.float32)
    vf = v.astype(jnp.float32)
    q_blocks = qf.reshape(b, h, nb, QBLK, d).transpose(2, 0, 1, 3, 4)
    starts = jnp.arange(nb, dtype=jnp.int32) * QBLK
    kpos = jnp.arange(s, dtype=jnp.int32)

    def block(args):
        qb, start = args
        z = jnp.einsum('bhqd,bhkd->bhqk', qb, kf)
        qpos = start + jnp.arange(QBLK, dtype=jnp.int32)
        causal = kpos[None, :] < qpos[:, None]
        log_not_beta = jnp.where(causal, jax.nn.log_sigmoid(-z), 0.0)
        tail = lax.cumsum(log_not_beta, axis=3, reverse=True) - log_not_beta
        log_w = jnp.where(causal, jax.nn.log_sigmoid(z) + tail, NEG_BIG)
        w = jnp.exp(log_w)
        return jnp.einsum('bhqk,bhkd->bhqd', w, vf)

    out = lax.map(block, (q_blocks, starts))
    return out.transpose(1, 2, 0, 3, 4).reshape(b, h, s, d)


def _dilated_partial(q, k, v, window, dilation):
    b, h, s, d = q.shape
    steps = window // dilation
    sub_len = -(-s // (QBLK * dilation)) * QBLK
    pad = sub_len * dilation - s
    nb = sub_len // QBLK

    def to_residue(x):
        x = jnp.pad(x.astype(jnp.float32), ((0, 0), (0, 0), (0, pad), (0, 0)))
        x = x.reshape(b, h, sub_len, dilation, d).transpose(0, 1, 3, 2, 4)
        return x.reshape(b, h, dilation, nb, QBLK, d)

    qr, kr, vr = to_residue(q), to_residue(k), to_residue(v)

    def with_prev_block(x):
        prev = jnp.pad(x, ((0, 0), (0, 0), (0, 0), (1, 0), (0, 0), (0, 0)))[:, :, :, :-1]
        return jnp.concatenate([prev, x], axis=4)

    kw, vw = with_prev_block(kr), with_prev_block(vr)
    scores = jnp.einsum('bhrnqd,bhrnkd->bhrnqk', qr, kw)
    a_idx = jnp.arange(QBLK)[:, None]
    k_idx = jnp.arange(2 * QBLK)[None, :]
    dist = a_idx + QBLK - k_idx
    band = (dist >= 0) & (dist <= steps)
    valid = band[None] & ((jnp.arange(nb)[:, None, None] > 0) | (k_idx[None] >= QBLK))
    scores = jnp.where(valid, scores, NEG_BIG)
    mx = jnp.max(scores, axis=-1)
    p = jnp.where(valid, jnp.exp(scores - mx[..., None]), 0.0)
    den = jnp.sum(p, axis=-1)
    num = jnp.einsum('bhrnqk,bhrnkd->bhrnqd', p, vw)

    def back(x):
        x = x.reshape((b, h, dilation, sub_len) + x.shape[5:])
        perm = (0, 1, 3, 2) + tuple(range(4, x.ndim))
        x = x.transpose(perm).reshape((b, h, sub_len * dilation) + x.shape[4:])
        return x[:, :, :s]

    return back(num), back(den), back(mx)


def _dilated_mixture(q, k, v):
    parts = [_dilated_partial(q, k, v, w, r) for (w, r) in DIL_PATTERNS]
    m_all = parts[0][2]
    for _, _, m in parts[1:]:
        m_all = jnp.maximum(m_all, m)
    num = 0.0
    den = 0.0
    for n_i, d_i, m_i in parts:
        scale = jnp.exp(m_i - m_all)
        num = num + n_i * scale[..., None]
        den = den + d_i * scale
    return num / den[..., None]


def _hgrn2_chunkwise(q, k, v, log_f):
    b, h, s, dk = q.shape
    dv = v.shape[-1]
    n = s // HGRN_CHUNK

    def chunks(x):
        return x.astype(jnp.float32).reshape(b, h, n, HGRN_CHUNK, x.shape[-1]).transpose(2, 0, 1, 3, 4)

    qc, kc, vc = chunks(q), chunks(k), chunks(v)
    g_cum = jnp.cumsum(chunks(log_f), axis=3)
    causal = jnp.tril(jnp.ones((HGRN_CHUNK, HGRN_CHUNK), dtype=bool))[:, :, None]

    def step(state, xs):
        qb, kb, vb, gb = xs
        diff = gb[:, :, :, None, :] - gb[:, :, None, :, :]
        decay = jnp.where(causal, jnp.exp(jnp.where(causal, diff, 0.0)), 0.0)
        scores = jnp.einsum('bhtd,bhsd,bhtsd->bhts', qb, kb, decay)
        o = jnp.einsum('bhts,bhsv->bhtv', scores, vb) + jnp.einsum('bhtd,bhdv->bhtv', qb * jnp.exp(gb), state)
        g_last = gb[:, :, -1, :]
        new_state = jnp.exp(g_last)[..., None] * state + jnp.einsum('bhsd,bhsv->bhdv', kb * jnp.exp(g_last[:, :, None, :] - gb), vb)
        return new_state, o

    state0 = jnp.zeros((b, h, dk, dv), jnp.float32)
    _, o = lax.scan(step, state0, (qc, kc, vc, g_cum))
    return o.transpose(1, 2, 0, 3, 4).reshape(b, h, s, dv)


def _hybrid_mixer(h, w_in, w_out, q_norm_g, k_norm_g, hgrn_norm_g, lower_bound):
    dt = h.dtype
    proj = h @ w_in
    offsets = []
    acc = 0
    for wdt in IN_SPLITS[:-1]:
        acc += wdt
        offsets.append(acc)
    qa, ka, va, qd, kd, vd, qh, fh, ih, gh = jnp.split(proj, offsets, axis=-1)

    o_a = _stick_breaking(_split_heads(qa, SB_HEADS), _split_heads(ka, SB_HEADS), _split_heads(va, SB_HEADS))

    qd = _rope(_rmsnorm(_split_heads(qd, DIL_HEADS)) * q_norm_g) * (HEAD_DIM ** -0.5)
    kd = _rope(_rmsnorm(_split_heads(kd, DIL_HEADS)) * k_norm_g)
    o_d = _dilated_mixture(qd, kd, _split_heads(vd, DIL_HEADS))

    lb = lower_bound.reshape(HGRN_HEADS, 1, HGRN_DK).astype(jnp.float32)
    z = _split_heads(fh, HGRN_HEADS).astype(jnp.float32)
    log_f = jnp.logaddexp(jnp.log(jnp.maximum(lb, LB_FLOOR)), jnp.log1p(-lb) + jax.nn.log_sigmoid(z))
    k_h = -jnp.expm1(log_f)
    q_h = jax.nn.silu(_split_heads(qh, HGRN_HEADS))
    o_h = _hgrn2_chunkwise(q_h, k_h, _split_heads(ih, HGRN_HEADS), log_f)
    o_h = _rmsnorm(o_h) * hgrn_norm_g * jax.nn.silu(_split_heads(gh, HGRN_HEADS).astype(jnp.float32))

    y = jnp.concatenate([_merge_heads(o_a).astype(dt), _merge_heads(o_d).astype(dt), _merge_heads(o_h).astype(dt)], axis=-1)
    return y @ w_out


def setup_inputs(seed: int = 0) -> dict:
    key = jax.random.key(seed)
    ks = jax.random.split(key, 16)

    def nrm(k, shape, scale):
        return jax.random.normal(k, shape, jnp.float32) * scale

    return {
        "x": nrm(ks[0], (BATCH, SEQ, D_MODEL), 1.0),
        "c": nrm(ks[1], (BATCH, D_MODEL), 1.0),
        "w_mod": nrm(ks[2], (DEPTH, D_MODEL, N_MOD * D_MODEL), D_MODEL ** -0.5),
        "b_mod": nrm(ks[3], (DEPTH, N_MOD * D_MODEL), 0.02),
        "ffn1_w_gate": nrm(ks[4], (DEPTH, D_MODEL, D_FF), D_MODEL ** -0.5),
        "ffn1_w_up": nrm(ks[5], (DEPTH, D_MODEL, D_FF), D_MODEL ** -0.5),
        "ffn1_w_down": nrm(ks[6], (DEPTH, D_FF, D_MODEL), D_FF ** -0.5),
        "w_in": nrm(ks[7], (DEPTH, D_MODEL, IN_WIDTH), D_MODEL ** -0.5),
        "w_out": nrm(ks[8], (DEPTH, MIX_WIDTH, D_MODEL), MIX_WIDTH ** -0.5),
        "q_norm_g": 1.0 + nrm(ks[9], (DEPTH, HEAD_DIM), 0.02),
        "k_norm_g": 1.0 + nrm(ks[10], (DEPTH, HEAD_DIM), 0.02),
        "hgrn_norm_g": 1.0 + nrm(ks[11], (DEPTH, HGRN_DV), 0.02),
        "hgrn_lb_logits": nrm(ks[12], (DEPTH, HGRN_KDIM), 0.5),
        "ffn2_w_gate": nrm(ks[13], (DEPTH, D_MODEL, D_FF), D_MODEL ** -0.5),
        "ffn2_w_up": nrm(ks[14], (DEPTH, D_MODEL, D_FF), D_MODEL ** -0.5),
        "ffn2_w_down": nrm(ks[15], (DEPTH, D_FF, D_MODEL), D_FF ** -0.5),
    }


def reference(x, c, w_mod, b_mod, ffn1_w_gate, ffn1_w_up, ffn1_w_down, w_in, w_out, q_norm_g, k_norm_g, hgrn_norm_g, hgrn_lb_logits, ffn2_w_gate, ffn2_w_up, ffn2_w_down):
    lb_sm = jax.nn.softmax(hgrn_lb_logits.astype(jnp.float32), axis=0)
    lower_bounds = jnp.clip(jnp.cumsum(lb_sm, axis=0) - lb_sm[0:1], 0.0, 1.0 - EPS)
    for l in range(DEPTH):
        mod = jax.nn.silu(c) @ w_mod[l] + b_mod[l]
        sh1, sc1, g1, sh2, sc2, g2, sh3, sc3, g3 = jnp.split(mod[:, None, :], N_MOD, axis=-1)
        h = _rmsnorm(x) * (1.0 + sc1) + sh1
        x = x + HALF_STEP * g1 * _swiglu(h, ffn1_w_gate[l], ffn1_w_up[l], ffn1_w_down[l])
        h = _rmsnorm(x) * (1.0 + sc2) + sh2
        x = x + g2 * _hybrid_mixer(h, w_in[l], w_out[l], q_norm_g[l], k_norm_g[l], hgrn_norm_g[l], lower_bounds[l])
        h = _rmsnorm(x) * (1.0 + sc3) + sh3
        x = x + HALF_STEP * g3 * _swiglu(h, ffn2_w_gate[l], ffn2_w_up[l], ffn2_w_down[l])
    return x
```

```python
import functools
import math

import jax
import jax.numpy as jnp
from jax import lax
from jax.experimental import pallas as pl
from jax.experimental.pallas import tpu as pltpu

F32 = jnp.float32
BF16 = jnp.bfloat16

HEAD_DIM = 64
N_HEADS = 4
SLAB = N_HEADS * HEAD_DIM
HGRN_HEADS = 4
HGRN_D = 128
HGRN_W = HGRN_HEADS * HGRN_D
N_MOD = 9
EPS = 1e-6
LB_FLOOR = 1e-30
NEG_BIG = -1e30
HALF_STEP = 0.5
ROPE_THETA = 10000.0
DIL_PATTERNS = ((128, 1), (512, 4), (2048, 16))
DIL_QBLK = 128

OFF_QA, OFF_KA, OFF_VA = 0, 256, 512
OFF_QD, OFF_KD, OFF_VD = 768, 1024, 1280
OFF_QH, OFF_FH, OFF_IH, OFF_GH = 1536, 2048, 2560, 3072

VMEM_LIMIT = 56 * 1024 * 1024

SB_DEAD = 110.0


def _params(sem):
    return pltpu.CompilerParams(dimension_semantics=sem, vmem_limit_bytes=VMEM_LIMIT)


def _const_spec(shape):
    nd = len(shape)
    return pl.BlockSpec(shape, lambda *_: (0,) * nd, pipeline_mode=pl.Buffered(1))


def _split_bf16(x):
    hi = x.astype(BF16)
    lo = (x - hi.astype(F32)).astype(BF16)
    return hi, lo


def _dot(a, b):
    return jnp.dot(a, b, preferred_element_type=F32)


def _dot_nt(a, b):
    return lax.dot_general(a, b, (((1,), (1,)), ((), ())), preferred_element_type=F32)


def _dot_tn(a, b):
    return lax.dot_general(a, b, (((0,), (0,)), ((), ())), preferred_element_type=F32)


def _dot_nt_3pass(a_hi, a_lo, b_hi, b_lo):
    return _dot_nt(a_hi, b_hi) + (_dot_nt(a_hi, b_lo) + _dot_nt(a_lo, b_hi))


def _sigmoid(x):
    return 1.0 / (1.0 + jnp.exp(-x))


def _log_sigmoid(x):
    return jnp.minimum(x, 0.0) - jnp.log(1.0 + jnp.exp(-jnp.abs(x)))


def _rms_scale(x):
    return lax.rsqrt(jnp.mean(x * x, axis=-1, keepdims=True) + EPS)


def _head_of_lane(lane):
    return jnp.right_shift(lane, HEAD_DIM.bit_length() - 1)


def _mod_kernel(c_ref, w_ref, b_ref, o_ref):
    c = c_ref[...]
    sc = c * _sigmoid(c)
    sc_hi, sc_lo = _split_bf16(sc)
    w = w_ref[0]
    w_hi, w_lo = _split_bf16(w)
    o_ref[0] = _dot(sc_hi, w_hi) + (_dot(sc_hi, w_lo) + _dot(sc_lo, w_hi)) + b_ref[0]


def _modulation(c, w_mod, b_mod):
    depth, d, n = w_mod.shape
    rows = 16
    nb = c.shape[0]
    assert nb <= rows
    c = jnp.pad(c, ((0, rows - nb), (0, 0)))
    b = rows
    tn = 1152
    out = pl.pallas_call(
        _mod_kernel,
        grid=(depth, n // tn),
        in_specs=[
            pl.BlockSpec((b, d), lambda l, j: (0, 0)),
            pl.BlockSpec((1, d, tn), lambda l, j: (l, 0, j)),
            pl.BlockSpec((1, 1, tn), lambda l, j: (l, 0, j)),
        ],
        out_specs=pl.BlockSpec((1, b, tn), lambda l, j: (l, 0, j)),
        out_shape=jax.ShapeDtypeStruct((depth, b, n), F32),
        compiler_params=_params(("arbitrary", "arbitrary")),
        name="adaln_mod",
    )(c, w_mod, b_mod.reshape(depth, 1, n))
    return out[:, :nb]


FFN_CHUNK = 256


def _ffn_kernel(x_ref, mod_ref, wgu_ref, wd_ref, o_ref, h_scr, acc_scr, *, row0, n_chunks):
    x = x_ref[0]
    shift = mod_ref[0, row0:row0 + 1, :]
    scale = mod_ref[0, row0 + 1:row0 + 2, :]
    gate = mod_ref[0, row0 + 2:row0 + 3, :]
    h = x * _rms_scale(x) * (1.0 + scale) + shift
    h_scr[...] = h.astype(BF16)
    acc_scr[...] = jnp.zeros_like(acc_scr)

    def body(c, carry):
        gu = _dot(h_scr[...], wgu_ref[c])
        g = gu[:, :FFN_CHUNK]
        u = gu[:, FFN_CHUNK:]
        a = (g * _sigmoid(g) * u).astype(BF16)
        acc_scr[...] += _dot(a, wd_ref[c])
        return carry

    lax.fori_loop(0, n_chunks, body, 0)
    o_ref[0] = x + (HALF_STEP * gate) * acc_scr[...]


def _ffn(x, mod_l, w_gate, w_up, w_down, row0, tm=512):
    b, s, d = x.shape
    dff = w_gate.shape[1]
    n_chunks = dff // FFN_CHUNK
    wg = w_gate.astype(BF16).reshape(d, n_chunks, FFN_CHUNK).transpose(1, 0, 2)
    wu = w_up.astype(BF16).reshape(d, n_chunks, FFN_CHUNK).transpose(1, 0, 2)
    wgu = jnp.concatenate([wg, wu], axis=-1)
    wd = w_down.astype(BF16).reshape(n_chunks, FFN_CHUNK, d)
    return pl.pallas_call(
        functools.partial(_ffn_kernel, row0=row0, n_chunks=n_chunks),
        grid=(b, s // tm),
        in_specs=[
            pl.BlockSpec((1, tm, d), lambda bi, i: (bi, i, 0)),
            pl.BlockSpec((1, N_MOD, d), lambda bi, i: (bi, 0, 0)),
            _const_spec(wgu.shape),
            _const_spec(wd.shape),
        ],
        out_specs=pl.BlockSpec((1, tm, d), lambda bi, i: (bi, i, 0)),
        out_shape=jax.ShapeDtypeStruct((b, s, d), F32),
        scratch_shapes=[pltpu.VMEM((tm, d), BF16), pltpu.VMEM((tm, d), F32)],
        compiler_params=_params(("arbitrary", "arbitrary")),
        name="ffn",
    )(x, mod_l, wgu, wd)


def _swap_halves(x):
    n = x.shape[-1]
    lane = lax.broadcasted_iota(jnp.int32, x.shape, x.ndim - 1)
    up = pltpu.roll(x, n - HEAD_DIM // 2, x.ndim - 1)
    down = pltpu.roll(x, HEAD_DIM // 2, x.ndim - 1)
    return jnp.where(jnp.bitwise_and(lane, HEAD_DIM - 1) < HEAD_DIM // 2, up, down)


def _qk_norm_rope(x, gain, cos, sin_signed, head_mean):
    sq_hi, sq_lo = _split_bf16(x * x)
    ms = _dot(sq_hi, head_mean) + _dot(sq_lo, head_mean)
    xn = x * lax.rsqrt(ms + EPS) * gain
    return xn * cos + _swap_halves(xn) * sin_signed


def _inproj_kernel(x_ref, mod_ref, w_ref, cos_ref, sin_ref, qg_ref, kg_ref, hm_ref,
                   qa_ref, ka_ref, va_ref, qd_ref, kd_ref, vd_ref, qh_ref, fh_ref, ih_ref, gh_ref):
    x = x_ref[0]
    shift = mod_ref[0, 3:4, :]
    scale = mod_ref[0, 4:5, :]
    h = (x * _rms_scale(x) * (1.0 + scale) + shift).astype(BF16)

    def proj(off, width):
        return _dot(h, w_ref[:, off:off + width])

    qa_ref[0] = proj(OFF_QA, SLAB)
    ka_ref[0] = proj(OFF_KA, SLAB)
    va_ref[0] = proj(OFF_VA, SLAB).astype(BF16)
    cos = cos_ref[...]
    sin = sin_ref[...]
    hm = hm_ref[...]
    qd = _qk_norm_rope(proj(OFF_QD, SLAB), qg_ref[...], cos, sin, hm)
    qd_ref[0] = qd * (HEAD_DIM ** -0.5)
    kd_ref[0] = _qk_norm_rope(proj(OFF_KD, SLAB), kg_ref[...], cos, sin, hm)
    vd_ref[0] = proj(OFF_VD, SLAB).astype(BF16)
    qh_ref[0] = proj(OFF_QH, HGRN_W)
    fh_ref[0] = proj(OFF_FH, HGRN_W)
    ih_ref[0] = proj(OFF_IH, HGRN_W).astype(BF16)
    gh_ref[0] = proj(OFF_GH, HGRN_W)


def _rope_tables(s):
    half = HEAD_DIM // 2
    inv_freq = ROPE_THETA ** (-jnp.arange(half, dtype=F32) * 2.0 / HEAD_DIM)
    ang = jnp.arange(s, dtype=F32)[:, None] * inv_freq[None, :]
    cos, sin = jnp.cos(ang), jnp.sin(ang)
    cos_full = jnp.tile(jnp.concatenate([cos, cos], axis=-1), (1, N_HEADS))
    sin_signed = jnp.tile(jnp.concatenate([-sin, sin], axis=-1), (1, N_HEADS))
    return cos_full, sin_signed


def _inproj(x, mod_l, w_in, q_gain, k_gain, cos, sin, tm=512):
    b, s, d = x.shape
    w = w_in.astype(BF16)
    qg = jnp.tile(q_gain.reshape(1, HEAD_DIM), (1, N_HEADS))
    kg = jnp.tile(k_gain.reshape(1, HEAD_DIM), (1, N_HEADS))
    head_id = jnp.arange(SLAB) // HEAD_DIM
    head_mean = jnp.where(head_id[:, None] == head_id[None, :], 1.0 / HEAD_DIM, 0.0).astype(BF16)

    def tok(width):
        return pl.BlockSpec((1, tm, width), lambda bi, i: (bi, i, 0))

    def out(width, dt):
        return jax.ShapeDtypeStruct((b, s, width), dt)

    return pl.pallas_call(
        _inproj_kernel,
        grid=(b, s // tm),
        in_specs=[
            tok(d),
            pl.BlockSpec((1, N_MOD, d), lambda bi, i: (bi, 0, 0)),
            _const_spec(w.shape),
            pl.BlockSpec((tm, SLAB), lambda bi, i: (i, 0)),
            pl.BlockSpec((tm, SLAB), lambda bi, i: (i, 0)),
            _const_spec((1, SLAB)),
            _const_spec((1, SLAB)),
            _const_spec((SLAB, SLAB)),
        ],
        out_specs=[tok(SLAB)] * 6 + [tok(HGRN_W)] * 4,
        out_shape=[out(SLAB, F32), out(SLAB, F32), out(SLAB, BF16),
                   out(SLAB, F32), out(SLAB, F32), out(SLAB, BF16),
                   out(HGRN_W, F32), out(HGRN_W, F32), out(HGRN_W, BF16), out(HGRN_W, F32)],
        compiler_params=_params(("arbitrary", "arbitrary")),
        name="mixer_inproj",
    )(x, mod_l, w, cos, sin, qg, kg, head_mean)


SB_BLK = 256


def _sb_kernel(q_ref, k_ref, v_ref, u_ref, o_ref, acc_ref, run_ref):
    i = pl.program_id(1)
    hd = pl.program_id(2)
    blk = SB_BLK
    lane = lax.broadcasted_iota(jnp.int32, (1, SLAB), 1)
    head_mask = _head_of_lane(lane) == hd
    q = jnp.where(head_mask, q_ref[0] * (HEAD_DIM ** -0.5), 0.0)
    q_hi, q_lo = _split_bf16(q)
    acc_ref[...] = jnp.zeros_like(acc_ref)
    run_ref[...] = jnp.zeros_like(run_ref)
    row = lax.broadcasted_iota(jnp.int32, (blk, blk), 0)
    col = lax.broadcasted_iota(jnp.int32, (blk, blk), 1)
    u = u_ref[...]

    def cond(carry):
        j, alive = carry
        return jnp.logical_and(j >= 0, alive)

    def body(carry):
        j, _ = carry
        off = pl.multiple_of(j * blk, blk)
        k_hi, k_lo = _split_bf16(k_ref[0, pl.ds(off, blk), :])
        z = _dot_nt_3pass(q_hi, q_lo, k_hi, k_lo)
        causal = (col - row) < (i - j) * blk
        ls = _log_sigmoid(z)
        log_not_beta = jnp.where(causal, ls - z, 0.0)
        lnb_hi, lnb_lo = _split_bf16(log_not_beta)
        tail = _dot(lnb_hi, u) + _dot(lnb_lo, u)
        run = run_ref[...]
        w = jnp.where(causal, jnp.exp(ls + tail + run), 0.0)
        acc_ref[...] += _dot(w.astype(BF16), v_ref[0, pl.ds(off, blk), :])
        new_run = run + jnp.sum(log_not_beta, axis=1, keepdims=True)
        run_ref[...] = new_run
        return j - 1, jnp.max(new_run) > -SB_DEAD

    lax.while_loop(cond, body, (i, jnp.bool_(True)))
    contrib = jnp.where(head_mask, acc_ref[...], 0.0)

    @pl.when(hd == 0)
    def _():
        o_ref[0] = contrib

    @pl.when(hd > 0)
    def _():
        o_ref[0] += contrib


def _stick_breaking(q, k, v):
    b, s, _ = q.shape
    blk = SB_BLK
    idx = jnp.arange(blk)
    u = (idx[:, None] > idx[None, :]).astype(BF16)
    return pl.pallas_call(
        _sb_kernel,
        grid=(b, s // blk, N_HEADS),
        in_specs=[
            pl.BlockSpec((1, blk, SLAB), lambda bi, i, h: (bi, i, 0)),
            pl.BlockSpec((1, s, SLAB), lambda bi, i, h: (bi, 0, 0), pipeline_mode=pl.Buffered(1)),
            pl.BlockSpec((1, s, SLAB), lambda bi, i, h: (bi, 0, 0), pipeline_mode=pl.Buffered(1)),
            _const_spec((blk, blk)),
        ],
        out_specs=pl.BlockSpec((1, blk, SLAB), lambda bi, i, h: (bi, i, 0)),
        out_shape=jax.ShapeDtypeStruct((b, s, SLAB), F32),
        scratch_shapes=[pltpu.VMEM((blk, SLAB), F32), pltpu.VMEM((blk, 1), F32)],
        compiler_params=_params(("arbitrary", "arbitrary", "arbitrary")),
        name="stick_breaking",
    )(q, k, v, u)


def _dil_kernel(q_ref, kp_ref, kc_ref, vp_ref, vc_ref, num_ref, den_ref, mx_ref, *, steps):
    n = pl.program_id(2)
    qb = DIL_QBLK
    q = q_ref[0]
    k = jnp.concatenate([kp_ref[0], kc_ref[0]], axis=0)
    v = jnp.concatenate([vp_ref[0], vc_ref[0]], axis=0)
    k_hi, k_lo = _split_bf16(k)
    a_idx = lax.broadcasted_iota(jnp.int32, (qb, 2 * qb), 0)
    k_idx = lax.broadcasted_iota(jnp.int32, (qb, 2 * qb), 1)
    dist = a_idx + qb - k_idx
    band = jnp.logical_and(dist >= 0, dist <= steps)
    first_key = jnp.where(n > 0, 0, qb)
    valid = jnp.logical_and(band, k_idx >= first_key)
    lane = lax.broadcasted_iota(jnp.int32, (1, SLAB), 1)
    num = jnp.zeros((qb, SLAB), F32)
    den = jnp.zeros((qb, SLAB), F32)
    mx = jnp.zeros((qb, SLAB), F32)
    for hd in range(N_HEADS):
        head_mask = _head_of_lane(lane) == hd
        q_hi, q_lo = _split_bf16(jnp.where(head_mask, q, 0.0))
        sc = jnp.where(valid, _dot_nt_3pass(q_hi, q_lo, k_hi, k_lo), NEG_BIG)
        m = jnp.max(sc, axis=-1, keepdims=True)
        p = jnp.where(valid, jnp.exp(sc - m), 0.0)
        dsum = jnp.sum(p, axis=-1, keepdims=True)
        pv = _dot(p.astype(BF16), v)
        num = jnp.where(head_mask, pv, num)
        den = jnp.where(head_mask, dsum, den)
        mx = jnp.where(head_mask, m, mx)
    num_ref[0] = num
    den_ref[0] = den
    mx_ref[0] = mx


def _dilated_partial(q, k, v, window, dilation):
    b, s, _ = q.shape
    r = dilation
    qb = DIL_QBLK
    rows = s // r
    nb = rows // qb
    qr, kr, vr = (t.reshape(b, rows, r * SLAB) for t in (q, k, v))
    cur = pl.BlockSpec((1, qb, SLAB), lambda bi, j, n: (bi, n, j))
    prev = pl.BlockSpec((1, qb, SLAB), lambda bi, j, n: (bi, jnp.maximum(n - 1, 0), j))
    outs = pl.pallas_call(
        functools.partial(_dil_kernel, steps=window // dilation),
        grid=(b, r, nb),
        in_specs=[cur, prev, cur, prev, cur],
        out_specs=[cur, cur, cur],
        out_shape=[jax.ShapeDtypeStruct((b, rows, r * SLAB), F32)] * 3,
        compiler_params=_params(("arbitrary", "arbitrary", "arbitrary")),
        name=f"dilated_r{r}",
    )(qr, kr, kr, vr, vr)
    return [o.reshape(b, s, SLAB) for o in outs]


HGRN_CHUNK = 64
HGRN_SUB = 16
HGRN_TILE = 256


def _hgrn_kernel(q_ref, f_ref, i_ref, g_ref, lbl_ref, ng_ref, tril_ref, o_ref, state_ref, *, layer):
    @pl.when(pl.program_id(1) == 0)
    def _():
        state_ref[...] = jnp.zeros_like(state_ref)

    logits = lbl_ref[...]
    ex = jnp.exp(logits - jnp.max(logits, axis=0, keepdims=True))
    sm = ex / jnp.sum(ex, axis=0, keepdims=True)
    lb = jnp.zeros((1, HGRN_W), F32)
    for l in range(1, layer + 1):
        lb = lb + sm[l:l + 1, :]
    lb = jnp.clip(lb, 0.0, 1.0 - EPS)
    lb_floor = jnp.maximum(lb, LB_FLOOR)
    log_lb = jnp.log(lb_floor)
    one_m_lb = 1.0 - lb
    log_one_m_lb = jnp.log(one_m_lb)

    c = HGRN_CHUNK
    sub = HGRN_SUB
    n_sub = c // sub
    tril = tril_ref[...]
    norm_gain = ng_ref[...]
    row = lax.broadcasted_iota(jnp.int32, (c, c), 0)
    col = lax.broadcasted_iota(jnp.int32, (c, c), 1)
    sub_start = jnp.bitwise_and(row, -sub)
    off_band = col < sub_start
    in_band = jnp.logical_and(col >= sub_start, col <= row)

    def chunk(ci, carry):
        r0 = pl.multiple_of(ci * c, c)
        zq = q_ref[0, pl.ds(r0, c), :]
        zf = f_ref[0, pl.ds(r0, c), :]
        val = i_ref[0, pl.ds(r0, c), :]
        zg = g_ref[0, pl.ds(r0, c), :]

        bterm = log_one_m_lb + _log_sigmoid(zf)
        log_f = jnp.maximum(log_lb, bterm) + jnp.log(1.0 + jnp.exp(-jnp.abs(log_lb - bterm)))
        kk = one_m_lb * _sigmoid(-zf) - (lb_floor - lb)
        qq = zq * _sigmoid(zq)

        lf_hi, lf_lo = _split_bf16(log_f)
        g = _dot(tril, lf_hi) + _dot(tril, lf_lo)
        g_last = g[c - 1:c, :]
        q_dec = qq * jnp.exp(g)
        k_dec = kk * jnp.exp(g_last - g)

        refs = [jnp.zeros((1, HGRN_W), F32)] + [g[sub * i - 1:sub * i, :] for i in range(1, n_sub)]
        g_ref_rows = jnp.concatenate([jnp.broadcast_to(r, (sub, HGRN_W)) for r in refs], axis=0)
        qs = (qq * jnp.exp(g - g_ref_rows)).astype(BF16)
        k_scaled = [None] + [(kk * jnp.exp(jnp.minimum(refs[i] - g, 0.0))).astype(BF16)
                             for i in range(1, n_sub)]

        diag_cols = []
        for delta in range(sub):
            if delta == 0:
                prod = qq * kk
            else:
                k_sh = pltpu.roll(kk, delta, 0)
                g_sh = pltpu.roll(g, delta, 0)
                prod = qq * k_sh * jnp.exp(jnp.minimum(g - g_sh, 0.0))
            diag_cols.append(prod)

        outs = []
        for hd in range(HGRN_HEADS):
            ls = slice(hd * HGRN_D, (hd + 1) * HGRN_D)
            st = state_ref[hd]
            o_inter = _dot_nt(q_dec[:, ls].astype(BF16), st.astype(BF16))
            blocks = [jnp.zeros((sub, c), F32)]
            for i in range(1, n_sub):
                blocks.append(_dot_nt(qs[sub * i:sub * (i + 1), ls], k_scaled[i][:, ls]))
            p_band = jnp.zeros((c, c), F32)
            for delta in range(sub):
                dcol = jnp.sum(diag_cols[delta][:, ls], axis=-1, keepdims=True)
                p_band = jnp.where(col == row - delta, dcol, p_band)
            p = jnp.where(off_band, jnp.concatenate(blocks, axis=0), jnp.where(in_band, p_band, 0.0))
            v_h = val[:, ls]
            o_h = _dot(p.astype(BF16), v_h) + o_inter
            new_state = st * jnp.exp(g_last[:, ls]) + _dot_tn(v_h, k_dec[:, ls].astype(BF16))
            state_ref[hd] = new_state
            o_h = o_h * _rms_scale(o_h) * norm_gain
            outs.append(o_h)
        o = jnp.concatenate(outs, axis=-1) * (zg * _sigmoid(zg))
        o_ref[0, pl.ds(r0, c), :] = o
        return carry

    lax.fori_loop(0, HGRN_TILE // c, chunk, 0)


def _hgrn(qh, fh, ih, gh, lb_logits, norm_gain, layer):
    b, s, _ = qh.shape
    t = HGRN_TILE
    depth = lb_logits.shape[0]
    idx = jnp.arange(HGRN_CHUNK)
    tril = (idx[:, None] >= idx[None, :]).astype(BF16)
    tok = pl.BlockSpec((1, t, HGRN_W), lambda bi, i: (bi, i, 0))
    return pl.pallas_call(
        functools.partial(_hgrn_kernel, layer=layer),
        grid=(b, s // t),
        in_specs=[tok, tok, tok, tok,
                  _const_spec((depth, HGRN_W)),
                  _const_spec((1, HGRN_D)),
                  _const_spec((HGRN_CHUNK, HGRN_CHUNK))],
        out_specs=tok,
        out_shape=jax.ShapeDtypeStruct((b, s, HGRN_W), F32),
        scratch_shapes=[pltpu.VMEM((HGRN_HEADS, HGRN_D, HGRN_D), F32)],
        compiler_params=_params(("arbitrary", "arbitrary")),
        name="hgrn2",
    )(qh, fh, ih, gh, lb_logits.astype(F32), norm_gain.reshape(1, HGRN_D), tril)


def _outproj_kernel(x_ref, mod_ref, oa_ref, *rest):
    n_pat = len(DIL_PATTERNS)
    dil = rest[:3 * n_pat]
    oh_ref, w_ref, o_ref = rest[3 * n_pat:]
    nums = [dil[3 * p][0] for p in range(n_pat)]
    dens = [dil[3 * p + 1][0] for p in range(n_pat)]
    mxs = [dil[3 * p + 2][0] for p in range(n_pat)]
    m_all = mxs[0]
    for m in mxs[1:]:
        m_all = jnp.maximum(m_all, m)
    num = 0.0
    den = 0.0
    for n_i, d_i, m_i in zip(nums, dens, mxs):
        sc = jnp.exp(m_i - m_all)
        num = num + n_i * sc
        den = den + d_i * sc
    o_d = num / den
    y = _dot(oa_ref[0].astype(BF16), w_ref[0:SLAB, :])
    y += _dot(o_d.astype(BF16), w_ref[SLAB:2 * SLAB, :])
    y += _dot(oh_ref[0].astype(BF16), w_ref[2 * SLAB:, :])
    gate = mod_ref[0, 5:6, :]
    o_ref[0] = x_ref[0] + gate * y


def _outproj(x, mod_l, o_a, dil_parts, o_h, w_out, tm=512):
    b, s, d = x.shape
    w = w_out.astype(BF16)

    def tok(width):
        return pl.BlockSpec((1, tm, width), lambda bi, i: (bi, i, 0))

    flat = [a for part in dil_parts for a in part]
    return pl.pallas_call(
        _outproj_kernel,
        grid=(b, s // tm),
        in_specs=[tok(d), pl.BlockSpec((1, N_MOD, d), lambda bi, i: (bi, 0, 0)), tok(SLAB)]
                 + [tok(SLAB)] * len(flat) + [tok(HGRN_W), _const_spec(w.shape)],
        out_specs=tok(d),
        out_shape=jax.ShapeDtypeStruct((b, s, d), F32),
        compiler_params=_params(("arbitrary", "arbitrary")),
        name="mixer_outproj",
    )(x, mod_l, o_a, *flat, o_h, w)


def kernel(x, c, w_mod, b_mod, ffn1_w_gate, ffn1_w_up, ffn1_w_down, w_in, w_out, q_norm_g, k_norm_g,
           hgrn_norm_g, hgrn_lb_logits, ffn2_w_gate, ffn2_w_up, ffn2_w_down):
    b, s, d = x.shape
    depth = w_mod.shape[0]
    mod = _modulation(c, w_mod, b_mod).reshape(depth, b, N_MOD, d)
    cos, sin = _rope_tables(s)
    for l in range(depth):
        x = _ffn(x, mod[l], ffn1_w_gate[l], ffn1_w_up[l], ffn1_w_down[l], row0=0)
        qa, ka, va, qd, kd, vd, qh, fh, ih, gh = _inproj(
            x, mod[l], w_in[l], q_norm_g[l], k_norm_g[l], cos, sin)
        o_a = _stick_breaking(qa, ka, va)
        dil_parts = [_dilated_partial(qd, kd, vd, w, r) for (w, r) in DIL_PATTERNS]
        o_h = _hgrn(qh, fh, ih, gh, hgrn_lb_logits, hgrn_norm_g[l], layer=l)
        x = _outproj(x, mod[l], o_a, dil_parts, o_h, w_out[l])
        x = _ffn(x, mod[l], ffn2_w_gate[l], ffn2_w_up[l], ffn2_w_down[l], row0=6)
    return x
```

```python
import functools

import jax
import jax.numpy as jnp
import numpy as np
from jax import lax
from jax.experimental import pallas as pl
from jax.experimental.pallas import tpu as pltpu

F32 = jnp.float32
BF16 = jnp.bfloat16

HEAD_DIM = 64
N_HEADS = 4
SLAB = N_HEADS * HEAD_DIM
LANES = 128
N_HALVES = SLAB // LANES
HEADS_PER_HALF = LANES // HEAD_DIM
HGRN_HEADS = 4
HGRN_D = 128
HGRN_W = HGRN_HEADS * HGRN_D
N_MOD = 9
EPS = 1e-6
LB_FLOOR = 1e-30
NEG_BIG = -1e30
HALF_STEP = 0.5
ROPE_THETA = 10000.0
DIL_PATTERNS = ((128, 1), (512, 4), (2048, 16))
DIL_QBLK = 128

OFF_QA, OFF_KA, OFF_VA = 0, 256, 512
OFF_QD, OFF_KD, OFF_VD = 768, 1024, 1280
OFF_QH, OFF_FH, OFF_IH, OFF_GH = 1536, 2048, 2560, 3072

VMEM_LIMIT = 56 * 1024 * 1024

SB_DEAD = 110.0


def _params(sem):
    return pltpu.CompilerParams(dimension_semantics=sem, vmem_limit_bytes=VMEM_LIMIT)


def _const_spec(shape):
    nd = len(shape)
    return pl.BlockSpec(shape, lambda *_: (0,) * nd, pipeline_mode=pl.Buffered(1))


def _split_bf16(x):
    hi = x.astype(BF16)
    lo = (x - hi.astype(F32)).astype(BF16)
    return hi, lo


def _dot(a, b):
    return jnp.dot(a, b, preferred_element_type=F32)


def _dot_nt(a, b):
    return lax.dot_general(a, b, (((1,), (1,)), ((), ())), preferred_element_type=F32)


def _dot_tn(a, b):
    return lax.dot_general(a, b, (((0,), (0,)), ((), ())), preferred_element_type=F32)


def _sigmoid(x):
    return 1.0 / (1.0 + jnp.exp(-x))


def _log_sigmoid(x):
    return jnp.minimum(x, 0.0) - jnp.log(1.0 + jnp.exp(-jnp.abs(x)))


def _rms_scale(x):
    return lax.rsqrt(jnp.mean(x * x, axis=-1, keepdims=True) + EPS)


def _head_of_lane(lane):
    return jnp.right_shift(lane, HEAD_DIM.bit_length() - 1)


def _mod_kernel(c_ref, w_ref, b_ref, o_ref):
    c = c_ref[...]
    sc = c * _sigmoid(c)
    sc_hi, sc_lo = _split_bf16(sc)
    w = w_ref[0]
    w_hi, w_lo = _split_bf16(w)
    o_ref[0] = _dot(sc_hi, w_hi) + (_dot(sc_hi, w_lo) + _dot(sc_lo, w_hi)) + b_ref[0]


def _modulation(c, w_mod, b_mod):
    depth, d, n = w_mod.shape
    rows = 16
    nb = c.shape[0]
    assert nb <= rows
    c = jnp.pad(c, ((0, rows - nb), (0, 0)))
    b = rows
    tn = 1152
    out = pl.pallas_call(
        _mod_kernel,
        grid=(depth, n // tn),
        in_specs=[
            pl.BlockSpec((b, d), lambda l, j: (0, 0)),
            pl.BlockSpec((1, d, tn), lambda l, j: (l, 0, j)),
            pl.BlockSpec((1, 1, tn), lambda l, j: (l, 0, j)),
        ],
        out_specs=pl.BlockSpec((1, b, tn), lambda l, j: (l, 0, j)),
        out_shape=jax.ShapeDtypeStruct((depth, b, n), F32),
        compiler_params=_params(("arbitrary", "arbitrary")),
        name="adaln_mod",
    )(c, w_mod, b_mod.reshape(depth, 1, n))
    return out[:, :nb]


FFN_CHUNK = 256


def _ffn_kernel(x_ref, mod_ref, wg_ref, wu_ref, wd_ref, o_ref, h_scr, acc_scr, *, row0, n_chunks):
    x = x_ref[0]
    shift = mod_ref[0, row0:row0 + 1, :]
    scale = mod_ref[0, row0 + 1:row0 + 2, :]
    gate = mod_ref[0, row0 + 2:row0 + 3, :]
    h = x * _rms_scale(x) * (1.0 + scale) + shift
    h_scr[...] = h.astype(BF16)
    for c in range(n_chunks):
        cols = slice(c * FFN_CHUNK, (c + 1) * FFN_CHUNK)
        g = _dot(h_scr[...], wg_ref[:, cols])
        u = _dot(h_scr[...], wu_ref[:, cols])
        a = (g * _sigmoid(g) * u).astype(BF16)
        y = _dot(a, wd_ref[cols, :])
        if c == 0:
            acc_scr[...] = y
        else:
            acc_scr[...] += y
    o_ref[0] = x + (HALF_STEP * gate) * acc_scr[...]


def _ffn(x, mod_l, w_gate, w_up, w_down, row0, tm=512):
    b, s, d = x.shape
    dff = w_gate.shape[1]
    n_chunks = dff // FFN_CHUNK
    return pl.pallas_call(
        functools.partial(_ffn_kernel, row0=row0, n_chunks=n_chunks),
        grid=(b, s // tm),
        in_specs=[
            pl.BlockSpec((1, tm, d), lambda bi, i: (bi, i, 0)),
            pl.BlockSpec((1, N_MOD, d), lambda bi, i: (bi, 0, 0)),
            _const_spec((d, dff)),
            _const_spec((d, dff)),
            _const_spec((dff, d)),
        ],
        out_specs=pl.BlockSpec((1, tm, d), lambda bi, i: (bi, i, 0)),
        out_shape=jax.ShapeDtypeStruct((b, s, d), F32),
        scratch_shapes=[pltpu.VMEM((tm, d), BF16), pltpu.VMEM((tm, d), F32)],
        compiler_params=_params(("arbitrary", "arbitrary")),
        name="ffn",
    )(x, mod_l, w_gate.astype(BF16), w_up.astype(BF16), w_down.astype(BF16))


def _swap_halves(x):
    n = x.shape[-1]
    lane = lax.broadcasted_iota(jnp.int32, x.shape, x.ndim - 1)
    up = pltpu.roll(x, n - HEAD_DIM // 2, x.ndim - 1)
    down = pltpu.roll(x, HEAD_DIM // 2, x.ndim - 1)
    return jnp.where(jnp.bitwise_and(lane, HEAD_DIM - 1) < HEAD_DIM // 2, up, down)


def _qk_norm_rope(x, gain, cos, sin_signed, head_mean):
    sq_hi, sq_lo = _split_bf16(x * x)
    ms = _dot(sq_hi, head_mean) + _dot(sq_lo, head_mean)
    xn = x * lax.rsqrt(ms + EPS) * gain
    return xn * cos + _swap_halves(xn) * sin_signed


def _inproj_kernel(x_ref, mod_ref, w_ref, cos_ref, sin_ref, qg_ref, kg_ref, hm_ref,
                   qa_ref, ka_ref, va_ref, qd_ref, kd_ref, vd_ref, qh_ref, fh_ref, ih_ref, gh_ref):
    x = x_ref[0]
    shift = mod_ref[0, 3:4, :]
    scale = mod_ref[0, 4:5, :]
    h = (x * _rms_scale(x) * (1.0 + scale) + shift).astype(BF16)

    def proj(off, width):
        return _dot(h, w_ref[:, off:off + width])

    def put_halves(ref, val):
        for half in range(N_HALVES):
            ref[0, half] = val[:, half * LANES:(half + 1) * LANES]

    qa_ref[0] = proj(OFF_QA, SLAB)
    ka_ref[0] = proj(OFF_KA, SLAB)
    va_ref[0] = proj(OFF_VA, SLAB).astype(BF16)
    cos = cos_ref[...]
    sin = sin_ref[...]
    hm = hm_ref[...]
    qd = _qk_norm_rope(proj(OFF_QD, SLAB), qg_ref[...], cos, sin, hm)
    put_halves(qd_ref, qd * (HEAD_DIM ** -0.5))
    put_halves(kd_ref, _qk_norm_rope(proj(OFF_KD, SLAB), kg_ref[...], cos, sin, hm))
    put_halves(vd_ref, proj(OFF_VD, SLAB))
    qh_ref[0] = proj(OFF_QH, HGRN_W)
    fh_ref[0] = proj(OFF_FH, HGRN_W)
    ih_ref[0] = proj(OFF_IH, HGRN_W).astype(BF16)
    gh_ref[0] = proj(OFF_GH, HGRN_W)


def _rope_tables(s):
    half = HEAD_DIM // 2
    inv_freq = ROPE_THETA ** (-jnp.arange(half, dtype=F32) * 2.0 / HEAD_DIM)
    ang = jnp.arange(s, dtype=F32)[:, None] * inv_freq[None, :]
    cos, sin = jnp.cos(ang), jnp.sin(ang)
    cos_full = jnp.tile(jnp.concatenate([cos, cos], axis=-1), (1, N_HEADS))
    sin_signed = jnp.tile(jnp.concatenate([-sin, sin], axis=-1), (1, N_HEADS))
    return cos_full, sin_signed


def _inproj(x, mod_l, w_in, q_gain, k_gain, cos, sin, tm=512):
    b, s, d = x.shape
    w = w_in.astype(BF16)
    qg = jnp.tile(q_gain.reshape(1, HEAD_DIM), (1, N_HEADS))
    kg = jnp.tile(k_gain.reshape(1, HEAD_DIM), (1, N_HEADS))
    head_id = np.arange(SLAB) // HEAD_DIM
    head_mean = jnp.asarray(np.where(head_id[:, None] == head_id[None, :], 1.0 / HEAD_DIM, 0.0), BF16)

    def tok(width):
        return pl.BlockSpec((1, tm, width), lambda bi, i: (bi, i, 0))

    def out(width, dt):
        return jax.ShapeDtypeStruct((b, s, width), dt)

    halves = pl.BlockSpec((1, N_HALVES, tm, LANES), lambda bi, i: (bi, 0, i, 0))
    halves_out = jax.ShapeDtypeStruct((b, N_HALVES, s, LANES), F32)

    return pl.pallas_call(
        _inproj_kernel,
        grid=(b, s // tm),
        in_specs=[
            tok(d),
            pl.BlockSpec((1, N_MOD, d), lambda bi, i: (bi, 0, 0)),
            _const_spec(w.shape),
            pl.BlockSpec((tm, SLAB), lambda bi, i: (i, 0)),
            pl.BlockSpec((tm, SLAB), lambda bi, i: (i, 0)),
            _const_spec((1, SLAB)),
            _const_spec((1, SLAB)),
            _const_spec((SLAB, SLAB)),
        ],
        out_specs=[tok(SLAB)] * 3 + [halves] * 3 + [tok(HGRN_W)] * 4,
        out_shape=[out(SLAB, F32), out(SLAB, F32), out(SLAB, BF16),
                   halves_out, halves_out, halves_out,
                   out(HGRN_W, F32), out(HGRN_W, F32), out(HGRN_W, BF16), out(HGRN_W, F32)],
        compiler_params=_params(("arbitrary", "arbitrary")),
        name="mixer_inproj",
    )(x, mod_l, w, cos, sin, qg, kg, head_mean)


SB_BLK = 128


def _sb_kernel(q_ref, k_ref, v_ref, uu_ref, o_ref, acc_ref, run_ref, qcat_ref):
    i = pl.program_id(1)
    blk = SB_BLK
    lane = lax.broadcasted_iota(jnp.int32, (1, SLAB), 1)
    head_masks = [_head_of_lane(lane) == hd for hd in range(N_HEADS)]
    q = q_ref[0] * (HEAD_DIM ** -0.5)
    for hd in range(N_HEADS):
        q_hi, q_lo = _split_bf16(jnp.where(head_masks[hd], q, 0.0))
        qcat_ref[hd] = jnp.concatenate([q_hi, q_lo], axis=1)
    acc_ref[...] = jnp.zeros_like(acc_ref)
    run_ref[...] = jnp.zeros_like(run_ref)
    row = lax.broadcasted_iota(jnp.int32, (blk, blk), 0)
    col = lax.broadcasted_iota(jnp.int32, (blk, blk), 1)

    def key_block(j, diagonal):
        off = pl.multiple_of(j * blk, blk)
        k_hi, k_lo = _split_bf16(k_ref[0, pl.ds(off, blk), :])
        k_hh = jnp.concatenate([k_hi, k_hi], axis=1)
        v = v_ref[0, pl.ds(off, blk), :]
        uu = uu_ref[...]
        heads = range(N_HEADS)
        z = [_dot_nt(qcat_ref[hd], k_hh) + _dot_nt(qcat_ref[hd, :, 0:SLAB], k_lo) for hd in heads]
        ls = [_log_sigmoid(z[hd]) for hd in heads]
        lnb = [ls[hd] - z[hd] for hd in heads]
        if diagonal:
            lnb = [jnp.where(col < row, x, 0.0) for x in lnb]
        tail = []
        for hd in heads:
            lnb_hi, lnb_lo = _split_bf16(lnb[hd])
            tail.append(_dot(jnp.concatenate([lnb_hi, lnb_lo], axis=1), uu))
        run = [run_ref[hd] for hd in heads]
        w = [jnp.exp(ls[hd] + tail[hd] + run[hd]) for hd in heads]
        if diagonal:
            w = [jnp.where(col < row, x, 0.0) for x in w]
        pv_h = [_dot(w[hd].astype(BF16), v) for hd in heads]
        pv = pv_h[0]
        run_max = None
        for hd in heads:
            if hd > 0:
                pv = jnp.where(head_masks[hd], pv_h[hd], pv)
            new_run = run[hd] + jnp.sum(lnb[hd], axis=1, keepdims=True)
            run_ref[hd] = new_run
            run_max = new_run if run_max is None else jnp.maximum(run_max, new_run)
        acc_ref[...] += pv
        return jnp.max(run_max) > -SB_DEAD

    alive0 = key_block(i, True)

    def cond(carry):
        j, alive = carry
        return jnp.logical_and(j >= 0, alive)

    def body(carry):
        j, _ = carry
        return j - 1, key_block(j, False)

    lax.while_loop(cond, body, (i - 1, alive0))
    o_ref[0] = acc_ref[...]


def _stick_breaking(q, k, v):
    b, s, _ = q.shape
    blk = SB_BLK
    idx = np.arange(blk)
    u = (idx[:, None] > idx[None, :])
    uu = jnp.asarray(np.concatenate([u, u], axis=0), BF16)
    return pl.pallas_call(
        _sb_kernel,
        grid=(b, s // blk),
        in_specs=[
            pl.BlockSpec((1, blk, SLAB), lambda bi, i: (bi, i, 0)),
            pl.BlockSpec((1, s, SLAB), lambda bi, i: (bi, 0, 0), pipeline_mode=pl.Buffered(1)),
            pl.BlockSpec((1, s, SLAB), lambda bi, i: (bi, 0, 0), pipeline_mode=pl.Buffered(1)),
            _const_spec((2 * blk, blk)),
        ],
        out_specs=pl.BlockSpec((1, blk, SLAB), lambda bi, i: (bi, i, 0)),
        out_shape=jax.ShapeDtypeStruct((b, s, SLAB), F32),
        scratch_shapes=[pltpu.VMEM((blk, SLAB), F32),
                        pltpu.VMEM((N_HEADS, blk, 1), F32),
                        pltpu.VMEM((N_HEADS, blk, 2 * SLAB), BF16)],
        compiler_params=_params(("arbitrary", "arbitrary")),
        name="stick_breaking",
    )(q, k, v, uu)


DIL_TILE = DIL_QBLK * max(r for _, r in DIL_PATTERNS)


def _dil_kernel(q_ref, kp_ref, kc_ref, vp_ref, vc_ref, bias_ref, o_ref,
                k_all, v_all, num_s, den_s, mx_s):
    n = pl.program_id(1)
    t_len = DIL_TILE
    qb = DIL_QBLK
    k_all[:, 0:t_len, :] = kp_ref[0]
    k_all[:, t_len:2 * t_len, :] = kc_ref[0]
    v_all[:, 0:t_len, :] = vp_ref[0]
    v_all[:, t_len:2 * t_len, :] = vc_ref[0]
    num_s[...] = jnp.zeros_like(num_s)
    den_s[...] = jnp.zeros_like(den_s)
    mx_s[...] = jnp.full(mx_s.shape, NEG_BIG, F32)
    lane = lax.broadcasted_iota(jnp.int32, (1, LANES), 1)
    head_masks = [_head_of_lane(lane) == hh for hh in range(HEADS_PER_HALF)]

    for pi, (window, r) in enumerate(DIL_PATTERNS):
        assert window // r == qb
        last = pi == len(DIL_PATTERNS) - 1
        shift = r.bit_length() - 1
        tiles = t_len // qb

        def rows_of(start, size, r=r):
            return pl.ds(start, size) if r == 1 else pl.ds(start, size, stride=r)

        def tile(idx, carry, r=r, shift=shift, last=last, rows_of=rows_of):
            t = jnp.right_shift(idx, shift)
            j = jnp.bitwise_and(idx, r - 1)
            q0 = t * (qb * r) + j
            has_prev = jnp.logical_or(n > 0, t > 0)
            bias = bias_ref[jnp.where(has_prev, 0, 1)]
            q_rows = rows_of(q0, qb)
            k_rows = rows_of(t_len + q0 - qb * r, 2 * qb)
            pairs = [(half, hh) for half in range(N_HALVES) for hh in range(HEADS_PER_HALF)]
            qs = [q_ref[0, half, q_rows, :] for half in range(N_HALVES)]
            kb = [k_all[half, k_rows, :].astype(BF16) for half in range(N_HALVES)]
            vb = [v_all[half, k_rows, :].astype(BF16) for half in range(N_HALVES)]
            sc = [_dot_nt(jnp.where(head_masks[hh], qs[half], 0.0).astype(BF16), kb[half]) + bias
                  for half, hh in pairs]
            mx = [jnp.max(x, axis=-1, keepdims=True) for x in sc]
            p = [jnp.exp(x - m) for x, m in zip(sc, mx)]
            den = [jnp.sum(x, axis=-1, keepdims=True) for x in p]
            pv = [_dot(x.astype(BF16), vb[half]) for x, (half, _) in zip(p, pairs)]
            for half in range(N_HALVES):
                first = half * HEADS_PER_HALF
                num_t, den_t, mx_t = pv[first], den[first], mx[first]
                for hh in range(1, HEADS_PER_HALF):
                    hm = head_masks[hh]
                    num_t = jnp.where(hm, pv[first + hh], num_t)
                    den_t = jnp.where(hm, den[first + hh], den_t)
                    mx_t = jnp.where(hm, mx[first + hh], mx_t)
                m_old = mx_s[half, q_rows, :]
                m_new = jnp.maximum(m_old, mx_t)
                a_old = jnp.exp(m_old - m_new)
                a_new = jnp.exp(mx_t - m_new)
                num_new = num_s[half, q_rows, :] * a_old + num_t * a_new
                den_new = den_s[half, q_rows, :] * a_old + den_t * a_new
                if last:
                    o_ref[0, half, q_rows, :] = num_new / den_new
                else:
                    mx_s[half, q_rows, :] = jnp.broadcast_to(m_new, (qb, LANES))
                    num_s[half, q_rows, :] = num_new
                    den_s[half, q_rows, :] = jnp.broadcast_to(den_new, (qb, LANES))
            return carry

        lax.fori_loop(0, tiles, tile, 0)


def _dilated(q, k, v):
    b, _, s, _ = q.shape
    qb = DIL_QBLK
    t_len = DIL_TILE
    a_idx = np.arange(qb)[:, None]
    k_idx = np.arange(2 * qb)[None, :]
    dist = a_idx + qb - k_idx
    band = (dist >= 0) & (dist <= qb)
    bias = np.stack([np.where(band, 0.0, NEG_BIG), np.where(band & (k_idx >= qb), 0.0, NEG_BIG)])
    cur = pl.BlockSpec((1, N_HALVES, t_len, LANES), lambda bi, n: (bi, 0, n, 0))
    prev = pl.BlockSpec((1, N_HALVES, t_len, LANES), lambda bi, n: (bi, 0, jnp.maximum(n - 1, 0), 0))
    stat = pltpu.VMEM((N_HALVES, t_len, LANES), F32)
    both = pltpu.VMEM((N_HALVES, 2 * t_len, LANES), F32)
    return pl.pallas_call(
        _dil_kernel,
        grid=(b, s // t_len),
        in_specs=[cur, prev, cur, prev, cur, _const_spec((2, qb, 2 * qb))],
        out_specs=cur,
        out_shape=jax.ShapeDtypeStruct((b, N_HALVES, s, LANES), F32),
        scratch_shapes=[both, both, stat, stat, stat],
        compiler_params=_params(("arbitrary", "arbitrary")),
        name="dilated",
    )(q, k, k, v, v, jnp.asarray(bias, F32))


HGRN_CHUNK = 64
HGRN_LEVELS = HGRN_CHUNK.bit_length() - 1
HGRN_TILE = 256


def _hgrn_kernel(q_ref, f_ref, i_ref, g_ref, lbl_ref, ng_ref, tril_ref, sel_ref, lvl_ref, o_ref,
                 state_ref, *, layer):
    @pl.when(pl.program_id(1) == 0)
    def _():
        state_ref[...] = jnp.zeros_like(state_ref)

    logits = lbl_ref[...]
    ex = jnp.exp(logits - jnp.max(logits, axis=0, keepdims=True))
    sm = ex / jnp.sum(ex, axis=0, keepdims=True)
    lb = jnp.zeros((1, HGRN_W), F32)
    for l in range(1, layer + 1):
        lb = lb + sm[l:l + 1, :]
    lb = jnp.clip(lb, 0.0, 1.0 - EPS)
    lb_floor = jnp.maximum(lb, LB_FLOOR)
    log_lb = jnp.log(lb_floor)
    one_m_lb = 1.0 - lb
    log_one_m_lb = jnp.log(one_m_lb)

    c = HGRN_CHUNK
    norm_gain = ng_ref[...]

    def chunk(ci, carry):
        r0 = pl.multiple_of(ci * c, c)
        zq = q_ref[0, pl.ds(r0, c), :]
        zf = f_ref[0, pl.ds(r0, c), :]
        val = i_ref[0, pl.ds(r0, c), :]
        zg = g_ref[0, pl.ds(r0, c), :]

        bterm = log_one_m_lb + _log_sigmoid(zf)
        log_f = jnp.maximum(log_lb, bterm) + jnp.log(1.0 + jnp.exp(-jnp.abs(log_lb - bterm)))
        kk = one_m_lb * _sigmoid(-zf) - (lb_floor - lb)
        qq = zq * _sigmoid(zq)

        tril = tril_ref[...]
        lf_hi, lf_lo = _split_bf16(log_f)
        g = _dot(tril, lf_hi) + _dot(tril, lf_lo)
        g_last = g[c - 1:c, :]
        q_dec = (qq * jnp.exp(g)).astype(BF16)
        k_dec = (kk * jnp.exp(g_last - g)).astype(BF16)

        g_hi, g_lo = _split_bf16(g)
        sel = sel_ref[...]
        g_split = _dot(sel, g_hi) + _dot(sel, g_lo)
        q_lvl = [qq.astype(BF16)]
        k_lvl = [kk.astype(BF16)]
        for lv in range(HGRN_LEVELS):
            decay = jnp.exp(-jnp.abs(g - g_split[lv * c:(lv + 1) * c, :]))
            q_lvl.append((qq * decay).astype(BF16))
            k_lvl.append((kk * decay).astype(BF16))

        lvl = lvl_ref[...]
        heads = range(HGRN_HEADS)
        lanes = [slice(hd * HGRN_D, (hd + 1) * HGRN_D) for hd in heads]
        st = [state_ref[hd] for hd in heads]
        o_inter = [_dot_nt(q_dec[:, lanes[hd]], st[hd].astype(BF16)) for hd in heads]
        scores = [[_dot_nt(q_lvl[lv][:, lanes[hd]], k_lvl[lv][:, lanes[hd]])
                   for lv in range(HGRN_LEVELS + 1)] for hd in heads]
        for hd in heads:
            state_ref[hd] = (st[hd] * jnp.exp(g_last[:, lanes[hd]])
                             + _dot_tn(val[:, lanes[hd]], k_dec[:, lanes[hd]]))
        outs = []
        for hd in heads:
            p = jnp.zeros((c, c), F32)
            for lv in range(HGRN_LEVELS + 1):
                p = jnp.where(lvl == lv, scores[hd][lv], p)
            o_h = _dot(p.astype(BF16), val[:, lanes[hd]]) + o_inter[hd]
            outs.append(o_h * _rms_scale(o_h) * norm_gain)
        o = jnp.concatenate(outs, axis=-1) * (zg * _sigmoid(zg))
        o_ref[0, pl.ds(r0, c), :] = o
        return carry

    lax.fori_loop(0, HGRN_TILE // c, chunk, 0)


def _hgrn_tables():
    c = HGRN_CHUNK
    idx = np.arange(c)
    tril = idx[:, None] >= idx[None, :]
    sel = np.zeros((HGRN_LEVELS, c, c), np.float32)
    for lv in range(HGRN_LEVELS):
        half = 1 << lv
        split = (idx // (2 * half)) * (2 * half) + half - 1
        sel[lv, idx, split] = 1.0
    diff = idx[:, None] ^ idx[None, :]
    lvl = np.where(idx[:, None] > idx[None, :], np.floor(np.log2(np.maximum(diff, 1))) + 1, -1)
    lvl = np.where(idx[:, None] == idx[None, :], 0, lvl).astype(np.int32)
    return (jnp.asarray(tril, BF16), jnp.asarray(sel.reshape(HGRN_LEVELS * c, c), BF16),
            jnp.asarray(lvl))


def _hgrn(qh, fh, ih, gh, lb_logits, norm_gain, layer):
    b, s, _ = qh.shape
    t = HGRN_TILE
    c = HGRN_CHUNK
    depth = lb_logits.shape[0]
    tril, sel, lvl = _hgrn_tables()
    tok = pl.BlockSpec((1, t, HGRN_W), lambda bi, i: (bi, i, 0))
    return pl.pallas_call(
        functools.partial(_hgrn_kernel, layer=layer),
        grid=(b, s // t),
        in_specs=[tok, tok, tok, tok,
                  _const_spec((depth, HGRN_W)),
                  _const_spec((1, HGRN_D)),
                  _const_spec((c, c)),
                  _const_spec((HGRN_LEVELS * c, c)),
                  _const_spec((c, c))],
        out_specs=tok,
        out_shape=jax.ShapeDtypeStruct((b, s, HGRN_W), F32),
        scratch_shapes=[pltpu.VMEM((HGRN_HEADS, HGRN_D, HGRN_D), F32)],
        compiler_params=_params(("arbitrary", "arbitrary")),
        name="hgrn2",
    )(qh, fh, ih, gh, lb_logits.astype(F32), norm_gain.reshape(1, HGRN_D), tril, sel, lvl)


def _outproj_kernel(x_ref, mod_ref, oa_ref, od_ref, oh_ref, w_ref, o_ref):
    y = _dot(oa_ref[0].astype(BF16), w_ref[0:SLAB, :])
    for half in range(N_HALVES):
        lo = SLAB + half * LANES
        y += _dot(od_ref[0, half].astype(BF16), w_ref[lo:lo + LANES, :])
    y += _dot(oh_ref[0].astype(BF16), w_ref[2 * SLAB:, :])
    gate = mod_ref[0, 5:6, :]
    o_ref[0] = x_ref[0] + gate * y


def _outproj(x, mod_l, o_a, o_d, o_h, w_out, tm=512):
    b, s, d = x.shape
    w = w_out.astype(BF16)

    def tok(width):
        return pl.BlockSpec((1, tm, width), lambda bi, i: (bi, i, 0))

    return pl.pallas_call(
        _outproj_kernel,
        grid=(b, s // tm),
        in_specs=[tok(d), pl.BlockSpec((1, N_MOD, d), lambda bi, i: (bi, 0, 0)), tok(SLAB),
                  pl.BlockSpec((1, N_HALVES, tm, LANES), lambda bi, i: (bi, 0, i, 0)),
                  tok(HGRN_W), _const_spec(w.shape)],
        out_specs=tok(d),
        out_shape=jax.ShapeDtypeStruct((b, s, d), F32),
        compiler_params=_params(("arbitrary", "arbitrary")),
        name="mixer_outproj",
    )(x, mod_l, o_a, o_d, o_h, w)


def kernel(x, c, w_mod, b_mod, ffn1_w_gate, ffn1_w_up, ffn1_w_down, w_in, w_out, q_norm_g, k_norm_g,
           hgrn_norm_g, hgrn_lb_logits, ffn2_w_gate, ffn2_w_up, ffn2_w_down):
    b, s, d = x.shape
    depth = w_mod.shape[0]
    assert s % DIL_TILE == 0 and s % HGRN_TILE == 0 and s % SB_BLK == 0
    mod = _modulation(c, w_mod, b_mod).reshape(depth, b, N_MOD, d)
    cos, sin = _rope_tables(s)
    for l in range(depth):
        x = _ffn(x, mod[l], ffn1_w_gate[l], ffn1_w_up[l], ffn1_w_down[l], row0=0)
        qa, ka, va, qd, kd, vd, qh, fh, ih, gh = _inproj(
            x, mod[l], w_in[l], q_norm_g[l], k_norm_g[l], cos, sin)
        o_a = _stick_breaking(qa, ka, va)
        o_d = _dilated(qd, kd, vd)
        o_h = _hgrn(qh, fh, ih, gh, hgrn_lb_logits, hgrn_norm_g[l], layer=l)
        x = _outproj(x, mod[l], o_a, o_d, o_h, w_out[l])
        x = _ffn(x, mod[l], ffn2_w_gate[l], ffn2_w_up[l], ffn2_w_down[l], row0=6)
    return x
```

```python
import functools

import jax
import jax.numpy as jnp
import numpy as np
from jax import lax
from jax.experimental import pallas as pl
from jax.experimental.pallas import tpu as pltpu

F32 = jnp.float32
BF16 = jnp.bfloat16

HEAD_DIM = 64
N_HEADS = 4
SLAB = N_HEADS * HEAD_DIM
LANES = 128
N_HALVES = SLAB // LANES
HEADS_PER_HALF = LANES // HEAD_DIM
HGRN_HEADS = 4
HGRN_D = 128
HGRN_W = HGRN_HEADS * HGRN_D
N_MOD = 9
EPS = 1e-6
LB_FLOOR = 1e-30
NEG_BIG = -1e30
HALF_STEP = 0.5
ROPE_THETA = 10000.0
DIL_PATTERNS = ((128, 1), (512, 4), (2048, 16))
DIL_QBLK = 128

OFF_QA, OFF_KA, OFF_VA = 0, 256, 512
OFF_QD, OFF_KD, OFF_VD = 768, 1024, 1280
OFF_QH, OFF_FH, OFF_IH, OFF_GH = 1536, 2048, 2560, 3072

VMEM_LIMIT = 56 * 1024 * 1024

SB_DEAD = 110.0


def _params(sem):
    return pltpu.CompilerParams(dimension_semantics=sem, vmem_limit_bytes=VMEM_LIMIT)


def _const_spec(shape):
    nd = len(shape)
    return pl.BlockSpec(shape, lambda *_: (0,) * nd, pipeline_mode=pl.Buffered(1))


def _split_bf16(x):
    hi = x.astype(BF16)
    lo = (x - hi.astype(F32)).astype(BF16)
    return hi, lo


def _dot(a, b):
    return jnp.dot(a, b, preferred_element_type=F32)


def _dot_nt(a, b):
    return lax.dot_general(a, b, (((1,), (1,)), ((), ())), preferred_element_type=F32)


def _dot_tn(a, b):
    return lax.dot_general(a, b, (((0,), (0,)), ((), ())), preferred_element_type=F32)


def _sigmoid(x):
    return 1.0 / (1.0 + jnp.exp(-x))


def _log_sigmoid(x):
    return jnp.minimum(x, 0.0) - jnp.log(1.0 + jnp.exp(-jnp.abs(x)))


def _rms_scale(x):
    return lax.rsqrt(jnp.mean(x * x, axis=-1, keepdims=True) + EPS)


def _head_of_lane(lane):
    return jnp.right_shift(lane, HEAD_DIM.bit_length() - 1)


def _mod_kernel(c_ref, w_ref, b_ref, o_ref):
    c = c_ref[...]
    sc = c * _sigmoid(c)
    sc_hi, sc_lo = _split_bf16(sc)
    w = w_ref[0]
    w_hi, w_lo = _split_bf16(w)
    o_ref[0] = _dot(sc_hi, w_hi) + (_dot(sc_hi, w_lo) + _dot(sc_lo, w_hi)) + b_ref[0]


def _modulation(c, w_mod, b_mod):
    depth, d, n = w_mod.shape
    rows = 16
    nb = c.shape[0]
    assert nb <= rows
    c = jnp.pad(c, ((0, rows - nb), (0, 0)))
    b = rows
    tn = 1152
    out = pl.pallas_call(
        _mod_kernel,
        grid=(depth, n // tn),
        in_specs=[
            pl.BlockSpec((b, d), lambda l, j: (0, 0)),
            pl.BlockSpec((1, d, tn), lambda l, j: (l, 0, j)),
            pl.BlockSpec((1, 1, tn), lambda l, j: (l, 0, j)),
        ],
        out_specs=pl.BlockSpec((1, b, tn), lambda l, j: (l, 0, j)),
        out_shape=jax.ShapeDtypeStruct((depth, b, n), F32),
        compiler_params=_params(("arbitrary", "arbitrary")),
        name="adaln_mod",
    )(c, w_mod, b_mod.reshape(depth, 1, n))
    return out[:, :nb]


FFN_CHUNK = 256


def _ffn_kernel(x_ref, mod_ref, wg_ref, wu_ref, wd_ref, o_ref, h_scr, acc_scr, *, row0, n_chunks):
    x = x_ref[0]
    shift = mod_ref[0, row0:row0 + 1, :]
    scale = mod_ref[0, row0 + 1:row0 + 2, :]
    gate = mod_ref[0, row0 + 2:row0 + 3, :]
    h = x * _rms_scale(x) * (1.0 + scale) + shift
    h_scr[...] = h.astype(BF16)
    for c in range(n_chunks):
        cols = slice(c * FFN_CHUNK, (c + 1) * FFN_CHUNK)
        g = _dot(h_scr[...], wg_ref[:, cols])
        u = _dot(h_scr[...], wu_ref[:, cols])
        a = (g * _sigmoid(g) * u).astype(BF16)
        y = _dot(a, wd_ref[cols, :])
        if c == 0:
            acc_scr[...] = y
        else:
            acc_scr[...] += y
    o_ref[0] = x + (HALF_STEP * gate) * acc_scr[...]


def _ffn(x, mod_l, w_gate, w_up, w_down, row0, tm=1024):
    b, s, d = x.shape
    dff = w_gate.shape[1]
    n_chunks = dff // FFN_CHUNK
    return pl.pallas_call(
        functools.partial(_ffn_kernel, row0=row0, n_chunks=n_chunks),
        grid=(b, s // tm),
        in_specs=[
            pl.BlockSpec((1, tm, d), lambda bi, i: (bi, i, 0)),
            pl.BlockSpec((1, N_MOD, d), lambda bi, i: (bi, 0, 0)),
            _const_spec((d, dff)),
            _const_spec((d, dff)),
            _const_spec((dff, d)),
        ],
        out_specs=pl.BlockSpec((1, tm, d), lambda bi, i: (bi, i, 0)),
        out_shape=jax.ShapeDtypeStruct((b, s, d), F32),
        scratch_shapes=[pltpu.VMEM((tm, d), BF16), pltpu.VMEM((tm, d), F32)],
        compiler_params=_params(("arbitrary", "arbitrary")),
        name="ffn",
    )(x, mod_l, w_gate.astype(BF16), w_up.astype(BF16), w_down.astype(BF16))


def _swap_halves(x):
    n = x.shape[-1]
    lane = lax.broadcasted_iota(jnp.int32, x.shape, x.ndim - 1)
    up = pltpu.roll(x, n - HEAD_DIM // 2, x.ndim - 1)
    down = pltpu.roll(x, HEAD_DIM // 2, x.ndim - 1)
    return jnp.where(jnp.bitwise_and(lane, HEAD_DIM - 1) < HEAD_DIM // 2, up, down)


def _qk_norm_rope(x, gain, cos, sin_signed, head_mean):
    sq_hi, sq_lo = _split_bf16(x * x)
    ms = _dot(sq_hi, head_mean) + _dot(sq_lo, head_mean)
    xn = x * lax.rsqrt(ms + EPS) * gain
    return xn * cos + _swap_halves(xn) * sin_signed


def _inproj_kernel(x_ref, mod_ref, w_ref, cos_ref, sin_ref, qg_ref, kg_ref, hm_ref,
                   qa_ref, ka_ref, va_ref, qd_ref, kd_ref, vd_ref, qh_ref, fh_ref, ih_ref, gh_ref):
    x = x_ref[0]
    shift = mod_ref[0, 3:4, :]
    scale = mod_ref[0, 4:5, :]
    h = (x * _rms_scale(x) * (1.0 + scale) + shift).astype(BF16)

    def proj(off, width):
        return _dot(h, w_ref[:, off:off + width])

    def put_halves(ref, val):
        for half in range(N_HALVES):
            ref[0, half] = val[:, half * LANES:(half + 1) * LANES]

    qa_ref[0] = proj(OFF_QA, SLAB)
    ka_ref[0] = proj(OFF_KA, SLAB).astype(BF16)
    va_ref[0] = proj(OFF_VA, SLAB).astype(BF16)
    cos = cos_ref[...]
    sin = sin_ref[...]
    hm = hm_ref[...]
    qd = _qk_norm_rope(proj(OFF_QD, SLAB), qg_ref[...], cos, sin, hm)
    put_halves(qd_ref, qd * (HEAD_DIM ** -0.5))
    put_halves(kd_ref, _qk_norm_rope(proj(OFF_KD, SLAB), kg_ref[...], cos, sin, hm))
    put_halves(vd_ref, proj(OFF_VD, SLAB))
    qh_ref[0] = proj(OFF_QH, HGRN_W)
    fh_ref[0] = proj(OFF_FH, HGRN_W)
    ih_ref[0] = proj(OFF_IH, HGRN_W).astype(BF16)
    gh_ref[0] = proj(OFF_GH, HGRN_W)


def _rope_tables(s):
    half = HEAD_DIM // 2
    inv_freq = ROPE_THETA ** (-jnp.arange(half, dtype=F32) * 2.0 / HEAD_DIM)
    ang = jnp.arange(s, dtype=F32)[:, None] * inv_freq[None, :]
    cos, sin = jnp.cos(ang), jnp.sin(ang)
    cos_full = jnp.tile(jnp.concatenate([cos, cos], axis=-1), (1, N_HEADS))
    sin_signed = jnp.tile(jnp.concatenate([-sin, sin], axis=-1), (1, N_HEADS))
    return cos_full, sin_signed


def _inproj(x, mod_l, w_in, q_gain, k_gain, cos, sin, tm=512):
    b, s, d = x.shape
    w = w_in.astype(BF16)
    qg = jnp.tile(q_gain.reshape(1, HEAD_DIM), (1, N_HEADS))
    kg = jnp.tile(k_gain.reshape(1, HEAD_DIM), (1, N_HEADS))
    head_id = np.arange(SLAB) // HEAD_DIM
    head_mean = jnp.asarray(np.where(head_id[:, None] == head_id[None, :], 1.0 / HEAD_DIM, 0.0), BF16)

    def tok(width):
        return pl.BlockSpec((1, tm, width), lambda bi, i: (bi, i, 0))

    def out(width, dt):
        return jax.ShapeDtypeStruct((b, s, width), dt)

    halves = pl.BlockSpec((1, N_HALVES, tm, LANES), lambda bi, i: (bi, 0, i, 0))
    halves_out = jax.ShapeDtypeStruct((b, N_HALVES, s, LANES), F32)

    return pl.pallas_call(
        _inproj_kernel,
        grid=(b, s // tm),
        in_specs=[
            tok(d),
            pl.BlockSpec((1, N_MOD, d), lambda bi, i: (bi, 0, 0)),
            _const_spec(w.shape),
            pl.BlockSpec((tm, SLAB), lambda bi, i: (i, 0)),
            pl.BlockSpec((tm, SLAB), lambda bi, i: (i, 0)),
            _const_spec((1, SLAB)),
            _const_spec((1, SLAB)),
            _const_spec((SLAB, SLAB)),
        ],
        out_specs=[tok(SLAB)] * 3 + [halves] * 3 + [tok(HGRN_W)] * 4,
        out_shape=[out(SLAB, F32), out(SLAB, BF16), out(SLAB, BF16),
                   halves_out, halves_out, halves_out,
                   out(HGRN_W, F32), out(HGRN_W, F32), out(HGRN_W, BF16), out(HGRN_W, F32)],
        compiler_params=_params(("arbitrary", "arbitrary")),
        name="mixer_inproj",
    )(x, mod_l, w, cos, sin, qg, kg, head_mean)


SB_BLK = 128


def _sb_kernel(q_ref, k_ref, v_ref, uu_ref, o_ref, acc_ref, run_ref, qm_ref, z_ref):
    i = pl.program_id(1)
    blk = SB_BLK
    heads = range(N_HEADS)
    lane = lax.broadcasted_iota(jnp.int32, (1, SLAB), 1)
    head_masks = [_head_of_lane(lane) == hd for hd in heads]
    q = q_ref[0] * (HEAD_DIM ** -0.5)
    for hd in heads:
        qm_ref[hd] = jnp.where(head_masks[hd], q, 0.0).astype(BF16)
    acc_ref[...] = jnp.zeros_like(acc_ref)
    run_ref[...] = jnp.zeros_like(run_ref)
    row = lax.broadcasted_iota(jnp.int32, (blk, blk), 0)
    col = lax.broadcasted_iota(jnp.int32, (blk, blk), 1)

    def scores(j):
        off = pl.multiple_of(jnp.maximum(j, 0) * blk, blk)
        kb = k_ref[0, pl.ds(off, blk), :]
        return [_dot_nt(qm_ref[hd], kb) for hd in heads]

    def key_block(j, z, diagonal):
        off = pl.multiple_of(j * blk, blk)
        v = v_ref[0, pl.ds(off, blk), :]
        uu = uu_ref[...]
        ls = [_log_sigmoid(z[hd]) for hd in heads]
        lnb = [ls[hd] - z[hd] for hd in heads]
        if diagonal:
            lnb = [jnp.where(col < row, x, 0.0) for x in lnb]
        tail = []
        for hd in heads:
            lnb_hi, lnb_lo = _split_bf16(lnb[hd])
            tail.append(_dot(jnp.concatenate([lnb_hi, lnb_lo], axis=1), uu))
        run = [run_ref[hd] for hd in heads]
        w = [jnp.exp(ls[hd] + tail[hd] + run[hd]) for hd in heads]
        if diagonal:
            w = [jnp.where(col < row, x, 0.0) for x in w]
        pv_h = [_dot(w[hd].astype(BF16), v) for hd in heads]
        pv = pv_h[0]
        run_max = None
        for hd in heads:
            if hd > 0:
                pv = jnp.where(head_masks[hd], pv_h[hd], pv)
            new_run = run[hd] + (tail[hd][:, 0:1] + lnb[hd][:, 0:1])
            run_ref[hd] = new_run
            run_max = new_run if run_max is None else jnp.maximum(run_max, new_run)
        acc_ref[...] += pv
        return jnp.max(run_max) > -SB_DEAD

    z_diag = scores(i)
    z_next = scores(i - 1)
    for hd in heads:
        z_ref[hd] = z_next[hd]
    alive0 = key_block(i, z_diag, True)

    def cond(carry):
        j, alive = carry
        return jnp.logical_and(j >= 0, alive)

    def body(carry):
        j, _ = carry
        z = [z_ref[hd] for hd in heads]
        z_ahead = scores(j - 1)
        alive = key_block(j, z, False)
        for hd in heads:
            z_ref[hd] = z_ahead[hd]
        return j - 1, alive

    lax.while_loop(cond, body, (i - 1, alive0))
    o_ref[0] = acc_ref[...]


def _stick_breaking(q, k, v):
    b, s, _ = q.shape
    blk = SB_BLK
    idx = np.arange(blk)
    u = (idx[:, None] > idx[None, :])
    uu = jnp.asarray(np.concatenate([u, u], axis=0), BF16)
    return pl.pallas_call(
        _sb_kernel,
        grid=(b, s // blk),
        in_specs=[
            pl.BlockSpec((1, blk, SLAB), lambda bi, i: (bi, i, 0)),
            pl.BlockSpec((1, s, SLAB), lambda bi, i: (bi, 0, 0), pipeline_mode=pl.Buffered(1)),
            pl.BlockSpec((1, s, SLAB), lambda bi, i: (bi, 0, 0), pipeline_mode=pl.Buffered(1)),
            _const_spec((2 * blk, blk)),
        ],
        out_specs=pl.BlockSpec((1, blk, SLAB), lambda bi, i: (bi, i, 0)),
        out_shape=jax.ShapeDtypeStruct((b, s, SLAB), F32),
        scratch_shapes=[pltpu.VMEM((blk, SLAB), F32),
                        pltpu.VMEM((N_HEADS, blk, 1), F32),
                        pltpu.VMEM((N_HEADS, blk, SLAB), BF16),
                        pltpu.VMEM((N_HEADS, blk, blk), F32)],
        compiler_params=_params(("arbitrary", "arbitrary")),
        name="stick_breaking",
    )(q, k, v, uu)


DIL_TILE = DIL_QBLK * max(r for _, r in DIL_PATTERNS)
DIL_UNROLL = 2


def _dil_kernel(q_ref, kp_ref, kc_ref, vp_ref, vc_ref, bias_ref, o_ref,
                k_all, v_all, num_s, den_s, mx_s):
    n = pl.program_id(1)
    t_len = DIL_TILE
    qb = DIL_QBLK
    k_all[:, 0:t_len, :] = kp_ref[0]
    k_all[:, t_len:2 * t_len, :] = kc_ref[0]
    v_all[:, 0:t_len, :] = vp_ref[0]
    v_all[:, t_len:2 * t_len, :] = vc_ref[0]
    num_s[...] = jnp.zeros_like(num_s)
    den_s[...] = jnp.zeros_like(den_s)
    mx_s[...] = jnp.full(mx_s.shape, NEG_BIG, F32)
    lane = lax.broadcasted_iota(jnp.int32, (1, LANES), 1)
    head_masks = [_head_of_lane(lane) == hh for hh in range(HEADS_PER_HALF)]

    for pi, (window, r) in enumerate(DIL_PATTERNS):
        assert window // r == qb
        last = pi == len(DIL_PATTERNS) - 1
        shift = r.bit_length() - 1
        tiles = t_len // qb

        def rows_of(start, size, r=r):
            return pl.ds(start, size) if r == 1 else pl.ds(start, size, stride=r)

        def tile_group(it, carry, r=r, shift=shift, last=last, rows_of=rows_of):
            q_rows, k_rows, bias = [], [], []
            for u in range(DIL_UNROLL):
                idx = it * DIL_UNROLL + u
                t = jnp.right_shift(idx, shift)
                j = jnp.bitwise_and(idx, r - 1)
                q0 = t * (qb * r) + j
                has_prev = jnp.logical_or(n > 0, t > 0)
                bias.append(bias_ref[jnp.where(has_prev, 0, 1)])
                q_rows.append(rows_of(q0, qb))
                k_rows.append(rows_of(t_len + q0 - qb * r, 2 * qb))
            slabs = [(u, half) for u in range(DIL_UNROLL) for half in range(N_HALVES)]
            units = [(si, hh) for si in range(len(slabs)) for hh in range(HEADS_PER_HALF)]
            qs = [q_ref[0, half, q_rows[u], :] for u, half in slabs]
            kb = [k_all[half, k_rows[u], :].astype(BF16) for u, half in slabs]
            vb = [v_all[half, k_rows[u], :].astype(BF16) for u, half in slabs]
            sc = [_dot_nt(jnp.where(head_masks[hh], qs[si], 0.0).astype(BF16), kb[si]) + bias[slabs[si][0]]
                  for si, hh in units]
            mx = [jnp.max(x, axis=-1, keepdims=True) for x in sc]
            p = [jnp.exp(x - m) for x, m in zip(sc, mx)]
            den = [jnp.sum(x, axis=-1, keepdims=True) for x in p]
            pv = [_dot(x.astype(BF16), vb[si]) for x, (si, _) in zip(p, units)]
            for si, (u, half) in enumerate(slabs):
                first = si * HEADS_PER_HALF
                num_t, den_t, mx_t = pv[first], den[first], mx[first]
                for hh in range(1, HEADS_PER_HALF):
                    hm = head_masks[hh]
                    num_t = jnp.where(hm, pv[first + hh], num_t)
                    den_t = jnp.where(hm, den[first + hh], den_t)
                    mx_t = jnp.where(hm, mx[first + hh], mx_t)
                rows = q_rows[u]
                m_old = mx_s[half, rows, :]
                m_new = jnp.maximum(m_old, mx_t)
                a_old = jnp.exp(m_old - m_new)
                a_new = jnp.exp(mx_t - m_new)
                num_new = num_s[half, rows, :] * a_old + num_t * a_new
                den_new = den_s[half, rows, :] * a_old + den_t * a_new
                if last:
                    o_ref[0, half, rows, :] = num_new / den_new
                else:
                    mx_s[half, rows, :] = jnp.broadcast_to(m_new, (qb, LANES))
                    num_s[half, rows, :] = num_new
                    den_s[half, rows, :] = jnp.broadcast_to(den_new, (qb, LANES))
            return carry

        lax.fori_loop(0, tiles // DIL_UNROLL, tile_group, 0)


def _dilated(q, k, v):
    b, _, s, _ = q.shape
    qb = DIL_QBLK
    t_len = DIL_TILE
    a_idx = np.arange(qb)[:, None]
    k_idx = np.arange(2 * qb)[None, :]
    dist = a_idx + qb - k_idx
    band = (dist >= 0) & (dist <= qb)
    bias = np.stack([np.where(band, 0.0, NEG_BIG), np.where(band & (k_idx >= qb), 0.0, NEG_BIG)])
    cur = pl.BlockSpec((1, N_HALVES, t_len, LANES), lambda bi, n: (bi, 0, n, 0))
    prev = pl.BlockSpec((1, N_HALVES, t_len, LANES), lambda bi, n: (bi, 0, jnp.maximum(n - 1, 0), 0))
    stat = pltpu.VMEM((N_HALVES, t_len, LANES), F32)
    both = pltpu.VMEM((N_HALVES, 2 * t_len, LANES), F32)
    return pl.pallas_call(
        _dil_kernel,
        grid=(b, s // t_len),
        in_specs=[cur, prev, cur, prev, cur, _const_spec((2, qb, 2 * qb))],
        out_specs=cur,
        out_shape=jax.ShapeDtypeStruct((b, N_HALVES, s, LANES), F32),
        scratch_shapes=[both, both, stat, stat, stat],
        compiler_params=_params(("arbitrary", "arbitrary")),
        name="dilated",
    )(q, k, k, v, v, jnp.asarray(bias, F32))


HGRN_CHUNK = 256
HGRN_LEVELS = HGRN_CHUNK.bit_length() - 1
HGRN_TILE = 256
SUBLANES = 8


def _neg_abs_split_distance(g, log_f, half):
    c, w = g.shape
    block = 2 * half
    if block >= SUBLANES:
        g3 = g.reshape(c // block, block, w)
        return -jnp.abs(g3 - g3[:, half - 1:half, :]).reshape(c, w)
    pos = jnp.bitwise_and(lax.broadcasted_iota(jnp.int32, (c, 1), 0), block - 1)
    if half == 1:
        return jnp.where(pos == 1, log_f, 0.0)
    assert half == 2
    nxt = pltpu.roll(log_f, c - 1, 0)
    prv = pltpu.roll(log_f, 1, 0)
    return jnp.where(pos == 0, nxt, jnp.where(pos == 1, 0.0, jnp.where(pos == 2, log_f, log_f + prv)))


def _hgrn_kernel(q_ref, f_ref, i_ref, g_ref, lbl_ref, ng_ref, tril_ref, lvl_ref, o_ref,
                 state_ref, *, layer):
    @pl.when(pl.program_id(1) == 0)
    def _():
        state_ref[...] = jnp.zeros_like(state_ref)

    logits = lbl_ref[...]
    ex = jnp.exp(logits - jnp.max(logits, axis=0, keepdims=True))
    sm = ex / jnp.sum(ex, axis=0, keepdims=True)
    lb = jnp.zeros((1, HGRN_W), F32)
    for l in range(1, layer + 1):
        lb = lb + sm[l:l + 1, :]
    lb = jnp.clip(lb, 0.0, 1.0 - EPS)
    lb_floor = jnp.maximum(lb, LB_FLOOR)
    log_lb = jnp.log(lb_floor)
    one_m_lb = 1.0 - lb
    log_one_m_lb = jnp.log(one_m_lb)

    c = HGRN_CHUNK
    norm_gain = ng_ref[...]

    def chunk(ci, carry):
        r0 = 0 if HGRN_TILE == c else pl.multiple_of(ci * c, c)
        zq = q_ref[0, pl.ds(r0, c), :]
        zf = f_ref[0, pl.ds(r0, c), :]
        val = i_ref[0, pl.ds(r0, c), :]
        zg = g_ref[0, pl.ds(r0, c), :]

        bterm = log_one_m_lb + _log_sigmoid(zf)
        log_f = jnp.maximum(log_lb, bterm) + jnp.log(1.0 + jnp.exp(-jnp.abs(log_lb - bterm)))
        kk = one_m_lb * _sigmoid(-zf) - (lb_floor - lb)
        qq = zq * _sigmoid(zq)

        tril = tril_ref[...]
        lf_hi, lf_lo = _split_bf16(log_f)
        g = _dot(tril, lf_hi) + _dot(tril, lf_lo)
        g_last = g[c - 1:c, :]
        q_dec = (qq * jnp.exp(g)).astype(BF16)
        k_dec = (kk * jnp.exp(g_last - g)).astype(BF16)

        q_lvl = [qq.astype(BF16)]
        k_lvl = [kk.astype(BF16)]
        for lv in range(HGRN_LEVELS):
            decay = jnp.exp(_neg_abs_split_distance(g, log_f, 1 << lv).astype(BF16))
            q_lvl.append(q_lvl[0] * decay)
            k_lvl.append(k_lvl[0] * decay)

        lvl = lvl_ref[...]
        heads = range(HGRN_HEADS)
        lanes = [slice(hd * HGRN_D, (hd + 1) * HGRN_D) for hd in heads]
        st = [state_ref[hd] for hd in heads]
        o_inter = [_dot_nt(q_dec[:, lanes[hd]], st[hd].astype(BF16)) for hd in heads]
        scores = [[_dot_nt(q_lvl[lv][:, lanes[hd]], k_lvl[lv][:, lanes[hd]])
                   for lv in range(HGRN_LEVELS + 1)] for hd in heads]
        for hd in heads:
            state_ref[hd] = (st[hd] * jnp.exp(g_last[:, lanes[hd]])
                             + _dot_tn(val[:, lanes[hd]], k_dec[:, lanes[hd]]))
        outs = []
        level_masks = [lvl == lv for lv in range(HGRN_LEVELS + 1)]
        for hd in heads:
            p = jnp.zeros((c, c), F32)
            for lv in range(HGRN_LEVELS + 1):
                p = jnp.where(level_masks[lv], scores[hd][lv], p)
            o_h = _dot(p.astype(BF16), val[:, lanes[hd]]) + o_inter[hd]
            outs.append(o_h * _rms_scale(o_h) * norm_gain)
        o = jnp.concatenate(outs, axis=-1) * (zg * _sigmoid(zg))
        o_ref[0, pl.ds(r0, c), :] = o
        return carry

    if HGRN_TILE == c:
        chunk(0, 0)
    else:
        lax.fori_loop(0, HGRN_TILE // c, chunk, 0)


def _hgrn_tables():
    c = HGRN_CHUNK
    idx = np.arange(c)
    tril = idx[:, None] >= idx[None, :]
    diff = idx[:, None] ^ idx[None, :]
    lvl = np.where(idx[:, None] > idx[None, :], np.floor(np.log2(np.maximum(diff, 1))) + 1, -1)
    lvl = np.where(idx[:, None] == idx[None, :], 0, lvl).astype(np.int32)
    return jnp.asarray(tril, BF16), jnp.asarray(lvl)


def _hgrn(qh, fh, ih, gh, lb_logits, norm_gain, layer):
    b, s, _ = qh.shape
    t = HGRN_TILE
    c = HGRN_CHUNK
    depth = lb_logits.shape[0]
    tril, lvl = _hgrn_tables()
    tok = pl.BlockSpec((1, t, HGRN_W), lambda bi, i: (bi, i, 0))
    return pl.pallas_call(
        functools.partial(_hgrn_kernel, layer=layer),
        grid=(b, s // t),
        in_specs=[tok, tok, tok, tok,
                  _const_spec((depth, HGRN_W)),
                  _const_spec((1, HGRN_D)),
                  _const_spec((c, c)),
                  _const_spec((c, c))],
        out_specs=tok,
        out_shape=jax.ShapeDtypeStruct((b, s, HGRN_W), F32),
        scratch_shapes=[pltpu.VMEM((HGRN_HEADS, HGRN_D, HGRN_D), F32)],
        compiler_params=_params(("arbitrary", "arbitrary")),
        name="hgrn2",
    )(qh, fh, ih, gh, lb_logits.astype(F32), norm_gain.reshape(1, HGRN_D), tril, lvl)


def _outproj_kernel(x_ref, mod_ref, oa_ref, od_ref, oh_ref, w_ref, o_ref):
    y = _dot(oa_ref[0].astype(BF16), w_ref[0:SLAB, :])
    for half in range(N_HALVES):
        lo = SLAB + half * LANES
        y += _dot(od_ref[0, half].astype(BF16), w_ref[lo:lo + LANES, :])
    y += _dot(oh_ref[0].astype(BF16), w_ref[2 * SLAB:, :])
    gate = mod_ref[0, 5:6, :]
    o_ref[0] = x_ref[0] + gate * y


def _outproj(x, mod_l, o_a, o_d, o_h, w_out, tm=512):
    b, s, d = x.shape
    w = w_out.astype(BF16)

    def tok(width):
        return pl.BlockSpec((1, tm, width), lambda bi, i: (bi, i, 0))

    return pl.pallas_call(
        _outproj_kernel,
        grid=(b, s // tm),
        in_specs=[tok(d), pl.BlockSpec((1, N_MOD, d), lambda bi, i: (bi, 0, 0)), tok(SLAB),
                  pl.BlockSpec((1, N_HALVES, tm, LANES), lambda bi, i: (bi, 0, i, 0)),
                  tok(HGRN_W), _const_spec(w.shape)],
        out_specs=tok(d),
        out_shape=jax.ShapeDtypeStruct((b, s, d), F32),
        compiler_params=_params(("arbitrary", "arbitrary")),
        name="mixer_outproj",
    )(x, mod_l, o_a, o_d, o_h, w)


def kernel(x, c, w_mod, b_mod, ffn1_w_gate, ffn1_w_up, ffn1_w_down, w_in, w_out, q_norm_g, k_norm_g,
           hgrn_norm_g, hgrn_lb_logits, ffn2_w_gate, ffn2_w_up, ffn2_w_down):
    b, s, d = x.shape
    depth = w_mod.shape[0]
    assert s % DIL_TILE == 0 and s % HGRN_TILE == 0 and s % SB_BLK == 0
    mod = _modulation(c, w_mod, b_mod).reshape(depth, b, N_MOD, d)
    cos, sin = _rope_tables(s)
    for l in range(depth):
        x = _ffn(x, mod[l], ffn1_w_gate[l], ffn1_w_up[l], ffn1_w_down[l], row0=0)
        qa, ka, va, qd, kd, vd, qh, fh, ih, gh = _inproj(
            x, mod[l], w_in[l], q_norm_g[l], k_norm_g[l], cos, sin)
        o_a = _stick_breaking(qa, ka, va)
        o_d = _dilated(qd, kd, vd)
        o_h = _hgrn(qh, fh, ih, gh, hgrn_lb_logits, hgrn_norm_g[l], layer=l)
        x = _outproj(x, mod[l], o_a, o_d, o_h, w_out[l])
        x = _ffn(x, mod[l], ffn2_w_gate[l], ffn2_w_up[l], ffn2_w_down[l], row0=6)
    return x
```

```python
import functools

import jax
import jax.numpy as jnp
import numpy as np
from jax import lax
from jax.experimental import pallas as pl
from jax.experimental.pallas import tpu as pltpu

F32 = jnp.float32
BF16 = jnp.bfloat16

HEAD_DIM = 64
N_HEADS = 4
SLAB = N_HEADS * HEAD_DIM
LANES = 128
N_HALVES = SLAB // LANES
HEADS_PER_HALF = LANES // HEAD_DIM
HGRN_HEADS = 4
HGRN_D = 128
HGRN_W = HGRN_HEADS * HGRN_D
N_MOD = 9
EPS = 1e-6
LB_FLOOR = 1e-30
NEG_BIG = -1e30
HALF_STEP = 0.5
ROPE_THETA = 10000.0
DIL_PATTERNS = ((128, 1), (512, 4), (2048, 16))
DIL_QBLK = 128

OFF_QA, OFF_KA, OFF_VA = 0, 256, 512
OFF_QD, OFF_KD, OFF_VD = 768, 1024, 1280
OFF_QH, OFF_FH, OFF_IH, OFF_GH = 1536, 2048, 2560, 3072

VMEM_LIMIT = 56 * 1024 * 1024

SB_DEAD = 104.0


def _params(sem):
    return pltpu.CompilerParams(dimension_semantics=sem, vmem_limit_bytes=VMEM_LIMIT)


def _const_spec(shape):
    nd = len(shape)
    return pl.BlockSpec(shape, lambda *_: (0,) * nd, pipeline_mode=pl.Buffered(1))


def _split_bf16(x):
    hi = x.astype(BF16)
    lo = (x - hi.astype(F32)).astype(BF16)
    return hi, lo


def _dot(a, b):
    return jnp.dot(a, b, preferred_element_type=F32)


def _dot_nt(a, b):
    return lax.dot_general(a, b, (((1,), (1,)), ((), ())), preferred_element_type=F32)


def _dot_tn(a, b):
    return lax.dot_general(a, b, (((0,), (0,)), ((), ())), preferred_element_type=F32)


def _sigmoid(x):
    return 1.0 / (1.0 + jnp.exp(-x))


def _log_sigmoid(x):
    return jnp.minimum(x, 0.0) - jnp.log(1.0 + jnp.exp(-jnp.abs(x)))


def _rms_scale(x):
    return lax.rsqrt(jnp.mean(x * x, axis=-1, keepdims=True) + EPS)


def _head_of_lane(lane):
    return jnp.right_shift(lane, HEAD_DIM.bit_length() - 1)


def _mod_kernel(c_ref, w_ref, b_ref, o_ref):
    c = c_ref[...]
    sc = c * _sigmoid(c)
    sc_hi, sc_lo = _split_bf16(sc)
    w = w_ref[0]
    w_hi, w_lo = _split_bf16(w)
    o_ref[0] = _dot(sc_hi, w_hi) + (_dot(sc_hi, w_lo) + _dot(sc_lo, w_hi)) + b_ref[0]


def _modulation(c, w_mod, b_mod):
    depth, d, n = w_mod.shape
    rows = 16
    nb = c.shape[0]
    assert nb <= rows
    c = jnp.pad(c, ((0, rows - nb), (0, 0)))
    b = rows
    tn = 1152
    out = pl.pallas_call(
        _mod_kernel,
        grid=(depth, n // tn),
        in_specs=[
            pl.BlockSpec((b, d), lambda l, j: (0, 0)),
            pl.BlockSpec((1, d, tn), lambda l, j: (l, 0, j)),
            pl.BlockSpec((1, 1, tn), lambda l, j: (l, 0, j)),
        ],
        out_specs=pl.BlockSpec((1, b, tn), lambda l, j: (l, 0, j)),
        out_shape=jax.ShapeDtypeStruct((depth, b, n), F32),
        compiler_params=_params(("arbitrary", "arbitrary")),
        name="adaln_mod",
    )(c, w_mod, b_mod.reshape(depth, 1, n))
    return out[:, :nb]


FFN_CHUNK = 256


def _ffn_kernel(x_ref, mod_ref, wg_ref, wu_ref, wd_ref, o_ref, h_scr, acc_scr, *, row0, n_chunks):
    x = x_ref[0]
    shift = mod_ref[0, row0:row0 + 1, :]
    scale = mod_ref[0, row0 + 1:row0 + 2, :]
    gate = mod_ref[0, row0 + 2:row0 + 3, :]
    h = x * _rms_scale(x) * (1.0 + scale) + shift
    h_scr[...] = h.astype(BF16)
    for c in range(n_chunks):
        cols = slice(c * FFN_CHUNK, (c + 1) * FFN_CHUNK)
        g = _dot(h_scr[...], wg_ref[:, cols])
        u = _dot(h_scr[...], wu_ref[:, cols])
        a = (g * _sigmoid(g) * u).astype(BF16)
        y = _dot(a, wd_ref[cols, :])
        if c == 0:
            acc_scr[...] = y
        else:
            acc_scr[...] += y
    o_ref[0] = x + (HALF_STEP * gate) * acc_scr[...]


def _ffn(x, mod_l, w_gate, w_up, w_down, row0, tm=1024):
    b, s, d = x.shape
    dff = w_gate.shape[1]
    n_chunks = dff // FFN_CHUNK
    return pl.pallas_call(
        functools.partial(_ffn_kernel, row0=row0, n_chunks=n_chunks),
        grid=(b, s // tm),
        in_specs=[
            pl.BlockSpec((1, tm, d), lambda bi, i: (bi, i, 0)),
            pl.BlockSpec((1, N_MOD, d), lambda bi, i: (bi, 0, 0)),
            _const_spec((d, dff)),
            _const_spec((d, dff)),
            _const_spec((dff, d)),
        ],
        out_specs=pl.BlockSpec((1, tm, d), lambda bi, i: (bi, i, 0)),
        out_shape=jax.ShapeDtypeStruct((b, s, d), F32),
        scratch_shapes=[pltpu.VMEM((tm, d), BF16), pltpu.VMEM((tm, d), F32)],
        compiler_params=_params(("arbitrary", "arbitrary")),
        name="ffn",
    )(x, mod_l, w_gate.astype(BF16), w_up.astype(BF16), w_down.astype(BF16))


def _swap_halves(x):
    n = x.shape[-1]
    lane = lax.broadcasted_iota(jnp.int32, x.shape, x.ndim - 1)
    up = pltpu.roll(x, n - HEAD_DIM // 2, x.ndim - 1)
    down = pltpu.roll(x, HEAD_DIM // 2, x.ndim - 1)
    return jnp.where(jnp.bitwise_and(lane, HEAD_DIM - 1) < HEAD_DIM // 2, up, down)


def _qk_norm_rope(x, gain, cos, sin_signed, head_mean):
    sq_hi, sq_lo = _split_bf16(x * x)
    ms = _dot(sq_hi, head_mean) + _dot(sq_lo, head_mean)
    xn = x * lax.rsqrt(ms + EPS) * gain
    return xn * cos + _swap_halves(xn) * sin_signed


def _inproj_kernel(x_ref, mod_ref, w_ref, cos_ref, sin_ref, qg_ref, kg_ref, hm_ref,
                   qa_ref, ka_ref, va_ref, qd_ref, kd_ref, vd_ref, qh_ref, fh_ref, ih_ref, gh_ref):
    x = x_ref[0]
    shift = mod_ref[0, 3:4, :]
    scale = mod_ref[0, 4:5, :]
    h = (x * _rms_scale(x) * (1.0 + scale) + shift).astype(BF16)

    def proj(off, width):
        return _dot(h, w_ref[:, off:off + width])

    def put_halves(ref, val):
        for half in range(N_HALVES):
            ref[0, half] = val[:, half * LANES:(half + 1) * LANES]

    qa_ref[0] = proj(OFF_QA, SLAB)
    ka_ref[0] = proj(OFF_KA, SLAB).astype(BF16)
    va_ref[0] = proj(OFF_VA, SLAB).astype(BF16)
    cos = cos_ref[...]
    sin = sin_ref[...]
    hm = hm_ref[...]
    qd = _qk_norm_rope(proj(OFF_QD, SLAB), qg_ref[...], cos, sin, hm)
    put_halves(qd_ref, qd * (HEAD_DIM ** -0.5))
    put_halves(kd_ref, _qk_norm_rope(proj(OFF_KD, SLAB), kg_ref[...], cos, sin, hm))
    put_halves(vd_ref, proj(OFF_VD, SLAB))
    qh_ref[0] = proj(OFF_QH, HGRN_W)
    fh_ref[0] = proj(OFF_FH, HGRN_W)
    ih_ref[0] = proj(OFF_IH, HGRN_W).astype(BF16)
    gh_ref[0] = proj(OFF_GH, HGRN_W)


def _rope_tables(s):
    half = HEAD_DIM // 2
    inv_freq = ROPE_THETA ** (-jnp.arange(half, dtype=F32) * 2.0 / HEAD_DIM)
    ang = jnp.arange(s, dtype=F32)[:, None] * inv_freq[None, :]
    cos, sin = jnp.cos(ang), jnp.sin(ang)
    cos_full = jnp.tile(jnp.concatenate([cos, cos], axis=-1), (1, N_HEADS))
    sin_signed = jnp.tile(jnp.concatenate([-sin, sin], axis=-1), (1, N_HEADS))
    return cos_full, sin_signed


def _inproj(x, mod_l, w_in, q_gain, k_gain, cos, sin, tm=512):
    b, s, d = x.shape
    w = w_in.astype(BF16)
    qg = jnp.tile(q_gain.reshape(1, HEAD_DIM), (1, N_HEADS))
    kg = jnp.tile(k_gain.reshape(1, HEAD_DIM), (1, N_HEADS))
    head_id = np.arange(SLAB) // HEAD_DIM
    head_mean = jnp.asarray(np.where(head_id[:, None] == head_id[None, :], 1.0 / HEAD_DIM, 0.0), BF16)

    def tok(width):
        return pl.BlockSpec((1, tm, width), lambda bi, i: (bi, i, 0))

    def out(width, dt):
        return jax.ShapeDtypeStruct((b, s, width), dt)

    halves = pl.BlockSpec((1, N_HALVES, tm, LANES), lambda bi, i: (bi, 0, i, 0))
    halves_out = jax.ShapeDtypeStruct((b, N_HALVES, s, LANES), F32)

    return pl.pallas_call(
        _inproj_kernel,
        grid=(b, s // tm),
        in_specs=[
            tok(d),
            pl.BlockSpec((1, N_MOD, d), lambda bi, i: (bi, 0, 0)),
            _const_spec(w.shape),
            pl.BlockSpec((tm, SLAB), lambda bi, i: (i, 0)),
            pl.BlockSpec((tm, SLAB), lambda bi, i: (i, 0)),
            _const_spec((1, SLAB)),
            _const_spec((1, SLAB)),
            _const_spec((SLAB, SLAB)),
        ],
        out_specs=[tok(SLAB)] * 3 + [halves] * 3 + [tok(HGRN_W)] * 4,
        out_shape=[out(SLAB, F32), out(SLAB, BF16), out(SLAB, BF16),
                   halves_out, halves_out, halves_out,
                   out(HGRN_W, F32), out(HGRN_W, F32), out(HGRN_W, BF16), out(HGRN_W, F32)],
        compiler_params=_params(("arbitrary", "arbitrary")),
        name="mixer_inproj",
    )(x, mod_l, w, cos, sin, qg, kg, head_mean)


SB_QBLK = 256
SB_KBLK = 128


def _sb_kernel(q_ref, k_ref, v_ref, uu_ref, o_ref, acc_ref, run_ref, qm_ref, z_ref):
    i = pl.program_id(1)
    qblk, kblk = SB_QBLK, SB_KBLK
    heads = range(N_HEADS)
    lane = lax.broadcasted_iota(jnp.int32, (1, SLAB), 1)
    head_masks = [_head_of_lane(lane) == hd for hd in heads]
    q = q_ref[0] * (HEAD_DIM ** -0.5)
    for hd in heads:
        qm_ref[hd] = jnp.where(head_masks[hd], q, 0.0).astype(BF16)
    acc_ref[...] = jnp.zeros_like(acc_ref)
    run_ref[...] = jnp.zeros_like(run_ref)

    def scores(j):
        off = pl.multiple_of(jnp.maximum(j, 0) * kblk, kblk)
        kb = k_ref[0, pl.ds(off, kblk), :]
        return [_dot_nt(qm_ref[hd], kb) for hd in heads]

    def key_block(j, z, causal_below, first_row=0):
        off = pl.multiple_of(j * kblk, kblk)
        v = v_ref[0, pl.ds(off, kblk), :]
        uu = uu_ref[...]
        rows = slice(first_row, qblk)
        causal = None
        if causal_below is not None:
            shape = (qblk - first_row, kblk)
            col_minus_row = (lax.broadcasted_iota(jnp.int32, shape, 1)
                             - lax.broadcasted_iota(jnp.int32, shape, 0))
            causal = col_minus_row < causal_below + first_row
        zr = [z[hd][rows] for hd in heads]
        ls = [_log_sigmoid(zr[hd]) for hd in heads]
        lnb = [ls[hd] - zr[hd] for hd in heads]
        if causal is not None:
            lnb = [jnp.where(causal, x, 0.0) for x in lnb]
        tail = []
        for hd in heads:
            lnb_hi, lnb_lo = _split_bf16(lnb[hd])
            tail.append(_dot(jnp.concatenate([lnb_hi, lnb_lo], axis=1), uu))
        run = [run_ref[hd, rows, :] for hd in heads]
        w = [jnp.exp(ls[hd] + tail[hd] + run[hd]) for hd in heads]
        if causal is not None:
            w = [jnp.where(causal, x, 0.0) for x in w]
        pv_h = [_dot(w[hd].astype(BF16), v) for hd in heads]
        pv = pv_h[0]
        run_max = None
        for hd in heads:
            if hd > 0:
                pv = jnp.where(head_masks[hd], pv_h[hd], pv)
            new_run = run[hd] + (tail[hd][:, 0:1] + lnb[hd][:, 0:1])
            run_ref[hd, rows, :] = new_run
            run_max = new_run if run_max is None else jnp.maximum(run_max, new_run)
        acc_ref[rows, :] += pv
        return jnp.max(run_max) > -SB_DEAD

    per_q = qblk // kblk
    top = i * per_q + per_q - 1
    z_now = scores(top)
    alive = None
    for d in range(per_q):
        z_next = scores(top - d - 1)
        if d == per_q - 1:
            for hd in heads:
                z_ref[hd] = z_next[hd]
        alive = key_block(top - d, z_now, (d + 1 - per_q) * kblk, first_row=(per_q - 1 - d) * kblk)
        z_now = z_next

    def cond(carry):
        j, alive = carry
        return jnp.logical_and(j >= 0, alive)

    def body(carry):
        j, _ = carry
        z = [z_ref[hd] for hd in heads]
        z_ahead = scores(j - 1)
        alive = key_block(j, z, None)
        for hd in heads:
            z_ref[hd] = z_ahead[hd]
        return j - 1, alive

    lax.while_loop(cond, body, (top - per_q, alive))
    o_ref[0] = acc_ref[...]


def _stick_breaking(q, k, v):
    b, s, _ = q.shape
    qblk, kblk = SB_QBLK, SB_KBLK
    idx = np.arange(kblk)
    u = (idx[:, None] > idx[None, :])
    uu = jnp.asarray(np.concatenate([u, u], axis=0), BF16)
    return pl.pallas_call(
        _sb_kernel,
        grid=(b, s // qblk),
        in_specs=[
            pl.BlockSpec((1, qblk, SLAB), lambda bi, i: (bi, i, 0)),
            pl.BlockSpec((1, s, SLAB), lambda bi, i: (bi, 0, 0), pipeline_mode=pl.Buffered(1)),
            pl.BlockSpec((1, s, SLAB), lambda bi, i: (bi, 0, 0), pipeline_mode=pl.Buffered(1)),
            _const_spec((2 * kblk, kblk)),
        ],
        out_specs=pl.BlockSpec((1, qblk, SLAB), lambda bi, i: (bi, i, 0)),
        out_shape=jax.ShapeDtypeStruct((b, s, SLAB), F32),
        scratch_shapes=[pltpu.VMEM((qblk, SLAB), F32),
                        pltpu.VMEM((N_HEADS, qblk, kblk), F32),
                        pltpu.VMEM((N_HEADS, qblk, SLAB), BF16),
                        pltpu.VMEM((N_HEADS, qblk, kblk), F32)],
        compiler_params=_params(("arbitrary", "arbitrary")),
        name="stick_breaking",
    )(q, k, v, uu)


DIL_TILE = DIL_QBLK * max(r for _, r in DIL_PATTERNS)
DIL_UNROLL = 2


def _dil_kernel(q_ref, kp_ref, kc_ref, vp_ref, vc_ref, bias_ref, o_ref,
                k_all, v_all, num_s, den_s, mx_s):
    n = pl.program_id(1)
    t_len = DIL_TILE
    qb = DIL_QBLK
    k_all[:, 0:t_len, :] = kp_ref[0]
    k_all[:, t_len:2 * t_len, :] = kc_ref[0]
    v_all[:, 0:t_len, :] = vp_ref[0]
    v_all[:, t_len:2 * t_len, :] = vc_ref[0]
    num_s[...] = jnp.zeros_like(num_s)
    den_s[...] = jnp.zeros_like(den_s)
    mx_s[...] = jnp.full(mx_s.shape, NEG_BIG, F32)
    lane = lax.broadcasted_iota(jnp.int32, (1, LANES), 1)
    head_masks = [_head_of_lane(lane) == hh for hh in range(HEADS_PER_HALF)]

    for pi, (window, r) in enumerate(DIL_PATTERNS):
        assert window // r == qb
        last = pi == len(DIL_PATTERNS) - 1
        shift = r.bit_length() - 1
        tiles = t_len // qb

        def rows_of(start, size, r=r):
            return pl.ds(start, size) if r == 1 else pl.ds(start, size, stride=r)

        def tile_group(it, carry, r=r, shift=shift, last=last, rows_of=rows_of):
            q_rows, k_rows, bias = [], [], []
            for u in range(DIL_UNROLL):
                idx = it * DIL_UNROLL + u
                t = jnp.right_shift(idx, shift)
                j = jnp.bitwise_and(idx, r - 1)
                q0 = t * (qb * r) + j
                has_prev = jnp.logical_or(n > 0, t > 0)
                bias.append(bias_ref[jnp.where(has_prev, 0, 1)])
                q_rows.append(rows_of(q0, qb))
                k_rows.append(rows_of(t_len + q0 - qb * r, 2 * qb))
            slabs = [(u, half) for u in range(DIL_UNROLL) for half in range(N_HALVES)]
            units = [(si, hh) for si in range(len(slabs)) for hh in range(HEADS_PER_HALF)]
            qs = [q_ref[0, half, q_rows[u], :] for u, half in slabs]
            kb = [k_all[half, k_rows[u], :].astype(BF16) for u, half in slabs]
            vb = [v_all[half, k_rows[u], :].astype(BF16) for u, half in slabs]
            sc = [_dot_nt(jnp.where(head_masks[hh], qs[si], 0.0).astype(BF16), kb[si]) + bias[slabs[si][0]]
                  for si, hh in units]
            mx = [jnp.max(x, axis=-1, keepdims=True) for x in sc]
            p = [jnp.exp(x - m) for x, m in zip(sc, mx)]
            den = [jnp.sum(x, axis=-1, keepdims=True) for x in p]
            pv = [_dot(x.astype(BF16), vb[si]) for x, (si, _) in zip(p, units)]
            for si, (u, half) in enumerate(slabs):
                first = si * HEADS_PER_HALF
                num_t, den_t, mx_t = pv[first], den[first], mx[first]
                for hh in range(1, HEADS_PER_HALF):
                    hm = head_masks[hh]
                    num_t = jnp.where(hm, pv[first + hh], num_t)
                    den_t = jnp.where(hm, den[first + hh], den_t)
                    mx_t = jnp.where(hm, mx[first + hh], mx_t)
                rows = q_rows[u]
                m_old = mx_s[half, rows, :]
                m_new = jnp.maximum(m_old, mx_t)
                a_old = jnp.exp(m_old - m_new)
                a_new = jnp.exp(mx_t - m_new)
                num_new = num_s[half, rows, :] * a_old + num_t * a_new
                den_new = den_s[half, rows, :] * a_old + den_t * a_new
                if last:
                    o_ref[0, half, rows, :] = num_new / den_new
                else:
                    mx_s[half, rows, :] = jnp.broadcast_to(m_new, (qb, LANES))
                    num_s[half, rows, :] = num_new
                    den_s[half, rows, :] = jnp.broadcast_to(den_new, (qb, LANES))
            return carry

        lax.fori_loop(0, tiles // DIL_UNROLL, tile_group, 0)


def _dilated(q, k, v):
    b, _, s, _ = q.shape
    qb = DIL_QBLK
    t_len = DIL_TILE
    a_idx = np.arange(qb)[:, None]
    k_idx = np.arange(2 * qb)[None, :]
    dist = a_idx + qb - k_idx
    band = (dist >= 0) & (dist <= qb)
    bias = np.stack([np.where(band, 0.0, NEG_BIG), np.where(band & (k_idx >= qb), 0.0, NEG_BIG)])
    cur = pl.BlockSpec((1, N_HALVES, t_len, LANES), lambda bi, n: (bi, 0, n, 0))
    prev = pl.BlockSpec((1, N_HALVES, t_len, LANES), lambda bi, n: (bi, 0, jnp.maximum(n - 1, 0), 0))
    stat = pltpu.VMEM((N_HALVES, t_len, LANES), F32)
    both = pltpu.VMEM((N_HALVES, 2 * t_len, LANES), F32)
    return pl.pallas_call(
        _dil_kernel,
        grid=(b, s // t_len),
        in_specs=[cur, prev, cur, prev, cur, _const_spec((2, qb, 2 * qb))],
        out_specs=cur,
        out_shape=jax.ShapeDtypeStruct((b, N_HALVES, s, LANES), F32),
        scratch_shapes=[both, both, stat, stat, stat],
        compiler_params=_params(("arbitrary", "arbitrary")),
        name="dilated",
    )(q, k, k, v, v, jnp.asarray(bias, F32))


HGRN_CHUNK = 128
HGRN_LEVELS = HGRN_CHUNK.bit_length() - 1
HGRN_TILE = 256
SUBLANES = 8


def _neg_abs_split_distance(g, log_f, half):
    c, w = g.shape
    block = 2 * half
    if block >= SUBLANES:
        g3 = g.reshape(c // block, block, w)
        return -jnp.abs(g3 - g3[:, half - 1:half, :]).reshape(c, w)
    pos = jnp.bitwise_and(lax.broadcasted_iota(jnp.int32, (c, 1), 0), block - 1)
    if half == 1:
        return jnp.where(pos == 1, log_f, 0.0)
    assert half == 2
    nxt = pltpu.roll(log_f, c - 1, 0)
    prv = pltpu.roll(log_f, 1, 0)
    return jnp.where(pos == 0, nxt, jnp.where(pos == 1, 0.0, jnp.where(pos == 2, log_f, log_f + prv)))


def _hgrn_kernel(q_ref, f_ref, i_ref, g_ref, lbl_ref, ng_ref, tril_ref, lvl_ref, o_ref,
                 state_ref, *, layer):
    @pl.when(pl.program_id(1) == 0)
    def _():
        state_ref[...] = jnp.zeros_like(state_ref)

    logits = lbl_ref[...]
    ex = jnp.exp(logits - jnp.max(logits, axis=0, keepdims=True))
    sm = ex / jnp.sum(ex, axis=0, keepdims=True)
    lb = jnp.zeros((1, HGRN_W), F32)
    for l in range(1, layer + 1):
        lb = lb + sm[l:l + 1, :]
    lb = jnp.clip(lb, 0.0, 1.0 - EPS)
    lb_floor = jnp.maximum(lb, LB_FLOOR)
    log_lb = jnp.log(lb_floor)
    one_m_lb = 1.0 - lb
    log_one_m_lb = jnp.log(one_m_lb)

    c = HGRN_CHUNK
    norm_gain = ng_ref[...]

    def chunk(ci, carry):
        r0 = 0 if HGRN_TILE == c else pl.multiple_of(ci * c, c)
        zq = q_ref[0, pl.ds(r0, c), :]
        zf = f_ref[0, pl.ds(r0, c), :]
        val = i_ref[0, pl.ds(r0, c), :]
        zg = g_ref[0, pl.ds(r0, c), :]

        bterm = log_one_m_lb + _log_sigmoid(zf)
        log_f = jnp.maximum(log_lb, bterm) + jnp.log(1.0 + jnp.exp(-jnp.abs(log_lb - bterm)))
        kk = one_m_lb * _sigmoid(-zf) - (lb_floor - lb)
        qq = zq * _sigmoid(zq)

        tril = tril_ref[...]
        lf_hi, lf_lo = _split_bf16(log_f)
        g = _dot(tril, lf_hi) + _dot(tril, lf_lo)
        g_last = g[c - 1:c, :]
        q_dec = (qq * jnp.exp(g)).astype(BF16)
        k_dec = (kk * jnp.exp(g_last - g)).astype(BF16)

        q_lvl = [qq.astype(BF16)]
        k_lvl = [kk.astype(BF16)]
        for lv in range(HGRN_LEVELS):
            decay = jnp.exp(_neg_abs_split_distance(g, log_f, 1 << lv).astype(BF16))
            q_lvl.append(q_lvl[0] * decay)
            k_lvl.append(k_lvl[0] * decay)

        lvl = lvl_ref[...]
        heads = range(HGRN_HEADS)
        lanes = [slice(hd * HGRN_D, (hd + 1) * HGRN_D) for hd in heads]
        st = [state_ref[hd] for hd in heads]
        o_inter = [_dot_nt(q_dec[:, lanes[hd]], st[hd].astype(BF16)) for hd in heads]
        scores = [[_dot_nt(q_lvl[lv][:, lanes[hd]], k_lvl[lv][:, lanes[hd]])
                   for lv in range(HGRN_LEVELS + 1)] for hd in heads]
        for hd in heads:
            state_ref[hd] = (st[hd] * jnp.exp(g_last[:, lanes[hd]])
                             + _dot_tn(val[:, lanes[hd]], k_dec[:, lanes[hd]]))
        outs = []
        level_masks = [lvl == lv for lv in range(HGRN_LEVELS + 1)]
        for hd in heads:
            p = jnp.zeros((c, c), F32)
            for lv in range(HGRN_LEVELS + 1):
                p = jnp.where(level_masks[lv], scores[hd][lv], p)
            o_h = _dot(p.astype(BF16), val[:, lanes[hd]]) + o_inter[hd]
            outs.append(o_h * _rms_scale(o_h) * norm_gain)
        o = jnp.concatenate(outs, axis=-1) * (zg * _sigmoid(zg))
        o_ref[0, pl.ds(r0, c), :] = o
        return carry

    if HGRN_TILE == c:
        chunk(0, 0)
    else:
        lax.fori_loop(0, HGRN_TILE // c, chunk, 0, unroll=True)


def _hgrn_tables():
    c = HGRN_CHUNK
    idx = np.arange(c)
    tril = idx[:, None] >= idx[None, :]
    diff = idx[:, None] ^ idx[None, :]
    lvl = np.where(idx[:, None] > idx[None, :], np.floor(np.log2(np.maximum(diff, 1))) + 1, -1)
    lvl = np.where(idx[:, None] == idx[None, :], 0, lvl).astype(np.int32)
    return jnp.asarray(tril, BF16), jnp.asarray(lvl)


def _hgrn(qh, fh, ih, gh, lb_logits, norm_gain, layer):
    b, s, _ = qh.shape
    t = HGRN_TILE
    c = HGRN_CHUNK
    depth = lb_logits.shape[0]
    tril, lvl = _hgrn_tables()
    tok = pl.BlockSpec((1, t, HGRN_W), lambda bi, i: (bi, i, 0))
    return pl.pallas_call(
        functools.partial(_hgrn_kernel, layer=layer),
        grid=(b, s // t),
        in_specs=[tok, tok, tok, tok,
                  _const_spec((depth, HGRN_W)),
                  _const_spec((1, HGRN_D)),
                  _const_spec((c, c)),
                  _const_spec((c, c))],
        out_specs=tok,
        out_shape=jax.ShapeDtypeStruct((b, s, HGRN_W), F32),
        scratch_shapes=[pltpu.VMEM((HGRN_HEADS, HGRN_D, HGRN_D), F32)],
        compiler_params=_params(("arbitrary", "arbitrary")),
        name="hgrn2",
    )(qh, fh, ih, gh, lb_logits.astype(F32), norm_gain.reshape(1, HGRN_D), tril, lvl)


def _outproj_kernel(x_ref, mod_ref, oa_ref, od_ref, oh_ref, w_ref, o_ref):
    y = _dot(oa_ref[0].astype(BF16), w_ref[0:SLAB, :])
    for half in range(N_HALVES):
        lo = SLAB + half * LANES
        y += _dot(od_ref[0, half].astype(BF16), w_ref[lo:lo + LANES, :])
    y += _dot(oh_ref[0].astype(BF16), w_ref[2 * SLAB:, :])
    gate = mod_ref[0, 5:6, :]
    o_ref[0] = x_ref[0] + gate * y


def _outproj(x, mod_l, o_a, o_d, o_h, w_out, tm=512):
    b, s, d = x.shape
    w = w_out.astype(BF16)

    def tok(width):
        return pl.BlockSpec((1, tm, width), lambda bi, i: (bi, i, 0))

    return pl.pallas_call(
        _outproj_kernel,
        grid=(b, s // tm),
        in_specs=[tok(d), pl.BlockSpec((1, N_MOD, d), lambda bi, i: (bi, 0, 0)), tok(SLAB),
                  pl.BlockSpec((1, N_HALVES, tm, LANES), lambda bi, i: (bi, 0, i, 0)),
                  tok(HGRN_W), _const_spec(w.shape)],
        out_specs=tok(d),
        out_shape=jax.ShapeDtypeStruct((b, s, d), F32),
        compiler_params=_params(("arbitrary", "arbitrary")),
        name="mixer_outproj",
    )(x, mod_l, o_a, o_d, o_h, w)


def kernel(x, c, w_mod, b_mod, ffn1_w_gate, ffn1_w_up, ffn1_w_down, w_in, w_out, q_norm_g, k_norm_g,
           hgrn_norm_g, hgrn_lb_logits, ffn2_w_gate, ffn2_w_up, ffn2_w_down):
    b, s, d = x.shape
    depth = w_mod.shape[0]
    assert s % DIL_TILE == 0 and s % HGRN_TILE == 0 and s % SB_QBLK == 0
    mod = _modulation(c, w_mod, b_mod).reshape(depth, b, N_MOD, d)
    cos, sin = _rope_tables(s)
    for l in range(depth):
        x = _ffn(x, mod[l], ffn1_w_gate[l], ffn1_w_up[l], ffn1_w_down[l], row0=0)
        qa, ka, va, qd, kd, vd, qh, fh, ih, gh = _inproj(
            x, mod[l], w_in[l], q_norm_g[l], k_norm_g[l], cos, sin)
        o_a = _stick_breaking(qa, ka, va)
        o_d = _dilated(qd, kd, vd)
        o_h = _hgrn(qh, fh, ih, gh, hgrn_lb_logits, hgrn_norm_g[l], layer=l)
        x = _outproj(x, mod[l], o_a, o_d, o_h, w_out[l])
        x = _ffn(x, mod[l], ffn2_w_gate[l], ffn2_w_up[l], ffn2_w_down[l], row0=6)
    return x
```

```python
import functools

import jax
import jax.numpy as jnp
import numpy as np
from jax import lax
from jax.experimental import pallas as pl
from jax.experimental.pallas import tpu as pltpu

F32 = jnp.float32
BF16 = jnp.bfloat16

HEAD_DIM = 64
N_HEADS = 4
SLAB = N_HEADS * HEAD_DIM
LANES = 128
N_HALVES = SLAB // LANES
HEADS_PER_HALF = LANES // HEAD_DIM
HGRN_HEADS = 4
HGRN_D = 128
HGRN_W = HGRN_HEADS * HGRN_D
N_MOD = 9
EPS = 1e-6
LB_FLOOR = 1e-30
NEG_BIG = -1e30
HALF_STEP = 0.5
ROPE_THETA = 10000.0
DIL_PATTERNS = ((128, 1), (512, 4), (2048, 16))
DIL_QBLK = 128

OFF_QA, OFF_KA, OFF_VA = 0, 256, 512
OFF_QD, OFF_KD, OFF_VD = 768, 1024, 1280
OFF_QH, OFF_FH, OFF_IH, OFF_GH = 1536, 2048, 2560, 3072

VMEM_LIMIT = 56 * 1024 * 1024

SB_DEAD = 104.0


def _params(sem):
    return pltpu.CompilerParams(dimension_semantics=sem, vmem_limit_bytes=VMEM_LIMIT)


def _const_spec(shape):
    nd = len(shape)
    return pl.BlockSpec(shape, lambda *_: (0,) * nd, pipeline_mode=pl.Buffered(1))


def _split_bf16(x):
    hi = x.astype(BF16)
    lo = (x - hi.astype(F32)).astype(BF16)
    return hi, lo


def _dot(a, b):
    return jnp.dot(a, b, preferred_element_type=F32)


def _dot_nt(a, b):
    return lax.dot_general(a, b, (((1,), (1,)), ((), ())), preferred_element_type=F32)


def _dot_tn(a, b):
    return lax.dot_general(a, b, (((0,), (0,)), ((), ())), preferred_element_type=F32)


def _sigmoid(x):
    return 1.0 / (1.0 + jnp.exp(-x))


def _log_sigmoid(x):
    return jnp.minimum(x, 0.0) - jnp.log(1.0 + jnp.exp(-jnp.abs(x)))


def _rms_scale(x):
    return lax.rsqrt(jnp.mean(x * x, axis=-1, keepdims=True) + EPS)


def _head_of_lane(lane):
    return jnp.right_shift(lane, HEAD_DIM.bit_length() - 1)


def _mod_kernel(c_ref, w_ref, b_ref, o_ref):
    c = c_ref[...]
    sc = c * _sigmoid(c)
    sc_hi, sc_lo = _split_bf16(sc)
    w = w_ref[0]
    w_hi, w_lo = _split_bf16(w)
    o_ref[0] = _dot(sc_hi, w_hi) + (_dot(sc_hi, w_lo) + _dot(sc_lo, w_hi)) + b_ref[0]


def _modulation(c, w_mod, b_mod):
    depth, d, n = w_mod.shape
    rows = 16
    nb = c.shape[0]
    assert nb <= rows
    c = jnp.pad(c, ((0, rows - nb), (0, 0)))
    b = rows
    tn = 1152
    out = pl.pallas_call(
        _mod_kernel,
        grid=(depth, n // tn),
        in_specs=[
            pl.BlockSpec((b, d), lambda l, j: (0, 0)),
            pl.BlockSpec((1, d, tn), lambda l, j: (l, 0, j)),
            pl.BlockSpec((1, 1, tn), lambda l, j: (l, 0, j)),
        ],
        out_specs=pl.BlockSpec((1, b, tn), lambda l, j: (l, 0, j)),
        out_shape=jax.ShapeDtypeStruct((depth, b, n), F32),
        compiler_params=_params(("arbitrary", "arbitrary")),
        name="adaln_mod",
    )(c, w_mod, b_mod.reshape(depth, 1, n))
    return out[:, :nb]


FFN_CHUNK = 256


def _ffn_half_step(xo_ref, mod_ref, row0, wg_ref, wu_ref, wd_ref, h_scr, acc_scr):
    x = xo_ref[0]
    shift = mod_ref[0, row0:row0 + 1, :]
    scale = mod_ref[0, row0 + 1:row0 + 2, :]
    gate = mod_ref[0, row0 + 2:row0 + 3, :]
    h = x * _rms_scale(x) * (1.0 + scale) + shift
    h_scr[...] = h.astype(BF16)
    for c in range(wg_ref.shape[1] // FFN_CHUNK):
        cols = slice(c * FFN_CHUNK, (c + 1) * FFN_CHUNK)
        g = _dot(h_scr[...], wg_ref[:, cols])
        u = _dot(h_scr[...], wu_ref[:, cols])
        a = (g * _sigmoid(g) * u).astype(BF16)
        y = _dot(a, wd_ref[cols, :])
        if c == 0:
            acc_scr[...] = y
        else:
            acc_scr[...] += y
    xo_ref[0] = xo_ref[0] + (HALF_STEP * gate) * acc_scr[...]


def _ffn_kernel(x_ref, mod_ref, wg_ref, wu_ref, wd_ref, o_ref, h_scr, acc_scr, *, row0):
    o_ref[0] = x_ref[0]
    _ffn_half_step(o_ref, mod_ref, row0, wg_ref, wu_ref, wd_ref, h_scr, acc_scr)


def _layer_spec(shape, layer):
    nd = len(shape)
    return pl.BlockSpec((None,) + tuple(shape[1:]), lambda *_: (layer,) + (0,) * (nd - 1),
                        pipeline_mode=pl.Buffered(1))


def _ffn(x, mod_l, w_gate, w_up, w_down, layer, row0, tm=1024):
    b, s, d = x.shape
    return pl.pallas_call(
        functools.partial(_ffn_kernel, row0=row0),
        grid=(b, s // tm),
        in_specs=[
            pl.BlockSpec((1, tm, d), lambda bi, i: (bi, i, 0)),
            pl.BlockSpec((1, N_MOD, d), lambda bi, i: (bi, 0, 0)),
            _layer_spec(w_gate.shape, layer),
            _layer_spec(w_up.shape, layer),
            _layer_spec(w_down.shape, layer),
        ],
        out_specs=pl.BlockSpec((1, tm, d), lambda bi, i: (bi, i, 0)),
        out_shape=jax.ShapeDtypeStruct((b, s, d), F32),
        scratch_shapes=[pltpu.VMEM((tm, d), BF16), pltpu.VMEM((tm, d), F32)],
        compiler_params=_params(("arbitrary", "arbitrary")),
        name="ffn",
    )(x, mod_l, w_gate, w_up, w_down)


def _swap_halves(x):
    n = x.shape[-1]
    lane = lax.broadcasted_iota(jnp.int32, x.shape, x.ndim - 1)
    up = pltpu.roll(x, n - HEAD_DIM // 2, x.ndim - 1)
    down = pltpu.roll(x, HEAD_DIM // 2, x.ndim - 1)
    return jnp.where(jnp.bitwise_and(lane, HEAD_DIM - 1) < HEAD_DIM // 2, up, down)


def _qk_norm_rope(x, gain, cos, sin_signed, head_mean):
    sq_hi, sq_lo = _split_bf16(x * x)
    ms = _dot(sq_hi, head_mean) + _dot(sq_lo, head_mean)
    xn = x * lax.rsqrt(ms + EPS) * gain
    return xn * cos + _swap_halves(xn) * sin_signed


def _inproj_kernel(x_ref, mod_ref, w_ref, cos_ref, sin_ref, qg_ref, kg_ref, hm_ref,
                   qa_ref, ka_ref, va_ref, qd_ref, kd_ref, vd_ref, qh_ref, fh_ref, ih_ref, gh_ref):
    x = x_ref[0]
    shift = mod_ref[0, 3:4, :]
    scale = mod_ref[0, 4:5, :]
    h = (x * _rms_scale(x) * (1.0 + scale) + shift).astype(BF16)

    def proj(off, width):
        return _dot(h, w_ref[:, off:off + width])

    def put_halves(ref, val):
        for half in range(N_HALVES):
            ref[0, half] = val[:, half * LANES:(half + 1) * LANES]

    qa_ref[0] = proj(OFF_QA, SLAB)
    ka_ref[0] = proj(OFF_KA, SLAB).astype(BF16)
    va_ref[0] = proj(OFF_VA, SLAB).astype(BF16)
    cos = cos_ref[...]
    sin = sin_ref[...]
    hm = hm_ref[...]
    qd = _qk_norm_rope(proj(OFF_QD, SLAB), qg_ref[...], cos, sin, hm)
    put_halves(qd_ref, qd * (HEAD_DIM ** -0.5))
    put_halves(kd_ref, _qk_norm_rope(proj(OFF_KD, SLAB), kg_ref[...], cos, sin, hm))
    put_halves(vd_ref, proj(OFF_VD, SLAB))
    qh_ref[0] = proj(OFF_QH, HGRN_W)
    fh_ref[0] = proj(OFF_FH, HGRN_W)
    ih_ref[0] = proj(OFF_IH, HGRN_W).astype(BF16)
    gh_ref[0] = proj(OFF_GH, HGRN_W)


def _rope_tables(s):
    half = HEAD_DIM // 2
    inv_freq = ROPE_THETA ** (-jnp.arange(half, dtype=F32) * 2.0 / HEAD_DIM)
    ang = jnp.arange(s, dtype=F32)[:, None] * inv_freq[None, :]
    cos, sin = jnp.cos(ang), jnp.sin(ang)
    cos_full = jnp.tile(jnp.concatenate([cos, cos], axis=-1), (1, N_HEADS))
    sin_signed = jnp.tile(jnp.concatenate([-sin, sin], axis=-1), (1, N_HEADS))
    return cos_full, sin_signed


def _inproj(x, mod_l, w_in, layer, q_gain, k_gain, cos, sin, tm=512):
    b, s, d = x.shape
    qg = jnp.tile(q_gain.reshape(1, HEAD_DIM), (1, N_HEADS))
    kg = jnp.tile(k_gain.reshape(1, HEAD_DIM), (1, N_HEADS))
    head_id = np.arange(SLAB) // HEAD_DIM
    head_mean = jnp.asarray(np.where(head_id[:, None] == head_id[None, :], 1.0 / HEAD_DIM, 0.0), BF16)

    def tok(width):
        return pl.BlockSpec((1, tm, width), lambda bi, i: (bi, i, 0))

    def out(width, dt):
        return jax.ShapeDtypeStruct((b, s, width), dt)

    halves = pl.BlockSpec((1, N_HALVES, tm, LANES), lambda bi, i: (bi, 0, i, 0))
    halves_out = jax.ShapeDtypeStruct((b, N_HALVES, s, LANES), F32)

    return pl.pallas_call(
        _inproj_kernel,
        grid=(b, s // tm),
        in_specs=[
            tok(d),
            pl.BlockSpec((1, N_MOD, d), lambda bi, i: (bi, 0, 0)),
            _layer_spec(w_in.shape, layer),
            pl.BlockSpec((tm, SLAB), lambda bi, i: (i, 0)),
            pl.BlockSpec((tm, SLAB), lambda bi, i: (i, 0)),
            _const_spec((1, SLAB)),
            _const_spec((1, SLAB)),
            _const_spec((SLAB, SLAB)),
        ],
        out_specs=[tok(SLAB)] * 3 + [halves] * 3 + [tok(HGRN_W)] * 4,
        out_shape=[out(SLAB, F32), out(SLAB, BF16), out(SLAB, BF16),
                   halves_out, halves_out, halves_out,
                   out(HGRN_W, F32), out(HGRN_W, F32), out(HGRN_W, BF16), out(HGRN_W, F32)],
        compiler_params=_params(("arbitrary", "arbitrary")),
        name="mixer_inproj",
    )(x, mod_l, w_in, cos, sin, qg, kg, head_mean)


SB_QBLK = 256
SB_KBLK = 128


def _sb_kernel(q_ref, k_ref, v_ref, uu_ref, o_ref, acc_ref, run_ref, qm_ref, z_ref):
    i = pl.program_id(1)
    qblk, kblk = SB_QBLK, SB_KBLK
    heads = range(N_HEADS)
    lane = lax.broadcasted_iota(jnp.int32, (1, SLAB), 1)
    head_masks = [_head_of_lane(lane) == hd for hd in heads]
    qm_ref[...] = (q_ref[0] * (HEAD_DIM ** -0.5)).astype(BF16)
    acc_ref[...] = jnp.zeros_like(acc_ref)
    run_ref[...] = jnp.zeros_like(run_ref)

    def per_head_rows(x):
        return jnp.concatenate([jnp.where(head_masks[hd], x, jnp.zeros_like(x)) for hd in heads], axis=0)

    def scores(j):
        off = pl.multiple_of(jnp.maximum(j, 0) * kblk, kblk)
        z_all = _dot_nt(qm_ref[...], per_head_rows(k_ref[0, pl.ds(off, kblk), :]))
        return [z_all[:, hd * kblk:(hd + 1) * kblk] for hd in heads]

    def key_block(j, z, causal_below, first_row=0):
        off = pl.multiple_of(j * kblk, kblk)
        v = v_ref[0, pl.ds(off, kblk), :]
        uu = uu_ref[...]
        rows = slice(first_row, qblk)
        causal = None
        if causal_below is not None:
            shape = (qblk - first_row, kblk)
            col_minus_row = (lax.broadcasted_iota(jnp.int32, shape, 1)
                             - lax.broadcasted_iota(jnp.int32, shape, 0))
            causal = col_minus_row < causal_below + first_row
        zr = [z[hd][rows] for hd in heads]
        ls = [_log_sigmoid(zr[hd]) for hd in heads]
        lnb = [ls[hd] - zr[hd] for hd in heads]
        if causal is not None:
            lnb = [jnp.where(causal, x, 0.0) for x in lnb]
        tail = []
        for hd in heads:
            lnb_hi, lnb_lo = _split_bf16(lnb[hd])
            tail.append(_dot(jnp.concatenate([lnb_hi, lnb_lo], axis=1), uu))
        run = [run_ref[hd, rows, :] for hd in heads]
        w = [jnp.exp(ls[hd] + tail[hd] + run[hd]) for hd in heads]
        if causal is not None:
            w = [jnp.where(causal, x, 0.0) for x in w]
        pv = _dot(jnp.concatenate([w[hd].astype(BF16) for hd in heads], axis=1), per_head_rows(v))
        run_max = None
        for hd in heads:
            new_run = run[hd] + (tail[hd][:, 0:1] + lnb[hd][:, 0:1])
            run_ref[hd, rows, :] = new_run
            run_max = new_run if run_max is None else jnp.maximum(run_max, new_run)
        acc_ref[rows, :] += pv
        return jnp.max(run_max) > -SB_DEAD

    per_q = qblk // kblk
    top = i * per_q + per_q - 1
    z_now = scores(top)
    alive = None
    for d in range(per_q):
        z_next = scores(top - d - 1)
        if d == per_q - 1:
            for hd in heads:
                z_ref[hd] = z_next[hd]
        alive = key_block(top - d, z_now, (d + 1 - per_q) * kblk, first_row=(per_q - 1 - d) * kblk)
        z_now = z_next

    def cond(carry):
        j, alive = carry
        return jnp.logical_and(j >= 0, alive)

    def body(carry):
        j, _ = carry
        z = [z_ref[hd] for hd in heads]
        z_ahead = scores(j - 1)
        alive = key_block(j, z, None)
        for hd in heads:
            z_ref[hd] = z_ahead[hd]
        return j - 1, alive

    lax.while_loop(cond, body, (top - per_q, alive))
    o_ref[0] = acc_ref[...]


def _stick_breaking(q, k, v):
    b, s, _ = q.shape
    qblk, kblk = SB_QBLK, SB_KBLK
    idx = np.arange(kblk)
    u = (idx[:, None] > idx[None, :])
    uu = jnp.asarray(np.concatenate([u, u], axis=0), BF16)
    return pl.pallas_call(
        _sb_kernel,
        grid=(b, s // qblk),
        in_specs=[
            pl.BlockSpec((1, qblk, SLAB), lambda bi, i: (bi, i, 0)),
            pl.BlockSpec((1, s, SLAB), lambda bi, i: (bi, 0, 0), pipeline_mode=pl.Buffered(1)),
            pl.BlockSpec((1, s, SLAB), lambda bi, i: (bi, 0, 0), pipeline_mode=pl.Buffered(1)),
            _const_spec((2 * kblk, kblk)),
        ],
        out_specs=pl.BlockSpec((1, qblk, SLAB), lambda bi, i: (bi, i, 0)),
        out_shape=jax.ShapeDtypeStruct((b, s, SLAB), F32),
        scratch_shapes=[pltpu.VMEM((qblk, SLAB), F32),
                        pltpu.VMEM((N_HEADS, qblk, kblk), F32),
                        pltpu.VMEM((qblk, SLAB), BF16),
                        pltpu.VMEM((N_HEADS, qblk, kblk), F32)],
        compiler_params=_params(("arbitrary", "arbitrary")),
        name="stick_breaking",
    )(q, k, v, uu)


DIL_TILE = DIL_QBLK * max(r for _, r in DIL_PATTERNS)
DIL_UNROLL = 2


def _dil_kernel(q_ref, kp_ref, kc_ref, vp_ref, vc_ref, bias_ref, o_ref,
                k_all, v_all, num_s, den_s, mx_s):
    n = pl.program_id(1)
    t_len = DIL_TILE
    qb = DIL_QBLK
    k_all[:, 0:t_len, :] = kp_ref[0]
    k_all[:, t_len:2 * t_len, :] = kc_ref[0]
    v_all[:, 0:t_len, :] = vp_ref[0]
    v_all[:, t_len:2 * t_len, :] = vc_ref[0]
    num_s[...] = jnp.zeros_like(num_s)
    den_s[...] = jnp.zeros_like(den_s)
    mx_s[...] = jnp.full(mx_s.shape, NEG_BIG, F32)
    lane = lax.broadcasted_iota(jnp.int32, (1, LANES), 1)
    head_masks = [_head_of_lane(lane) == hh for hh in range(HEADS_PER_HALF)]

    for pi, (window, r) in enumerate(DIL_PATTERNS):
        assert window // r == qb
        last = pi == len(DIL_PATTERNS) - 1
        shift = r.bit_length() - 1
        tiles = t_len // qb

        def rows_of(start, size, r=r):
            return pl.ds(start, size) if r == 1 else pl.ds(start, size, stride=r)

        def tile_group(it, carry, r=r, shift=shift, last=last, rows_of=rows_of):
            q_rows, k_rows, bias = [], [], []
            for u in range(DIL_UNROLL):
                idx = it * DIL_UNROLL + u
                t = jnp.right_shift(idx, shift)
                j = jnp.bitwise_and(idx, r - 1)
                q0 = t * (qb * r) + j
                has_prev = jnp.logical_or(n > 0, t > 0)
                bias.append(bias_ref[jnp.where(has_prev, 0, 1)])
                q_rows.append(rows_of(q0, qb))
                k_rows.append(rows_of(t_len + q0 - qb * r, 2 * qb))
            slabs = [(u, half) for u in range(DIL_UNROLL) for half in range(N_HALVES)]
            units = [(si, hh) for si in range(len(slabs)) for hh in range(HEADS_PER_HALF)]
            qs = [q_ref[0, half, q_rows[u], :] for u, half in slabs]
            kb = [k_all[half, k_rows[u], :].astype(BF16) for u, half in slabs]
            vb = [v_all[half, k_rows[u], :].astype(BF16) for u, half in slabs]
            sc = [_dot_nt(jnp.where(head_masks[hh], qs[si], 0.0).astype(BF16), kb[si]) + bias[slabs[si][0]]
                  for si, hh in units]
            mx = [jnp.max(x, axis=-1, keepdims=True) for x in sc]
            p = [jnp.exp(x - m) for x, m in zip(sc, mx)]
            den = [jnp.sum(x, axis=-1, keepdims=True) for x in p]
            pv = [_dot(x.astype(BF16), vb[si]) for x, (si, _) in zip(p, units)]
            for si, (u, half) in enumerate(slabs):
                first = si * HEADS_PER_HALF
                num_t, den_t, mx_t = pv[first], den[first], mx[first]
                for hh in range(1, HEADS_PER_HALF):
                    hm = head_masks[hh]
                    num_t = jnp.where(hm, pv[first + hh], num_t)
                    den_t = jnp.where(hm, den[first + hh], den_t)
                    mx_t = jnp.where(hm, mx[first + hh], mx_t)
                rows = q_rows[u]
                m_old = mx_s[half, rows, :]
                m_new = jnp.maximum(m_old, mx_t)
                a_old = jnp.exp(m_old - m_new)
                a_new = jnp.exp(mx_t - m_new)
                num_new = num_s[half, rows, :] * a_old + num_t * a_new
                den_new = den_s[half, rows, :] * a_old + den_t * a_new
                if last:
                    o_ref[0, half, rows, :] = num_new / den_new
                else:
                    mx_s[half, rows, :] = jnp.broadcast_to(m_new, (qb, LANES))
                    num_s[half, rows, :] = num_new
                    den_s[half, rows, :] = jnp.broadcast_to(den_new, (qb, LANES))
            return carry

        lax.fori_loop(0, tiles // DIL_UNROLL, tile_group, 0)


def _dilated(q, k, v):
    b, _, s, _ = q.shape
    qb = DIL_QBLK
    t_len = DIL_TILE
    a_idx = np.arange(qb)[:, None]
    k_idx = np.arange(2 * qb)[None, :]
    dist = a_idx + qb - k_idx
    band = (dist >= 0) & (dist <= qb)
    bias = np.stack([np.where(band, 0.0, NEG_BIG), np.where(band & (k_idx >= qb), 0.0, NEG_BIG)])
    cur = pl.BlockSpec((1, N_HALVES, t_len, LANES), lambda bi, n: (bi, 0, n, 0))
    prev = pl.BlockSpec((1, N_HALVES, t_len, LANES), lambda bi, n: (bi, 0, jnp.maximum(n - 1, 0), 0))
    stat = pltpu.VMEM((N_HALVES, t_len, LANES), F32)
    both = pltpu.VMEM((N_HALVES, 2 * t_len, LANES), F32)
    return pl.pallas_call(
        _dil_kernel,
        grid=(b, s // t_len),
        in_specs=[cur, prev, cur, prev, cur, _const_spec((2, qb, 2 * qb))],
        out_specs=cur,
        out_shape=jax.ShapeDtypeStruct((b, N_HALVES, s, LANES), F32),
        scratch_shapes=[both, both, stat, stat, stat],
        compiler_params=_params(("arbitrary", "arbitrary")),
        name="dilated",
    )(q, k, k, v, v, jnp.asarray(bias, F32))


HGRN_CHUNK = 128
HGRN_LEVELS = HGRN_CHUNK.bit_length() - 1
HGRN_TILE = 256
SUBLANES = 8


def _neg_abs_split_distance(g, log_f, half):
    c, w = g.shape
    block = 2 * half
    if block >= SUBLANES:
        g3 = g.reshape(c // block, block, w)
        return -jnp.abs(g3 - g3[:, half - 1:half, :]).reshape(c, w)
    pos = jnp.bitwise_and(lax.broadcasted_iota(jnp.int32, (c, 1), 0), block - 1)
    if half == 1:
        return jnp.where(pos == 1, log_f, 0.0)
    assert half == 2
    nxt = pltpu.roll(log_f, c - 1, 0)
    prv = pltpu.roll(log_f, 1, 0)
    return jnp.where(pos == 0, nxt, jnp.where(pos == 1, 0.0, jnp.where(pos == 2, log_f, log_f + prv)))


def _hgrn_kernel(q_ref, f_ref, i_ref, g_ref, lbl_ref, ng_ref, tril_ref, lvl_ref, o_ref,
                 state_ref, *, layer):
    @pl.when(pl.program_id(1) == 0)
    def _():
        state_ref[...] = jnp.zeros_like(state_ref)

    logits = lbl_ref[...]
    ex = jnp.exp(logits - jnp.max(logits, axis=0, keepdims=True))
    sm = ex / jnp.sum(ex, axis=0, keepdims=True)
    lb = jnp.zeros((1, HGRN_W), F32)
    for l in range(1, layer + 1):
        lb = lb + sm[l:l + 1, :]
    lb = jnp.clip(lb, 0.0, 1.0 - EPS)
    lb_floor = jnp.maximum(lb, LB_FLOOR)
    log_lb = jnp.log(lb_floor)
    one_m_lb = 1.0 - lb
    log_one_m_lb = jnp.log(one_m_lb)

    c = HGRN_CHUNK
    norm_gain = ng_ref[...]

    def chunk(ci, carry):
        r0 = 0 if HGRN_TILE == c else pl.multiple_of(ci * c, c)
        zq = q_ref[0, pl.ds(r0, c), :]
        zf = f_ref[0, pl.ds(r0, c), :]
        val = i_ref[0, pl.ds(r0, c), :]
        zg = g_ref[0, pl.ds(r0, c), :]

        bterm = log_one_m_lb + _log_sigmoid(zf)
        log_f = jnp.maximum(log_lb, bterm) + jnp.log(1.0 + jnp.exp(-jnp.abs(log_lb - bterm)))
        kk = one_m_lb * _sigmoid(-zf) - (lb_floor - lb)
        qq = zq * _sigmoid(zq)

        tril = tril_ref[...]
        lf_hi, lf_lo = _split_bf16(log_f)
        g = _dot(tril, lf_hi) + _dot(tril, lf_lo)
        g_last = g[c - 1:c, :]
        q_dec = (qq * jnp.exp(g)).astype(BF16)
        k_dec = (kk * jnp.exp(g_last - g)).astype(BF16)

        q_lvl = [qq.astype(BF16)]
        k_lvl = [kk.astype(BF16)]
        for lv in range(HGRN_LEVELS):
            decay = jnp.exp(_neg_abs_split_distance(g, log_f, 1 << lv).astype(BF16))
            q_lvl.append(q_lvl[0] * decay)
            k_lvl.append(k_lvl[0] * decay)

        lvl = lvl_ref[...]
        heads = range(HGRN_HEADS)
        lanes = [slice(hd * HGRN_D, (hd + 1) * HGRN_D) for hd in heads]
        st = [state_ref[hd] for hd in heads]
        o_inter = [_dot_nt(q_dec[:, lanes[hd]], st[hd].astype(BF16)) for hd in heads]
        scores = [[_dot_nt(q_lvl[lv][:, lanes[hd]], k_lvl[lv][:, lanes[hd]])
                   for lv in range(HGRN_LEVELS + 1)] for hd in heads]
        for hd in heads:
            state_ref[hd] = (st[hd] * jnp.exp(g_last[:, lanes[hd]])
                             + _dot_tn(val[:, lanes[hd]], k_dec[:, lanes[hd]]))
        outs = []
        level_masks = [lvl == lv for lv in range(HGRN_LEVELS + 1)]
        for hd in heads:
            p = jnp.zeros((c, c), F32)
            for lv in range(HGRN_LEVELS + 1):
                p = jnp.where(level_masks[lv], scores[hd][lv], p)
            o_h = _dot(p.astype(BF16), val[:, lanes[hd]]) + o_inter[hd]
            outs.append(o_h * _rms_scale(o_h) * norm_gain)
        o = jnp.concatenate(outs, axis=-1) * (zg * _sigmoid(zg))
        o_ref[0, pl.ds(r0, c), :] = o
        return carry

    if HGRN_TILE == c:
        chunk(0, 0)
    else:
        lax.fori_loop(0, HGRN_TILE // c, chunk, 0, unroll=True)


def _hgrn_tables():
    c = HGRN_CHUNK
    idx = np.arange(c)
    tril = idx[:, None] >= idx[None, :]
    diff = idx[:, None] ^ idx[None, :]
    lvl = np.where(idx[:, None] > idx[None, :], np.floor(np.log2(np.maximum(diff, 1))) + 1, -1)
    lvl = np.where(idx[:, None] == idx[None, :], 0, lvl).astype(np.int32)
    return jnp.asarray(tril, BF16), jnp.asarray(lvl)


def _hgrn(qh, fh, ih, gh, lb_logits, norm_gain, layer):
    b, s, _ = qh.shape
    t = HGRN_TILE
    c = HGRN_CHUNK
    depth = lb_logits.shape[0]
    tril, lvl = _hgrn_tables()
    tok = pl.BlockSpec((1, t, HGRN_W), lambda bi, i: (bi, i, 0))
    return pl.pallas_call(
        functools.partial(_hgrn_kernel, layer=layer),
        grid=(b, s // t),
        in_specs=[tok, tok, tok, tok,
                  _const_spec((depth, HGRN_W)),
                  _const_spec((1, HGRN_D)),
                  _const_spec((c, c)),
                  _const_spec((c, c))],
        out_specs=tok,
        out_shape=jax.ShapeDtypeStruct((b, s, HGRN_W), F32),
        scratch_shapes=[pltpu.VMEM((HGRN_HEADS, HGRN_D, HGRN_D), F32)],
        compiler_params=_params(("arbitrary", "arbitrary")),
        name="hgrn2",
    )(qh, fh, ih, gh, lb_logits.astype(F32), norm_gain.reshape(1, HGRN_D), tril, lvl)


def _outproj_ffn_kernel(x_ref, mod_ref, oa_ref, od_ref, oh_ref, wo_ref, wg_ref, wu_ref, wd_ref, o_ref,
                        h_scr, acc_scr):
    y = _dot(oa_ref[0].astype(BF16), wo_ref[0:SLAB, :])
    for half in range(N_HALVES):
        lo = SLAB + half * LANES
        y += _dot(od_ref[0, half].astype(BF16), wo_ref[lo:lo + LANES, :])
    y += _dot(oh_ref[0].astype(BF16), wo_ref[2 * SLAB:, :])
    gate = mod_ref[0, 5:6, :]
    o_ref[0] = x_ref[0] + gate * y
    _ffn_half_step(o_ref, mod_ref, 6, wg_ref, wu_ref, wd_ref, h_scr, acc_scr)


def _outproj_ffn(x, mod_l, o_a, o_d, o_h, w_out, w_gate, w_up, w_down, layer, tm=512):
    b, s, d = x.shape

    def tok(width):
        return pl.BlockSpec((1, tm, width), lambda bi, i: (bi, i, 0))

    return pl.pallas_call(
        _outproj_ffn_kernel,
        grid=(b, s // tm),
        in_specs=[tok(d), pl.BlockSpec((1, N_MOD, d), lambda bi, i: (bi, 0, 0)), tok(SLAB),
                  pl.BlockSpec((1, N_HALVES, tm, LANES), lambda bi, i: (bi, 0, i, 0)),
                  tok(HGRN_W), _layer_spec(w_out.shape, layer),
                  _layer_spec(w_gate.shape, layer), _layer_spec(w_up.shape, layer),
                  _layer_spec(w_down.shape, layer)],
        out_specs=tok(d),
        out_shape=jax.ShapeDtypeStruct((b, s, d), F32),
        scratch_shapes=[pltpu.VMEM((tm, d), BF16), pltpu.VMEM((tm, d), F32)],
        compiler_params=_params(("arbitrary", "arbitrary")),
        name="outproj_ffn",
    )(x, mod_l, o_a, o_d, o_h, w_out, w_gate, w_up, w_down)


def kernel(x, c, w_mod, b_mod, ffn1_w_gate, ffn1_w_up, ffn1_w_down, w_in, w_out, q_norm_g, k_norm_g,
           hgrn_norm_g, hgrn_lb_logits, ffn2_w_gate, ffn2_w_up, ffn2_w_down):
    b, s, d = x.shape
    depth = w_mod.shape[0]
    assert s % DIL_TILE == 0 and s % HGRN_TILE == 0 and s % SB_QBLK == 0
    mod = _modulation(c, w_mod, b_mod).reshape(depth, b, N_MOD, d)
    cos, sin = _rope_tables(s)
    ffn1 = [w.astype(BF16) for w in (ffn1_w_gate, ffn1_w_up, ffn1_w_down)]
    ffn2 = [w.astype(BF16) for w in (ffn2_w_gate, ffn2_w_up, ffn2_w_down)]
    w_in_b = w_in.astype(BF16)
    w_out_b = w_out.astype(BF16)
    for l in range(depth):
        x = _ffn(x, mod[l], *ffn1, layer=l, row0=0)
        qa, ka, va, qd, kd, vd, qh, fh, ih, gh = _inproj(
            x, mod[l], w_in_b, l, q_norm_g[l], k_norm_g[l], cos, sin)
        o_a = _stick_breaking(qa, ka, va)
        o_d = _dilated(qd, kd, vd)
        o_h = _hgrn(qh, fh, ih, gh, hgrn_lb_logits, hgrn_norm_g[l], layer=l)
        x = _outproj_ffn(x, mod[l], o_a, o_d, o_h, w_out_b, *ffn2, layer=l)
    return x
```

```python
import functools

import jax
import jax.numpy as jnp
import numpy as np
from jax import lax
from jax.experimental import pallas as pl
from jax.experimental.pallas import tpu as pltpu

F32 = jnp.float32
BF16 = jnp.bfloat16

HEAD_DIM = 64
N_HEADS = 4
SLAB = N_HEADS * HEAD_DIM
LANES = 128
N_HALVES = SLAB // LANES
HEADS_PER_HALF = LANES // HEAD_DIM
HGRN_HEADS = 4
HGRN_D = 128
HGRN_W = HGRN_HEADS * HGRN_D
N_MOD = 9
EPS = 1e-6
LB_FLOOR = 1e-30
NEG_BIG = -1e30
HALF_STEP = 0.5
ROPE_THETA = 10000.0
DIL_PATTERNS = ((128, 1), (512, 4), (2048, 16))
DIL_QBLK = 128

OFF_QA, OFF_KA, OFF_VA = 0, 256, 512
OFF_QD, OFF_KD, OFF_VD = 768, 1024, 1280
OFF_QH, OFF_FH, OFF_IH, OFF_GH = 1536, 2048, 2560, 3072

VMEM_LIMIT = 56 * 1024 * 1024

SB_DEAD = 104.0


def _params(sem):
    return pltpu.CompilerParams(dimension_semantics=sem, vmem_limit_bytes=VMEM_LIMIT)


def _const_spec(shape):
    nd = len(shape)
    return pl.BlockSpec(shape, lambda *_: (0,) * nd, pipeline_mode=pl.Buffered(1))


def _split_bf16(x):
    hi = x.astype(BF16)
    lo = (x - hi.astype(F32)).astype(BF16)
    return hi, lo


def _dot(a, b):
    return jnp.dot(a, b, preferred_element_type=F32)


def _dot_nt(a, b):
    return lax.dot_general(a, b, (((1,), (1,)), ((), ())), preferred_element_type=F32)


def _dot_tn(a, b):
    return lax.dot_general(a, b, (((0,), (0,)), ((), ())), preferred_element_type=F32)


def _sigmoid(x):
    return 1.0 / (1.0 + jnp.exp(-x))


def _log_sigmoid(x):
    return jnp.minimum(x, 0.0) - jnp.log(1.0 + jnp.exp(-jnp.abs(x)))


def _rms_scale(x):
    return lax.rsqrt(jnp.mean(x * x, axis=-1, keepdims=True) + EPS)


def _head_of_lane(lane):
    return jnp.right_shift(lane, HEAD_DIM.bit_length() - 1)


def _mod_kernel(c_ref, w_ref, b_ref, o_ref):
    c = c_ref[...]
    sc = c * _sigmoid(c)
    sc_hi, sc_lo = _split_bf16(sc)
    w = w_ref[0]
    w_hi, w_lo = _split_bf16(w)
    o_ref[0] = _dot(sc_hi, w_hi) + (_dot(sc_hi, w_lo) + _dot(sc_lo, w_hi)) + b_ref[0]


def _modulation(c, w_mod, b_mod):
    depth, d, n = w_mod.shape
    rows = 16
    nb = c.shape[0]
    assert nb <= rows
    c = jnp.pad(c, ((0, rows - nb), (0, 0)))
    b = rows
    tn = 1152
    out = pl.pallas_call(
        _mod_kernel,
        grid=(depth, n // tn),
        in_specs=[
            pl.BlockSpec((b, d), lambda l, j: (0, 0)),
            pl.BlockSpec((1, d, tn), lambda l, j: (l, 0, j)),
            pl.BlockSpec((1, 1, tn), lambda l, j: (l, 0, j)),
        ],
        out_specs=pl.BlockSpec((1, b, tn), lambda l, j: (l, 0, j)),
        out_shape=jax.ShapeDtypeStruct((depth, b, n), F32),
        compiler_params=_params(("arbitrary", "arbitrary")),
        name="adaln_mod",
    )(c, w_mod, b_mod.reshape(depth, 1, n))
    return out[:, :nb]


FFN_CHUNK = 256


def _ffn_half_step(xo_ref, mod_ref, row0, wg_ref, wu_ref, wd_ref, h_scr, acc_scr):
    shift = mod_ref[0, row0:row0 + 1, :]
    scale = mod_ref[0, row0 + 1:row0 + 2, :]
    gate = mod_ref[0, row0 + 2:row0 + 3, :]
    x = xo_ref[0]
    h_scr[...] = (x * _rms_scale(x) * (1.0 + scale) + shift).astype(BF16)
    for c in range(wg_ref.shape[1] // FFN_CHUNK):
        cols = slice(c * FFN_CHUNK, (c + 1) * FFN_CHUNK)
        g = _dot(h_scr[...], wg_ref[:, cols])
        u = _dot(h_scr[...], wu_ref[:, cols])
        a = (g * _sigmoid(g) * u).astype(BF16)
        y = _dot(a, wd_ref[cols, :])
        if c == 0:
            acc_scr[...] = y
        else:
            acc_scr[...] += y
    xo_ref[0] = xo_ref[0] + (HALF_STEP * gate) * acc_scr[...]


def _ffn_kernel(x_ref, mod_ref, wg_ref, wu_ref, wd_ref, o_ref, h_scr, acc_scr, *, row0):
    o_ref[0] = x_ref[0]
    _ffn_half_step(o_ref, mod_ref, row0, wg_ref, wu_ref, wd_ref, h_scr, acc_scr)


def _layer_spec(shape, layer):
    nd = len(shape)
    return pl.BlockSpec((None,) + tuple(shape[1:]), lambda *_: (layer,) + (0,) * (nd - 1),
                        pipeline_mode=pl.Buffered(1))


def _ffn(x, mod_l, w_gate, w_up, w_down, layer, row0, tm=1024):
    b, s, d = x.shape
    return pl.pallas_call(
        functools.partial(_ffn_kernel, row0=row0),
        grid=(b, s // tm),
        in_specs=[
            pl.BlockSpec((1, tm, d), lambda bi, i: (bi, i, 0)),
            pl.BlockSpec((1, N_MOD, d), lambda bi, i: (bi, 0, 0)),
            _layer_spec(w_gate.shape, layer),
            _layer_spec(w_up.shape, layer),
            _layer_spec(w_down.shape, layer),
        ],
        out_specs=pl.BlockSpec((1, tm, d), lambda bi, i: (bi, i, 0)),
        out_shape=jax.ShapeDtypeStruct((b, s, d), F32),
        scratch_shapes=[pltpu.VMEM((tm, d), BF16), pltpu.VMEM((tm, d), F32)],
        compiler_params=_params(("arbitrary", "arbitrary")),
        name="ffn",
    )(x, mod_l, w_gate, w_up, w_down)


def _swap_halves(x):
    n = x.shape[-1]
    lane = lax.broadcasted_iota(jnp.int32, x.shape, x.ndim - 1)
    up = pltpu.roll(x, n - HEAD_DIM // 2, x.ndim - 1)
    down = pltpu.roll(x, HEAD_DIM // 2, x.ndim - 1)
    return jnp.where(jnp.bitwise_and(lane, HEAD_DIM - 1) < HEAD_DIM // 2, up, down)


def _qk_norm_rope(x, gain, cos, sin_signed, head_mean):
    sq_hi, sq_lo = _split_bf16(x * x)
    ms = _dot(sq_hi, head_mean) + _dot(sq_lo, head_mean)
    xn = x * lax.rsqrt(ms + EPS) * gain
    return xn * cos + _swap_halves(xn) * sin_signed


def _hgrn_gate_terms(z, logits, layer):
    ex = jnp.exp(logits - jnp.max(logits, axis=0, keepdims=True))
    sm = ex / jnp.sum(ex, axis=0, keepdims=True)
    lb = jnp.zeros((1, HGRN_W), F32)
    for l in range(1, layer + 1):
        lb = lb + sm[l:l + 1, :]
    lb = jnp.clip(lb, 0.0, 1.0 - EPS)
    lb_floor = jnp.maximum(lb, LB_FLOOR)
    log_lb = jnp.log(lb_floor)
    one_m_lb = 1.0 - lb
    e = jnp.exp(-jnp.abs(z))
    one_p_e = 1.0 + e
    bterm = jnp.log(one_m_lb) + (jnp.minimum(z, 0.0) - jnp.log(one_p_e))
    log_f = jnp.maximum(log_lb, bterm) + jnp.log(1.0 + jnp.exp(-jnp.abs(log_lb - bterm)))
    sigmoid_neg = jnp.where(z > 0.0, e, 1.0) / one_p_e
    one_m_f = one_m_lb * sigmoid_neg - (lb_floor - lb)
    return log_f, one_m_f


def _inproj_kernel(x_ref, mod_ref, w_ref, cos_ref, sin_ref, qg_ref, kg_ref, hm_ref,
                   qa_ref, ka_ref, va_ref, qd_ref, kd_ref, vd_ref, qh_ref, fh_ref, ih_ref, gh_ref):
    x = x_ref[0]
    shift = mod_ref[0, 3:4, :]
    scale = mod_ref[0, 4:5, :]
    h = (x * _rms_scale(x) * (1.0 + scale) + shift).astype(BF16)

    def proj(off, width):
        return _dot(h, w_ref[:, off:off + width])

    def put_halves(ref, val):
        for half in range(N_HALVES):
            ref[0, half] = val[:, half * LANES:(half + 1) * LANES]

    qd_raw = proj(OFF_QD, SLAB)
    kd_raw = proj(OFF_KD, SLAB)
    qh_ref[0] = proj(OFF_QH, HGRN_W)
    cos = cos_ref[...]
    sin = sin_ref[...]
    hm = hm_ref[...]
    qd = _qk_norm_rope(qd_raw, qg_ref[...], cos, sin, hm)
    put_halves(qd_ref, qd * (HEAD_DIM ** -0.5))
    put_halves(kd_ref, _qk_norm_rope(kd_raw, kg_ref[...], cos, sin, hm))
    fh_ref[0] = proj(OFF_FH, HGRN_W)
    gh_ref[0] = proj(OFF_GH, HGRN_W)
    ih_ref[0] = proj(OFF_IH, HGRN_W).astype(BF16)
    qa_ref[0] = proj(OFF_QA, SLAB)
    ka_ref[0] = proj(OFF_KA, SLAB).astype(BF16)
    va_ref[0] = proj(OFF_VA, SLAB).astype(BF16)
    put_halves(vd_ref, proj(OFF_VD, SLAB))


def _rope_tables(s):
    half = HEAD_DIM // 2
    inv_freq = ROPE_THETA ** (-jnp.arange(half, dtype=F32) * 2.0 / HEAD_DIM)
    ang = jnp.arange(s, dtype=F32)[:, None] * inv_freq[None, :]
    cos, sin = jnp.cos(ang), jnp.sin(ang)
    cos_full = jnp.tile(jnp.concatenate([cos, cos], axis=-1), (1, N_HEADS))
    sin_signed = jnp.tile(jnp.concatenate([-sin, sin], axis=-1), (1, N_HEADS))
    return cos_full, sin_signed


def _inproj(x, mod_l, w_in, layer, q_gain, k_gain, cos, sin, tm=512):
    b, s, d = x.shape
    qg = jnp.tile(q_gain.reshape(1, HEAD_DIM), (1, N_HEADS))
    kg = jnp.tile(k_gain.reshape(1, HEAD_DIM), (1, N_HEADS))
    head_id = np.arange(SLAB) // HEAD_DIM
    head_mean = jnp.asarray(np.where(head_id[:, None] == head_id[None, :], 1.0 / HEAD_DIM, 0.0), BF16)

    def tok(width):
        return pl.BlockSpec((1, tm, width), lambda bi, i: (bi, i, 0))

    def out(width, dt):
        return jax.ShapeDtypeStruct((b, s, width), dt)

    halves = pl.BlockSpec((1, N_HALVES, tm, LANES), lambda bi, i: (bi, 0, i, 0))
    halves_out = jax.ShapeDtypeStruct((b, N_HALVES, s, LANES), F32)

    return pl.pallas_call(
        _inproj_kernel,
        grid=(b, s // tm),
        in_specs=[
            tok(d),
            pl.BlockSpec((1, N_MOD, d), lambda bi, i: (bi, 0, 0)),
            _layer_spec(w_in.shape, layer),
            pl.BlockSpec((tm, SLAB), lambda bi, i: (i, 0)),
            pl.BlockSpec((tm, SLAB), lambda bi, i: (i, 0)),
            _const_spec((1, SLAB)),
            _const_spec((1, SLAB)),
            _const_spec((SLAB, SLAB)),
        ],
        out_specs=[tok(SLAB)] * 3 + [halves] * 3 + [tok(HGRN_W)] * 4,
        out_shape=[out(SLAB, F32), out(SLAB, BF16), out(SLAB, BF16),
                   halves_out, halves_out, halves_out,
                   out(HGRN_W, F32), out(HGRN_W, F32), out(HGRN_W, BF16), out(HGRN_W, F32)],
        compiler_params=_params(("arbitrary", "arbitrary")),
        name="mixer_inproj",
    )(x, mod_l, w_in, cos, sin, qg, kg, head_mean)


SB_QBLK = 256
SB_KBLK = 128


def _sb_kernel(q_ref, k_ref, v_ref, uu_ref, o_ref, acc_ref, run_ref, qm_ref, z_ref):
    i = pl.program_id(1)
    qblk, kblk = SB_QBLK, SB_KBLK
    heads = range(N_HEADS)
    lane = lax.broadcasted_iota(jnp.int32, (1, SLAB), 1)
    head_masks = [_head_of_lane(lane) == hd for hd in heads]
    qm_ref[...] = (q_ref[0] * (HEAD_DIM ** -0.5)).astype(BF16)
    acc_ref[...] = jnp.zeros_like(acc_ref)
    run_ref[...] = jnp.zeros_like(run_ref)

    def per_head_rows(x):
        return jnp.concatenate([jnp.where(head_masks[hd], x, jnp.zeros_like(x)) for hd in heads], axis=0)

    def scores(j):
        off = pl.multiple_of(jnp.maximum(j, 0) * kblk, kblk)
        z_all = _dot_nt(qm_ref[...], per_head_rows(k_ref[0, pl.ds(off, kblk), :]))
        return [z_all[:, hd * kblk:(hd + 1) * kblk] for hd in heads]

    def key_block(j, z, causal_below, first_row=0):
        off = pl.multiple_of(j * kblk, kblk)
        v = v_ref[0, pl.ds(off, kblk), :]
        uu = uu_ref[...]
        rows = slice(first_row, qblk)
        causal = None
        if causal_below is not None:
            shape = (qblk - first_row, kblk)
            col_minus_row = (lax.broadcasted_iota(jnp.int32, shape, 1)
                             - lax.broadcasted_iota(jnp.int32, shape, 0))
            causal = col_minus_row < causal_below + first_row
        zr = [z[hd][rows] for hd in heads]
        ls = [_log_sigmoid(zr[hd]) for hd in heads]
        lnb = [ls[hd] - zr[hd] for hd in heads]
        if causal is not None:
            lnb = [jnp.where(causal, x, 0.0) for x in lnb]
        tail = []
        for hd in heads:
            lnb_hi, lnb_lo = _split_bf16(lnb[hd])
            tail.append(_dot(jnp.concatenate([lnb_hi, lnb_lo], axis=1), uu))
        run = [run_ref[hd, rows, :] for hd in heads]
        w = [jnp.exp(ls[hd] + tail[hd] + run[hd]) for hd in heads]
        if causal is not None:
            w = [jnp.where(causal, x, 0.0) for x in w]
        pv = _dot(jnp.concatenate([w[hd].astype(BF16) for hd in heads], axis=1), per_head_rows(v))
        run_max = None
        for hd in heads:
            new_run = run[hd] + (tail[hd][:, 0:1] + lnb[hd][:, 0:1])
            run_ref[hd, rows, :] = new_run
            run_max = new_run if run_max is None else jnp.maximum(run_max, new_run)
        acc_ref[rows, :] += pv
        return jnp.max(run_max) > -SB_DEAD

    per_q = qblk // kblk
    top = i * per_q + per_q - 1
    z_now = scores(top)
    alive = None
    for d in range(per_q):
        z_next = scores(top - d - 1)
        if d == per_q - 1:
            for hd in heads:
                z_ref[hd] = z_next[hd]
        alive = key_block(top - d, z_now, (d + 1 - per_q) * kblk, first_row=(per_q - 1 - d) * kblk)
        z_now = z_next

    def cond(carry):
        j, alive = carry
        return jnp.logical_and(j >= 0, alive)

    def body(carry):
        j, _ = carry
        z = [z_ref[hd] for hd in heads]
        z_ahead = scores(j - 1)
        alive = key_block(j, z, None)
        for hd in heads:
            z_ref[hd] = z_ahead[hd]
        return j - 1, alive

    lax.while_loop(cond, body, (top - per_q, alive))
    o_ref[0] = acc_ref[...]


def _stick_breaking(q, k, v):
    b, s, _ = q.shape
    qblk, kblk = SB_QBLK, SB_KBLK
    idx = np.arange(kblk)
    u = (idx[:, None] > idx[None, :])
    uu = jnp.asarray(np.concatenate([u, u], axis=0), BF16)
    return pl.pallas_call(
        _sb_kernel,
        grid=(b, s // qblk),
        in_specs=[
            pl.BlockSpec((1, qblk, SLAB), lambda bi, i: (bi, i, 0)),
            pl.BlockSpec((1, s, SLAB), lambda bi, i: (bi, 0, 0), pipeline_mode=pl.Buffered(1)),
            pl.BlockSpec((1, s, SLAB), lambda bi, i: (bi, 0, 0), pipeline_mode=pl.Buffered(1)),
            _const_spec((2 * kblk, kblk)),
        ],
        out_specs=pl.BlockSpec((1, qblk, SLAB), lambda bi, i: (bi, i, 0)),
        out_shape=jax.ShapeDtypeStruct((b, s, SLAB), F32),
        scratch_shapes=[pltpu.VMEM((qblk, SLAB), F32),
                        pltpu.VMEM((N_HEADS, qblk, kblk), F32),
                        pltpu.VMEM((qblk, SLAB), BF16),
                        pltpu.VMEM((N_HEADS, qblk, kblk), F32)],
        compiler_params=_params(("arbitrary", "arbitrary")),
        name="stick_breaking",
    )(q, k, v, uu)


DIL_TILE = DIL_QBLK * max(r for _, r in DIL_PATTERNS)
DIL_UNROLL = 2


def _dil_kernel(q_ref, kp_ref, kc_ref, vp_ref, vc_ref, bias_ref, o_ref,
                k_all, v_all, num_s, den_s, mx_s):
    n = pl.program_id(1)
    t_len = DIL_TILE
    qb = DIL_QBLK
    k_all[:, 0:t_len, :] = kp_ref[0]
    k_all[:, t_len:2 * t_len, :] = kc_ref[0]
    v_all[:, 0:t_len, :] = vp_ref[0]
    v_all[:, t_len:2 * t_len, :] = vc_ref[0]
    num_s[...] = jnp.zeros_like(num_s)
    den_s[...] = jnp.zeros_like(den_s)
    mx_s[...] = jnp.full(mx_s.shape, NEG_BIG, F32)
    lane = lax.broadcasted_iota(jnp.int32, (1, LANES), 1)
    head_masks = [_head_of_lane(lane) == hh for hh in range(HEADS_PER_HALF)]

    for pi, (window, r) in enumerate(DIL_PATTERNS):
        assert window // r == qb
        last = pi == len(DIL_PATTERNS) - 1
        shift = r.bit_length() - 1
        tiles = t_len // qb

        def rows_of(start, size, r=r):
            return pl.ds(start, size) if r == 1 else pl.ds(start, size, stride=r)

        def tile_group(it, carry, r=r, shift=shift, last=last, rows_of=rows_of):
            q_rows, k_rows, bias = [], [], []
            for u in range(DIL_UNROLL):
                idx = it * DIL_UNROLL + u
                t = jnp.right_shift(idx, shift)
                j = jnp.bitwise_and(idx, r - 1)
                q0 = t * (qb * r) + j
                has_prev = jnp.logical_or(n > 0, t > 0)
                bias.append(bias_ref[jnp.where(has_prev, 0, 1)])
                q_rows.append(rows_of(q0, qb))
                k_rows.append(rows_of(t_len + q0 - qb * r, 2 * qb))
            slabs = [(u, half) for u in range(DIL_UNROLL) for half in range(N_HALVES)]
            units = [(si, hh) for si in range(len(slabs)) for hh in range(HEADS_PER_HALF)]
            qs = [q_ref[0, half, q_rows[u], :] for u, half in slabs]
            kb = [k_all[half, k_rows[u], :].astype(BF16) for u, half in slabs]
            vb = [v_all[half, k_rows[u], :].astype(BF16) for u, half in slabs]
            sc = [_dot_nt(jnp.where(head_masks[hh], qs[si], 0.0).astype(BF16), kb[si]) + bias[slabs[si][0]]
                  for si, hh in units]
            mx = [jnp.max(x, axis=-1, keepdims=True) for x in sc]
            p = [jnp.exp(x - m) for x, m in zip(sc, mx)]
            den = [jnp.sum(x, axis=-1, keepdims=True) for x in p]
            pv = [_dot(x.astype(BF16), vb[si]) for x, (si, _) in zip(p, units)]
            for si, (u, half) in enumerate(slabs):
                first = si * HEADS_PER_HALF
                num_t, den_t, mx_t = pv[first], den[first], mx[first]
                for hh in range(1, HEADS_PER_HALF):
                    hm = head_masks[hh]
                    num_t = jnp.where(hm, pv[first + hh], num_t)
                    den_t = jnp.where(hm, den[first + hh], den_t)
                    mx_t = jnp.where(hm, mx[first + hh], mx_t)
                rows = q_rows[u]
                m_old = mx_s[half, rows, :]
                m_new = jnp.maximum(m_old, mx_t)
                a_old = jnp.exp(m_old - m_new)
                a_new = jnp.exp(mx_t - m_new)
                num_new = num_s[half, rows, :] * a_old + num_t * a_new
                den_new = den_s[half, rows, :] * a_old + den_t * a_new
                if last:
                    o_ref[0, half, rows, :] = num_new / den_new
                else:
                    mx_s[half, rows, :] = jnp.broadcast_to(m_new, (qb, LANES))
                    num_s[half, rows, :] = num_new
                    den_s[half, rows, :] = jnp.broadcast_to(den_new, (qb, LANES))
            return carry

        lax.fori_loop(0, tiles // DIL_UNROLL, tile_group, 0)


def _dilated(q, k, v):
    b, _, s, _ = q.shape
    qb = DIL_QBLK
    t_len = DIL_TILE
    a_idx = np.arange(qb)[:, None]
    k_idx = np.arange(2 * qb)[None, :]
    dist = a_idx + qb - k_idx
    band = (dist >= 0) & (dist <= qb)
    bias = np.stack([np.where(band, 0.0, NEG_BIG), np.where(band & (k_idx >= qb), 0.0, NEG_BIG)])
    cur = pl.BlockSpec((1, N_HALVES, t_len, LANES), lambda bi, n: (bi, 0, n, 0))
    prev = pl.BlockSpec((1, N_HALVES, t_len, LANES), lambda bi, n: (bi, 0, jnp.maximum(n - 1, 0), 0))
    stat = pltpu.VMEM((N_HALVES, t_len, LANES), F32)
    both = pltpu.VMEM((N_HALVES, 2 * t_len, LANES), F32)
    return pl.pallas_call(
        _dil_kernel,
        grid=(b, s // t_len),
        in_specs=[cur, prev, cur, prev, cur, _const_spec((2, qb, 2 * qb))],
        out_specs=cur,
        out_shape=jax.ShapeDtypeStruct((b, N_HALVES, s, LANES), F32),
        scratch_shapes=[both, both, stat, stat, stat],
        compiler_params=_params(("arbitrary", "arbitrary")),
        name="dilated",
    )(q, k, k, v, v, jnp.asarray(bias, F32))


HGRN_CHUNK = 128
HGRN_LEVELS = HGRN_CHUNK.bit_length() - 1
HGRN_TILE = 256
SUBLANES = 8


def _neg_abs_split_distance(g, log_f, half):
    c, w = g.shape
    block = 2 * half
    if block >= SUBLANES:
        g3 = g.reshape(c // block, block, w)
        return -jnp.abs(g3 - g3[:, half - 1:half, :]).reshape(c, w)
    pos = jnp.bitwise_and(lax.broadcasted_iota(jnp.int32, (c, 1), 0), block - 1)
    if half == 1:
        return jnp.where(pos == 1, log_f, 0.0)
    assert half == 2
    nxt = pltpu.roll(log_f, c - 1, 0)
    prv = pltpu.roll(log_f, 1, 0)
    return jnp.where(pos == 0, nxt, jnp.where(pos == 1, 0.0, jnp.where(pos == 2, log_f, log_f + prv)))


def _hgrn_kernel(q_ref, f_ref, i_ref, g_ref, lbl_ref, ng_ref, tril_ref, lvl_ref, o_ref,
                 state_ref, *, layer):
    @pl.when(pl.program_id(1) == 0)
    def _():
        state_ref[...] = jnp.zeros_like(state_ref)

    c = HGRN_CHUNK
    chunks = range(HGRN_TILE // c)
    heads = range(HGRN_HEADS)
    lanes = [slice(hd * HGRN_D, (hd + 1) * HGRN_D) for hd in heads]
    rows = [slice(ci * c, (ci + 1) * c) for ci in chunks]
    norm_gain = ng_ref[...]
    tril = tril_ref[...]
    logits = lbl_ref[...]

    qq, kk, g, g_last, q_dec, k_dec, q_lvl, k_lvl = [], [], [], [], [], [], [], []
    for ci in chunks:
        zq = q_ref[0, rows[ci], :]
        log_f, one_m_f = _hgrn_gate_terms(f_ref[0, rows[ci], :], logits, layer)
        lf_hi, lf_lo = _split_bf16(log_f)
        g_c = _dot(tril, lf_hi) + _dot(tril, lf_lo)
        qq_c = zq * _sigmoid(zq)
        g.append(g_c)
        g_last.append(g_c[c - 1:c, :])
        q_dec.append((qq_c * jnp.exp(g_c)).astype(BF16))
        k_dec.append((one_m_f * jnp.exp(g_last[ci] - g_c)).astype(BF16))
        ql = [qq_c.astype(BF16)]
        kl = [one_m_f.astype(BF16)]
        for lv in range(HGRN_LEVELS):
            decay = jnp.exp(_neg_abs_split_distance(g_c, log_f, 1 << lv).astype(BF16))
            ql.append(ql[0] * decay)
            kl.append(kl[0] * decay)
        q_lvl.append(ql)
        k_lvl.append(kl)

    val = [i_ref[0, rows[ci], :] for ci in chunks]
    updates = [[_dot_tn(val[ci][:, lanes[hd]], k_dec[ci][:, lanes[hd]]) for hd in heads] for ci in chunks]
    states = [[state_ref[hd] for hd in heads]]
    for ci in chunks:
        states.append([states[ci][hd] * jnp.exp(g_last[ci][:, lanes[hd]]) + updates[ci][hd]
                       for hd in heads])
    for hd in heads:
        state_ref[hd] = states[-1][hd]
    o_inter = [[_dot_nt(q_dec[ci][:, lanes[hd]], states[ci][hd].astype(BF16)) for hd in heads]
               for ci in chunks]
    scores = [[[_dot_nt(q_lvl[ci][lv][:, lanes[hd]], k_lvl[ci][lv][:, lanes[hd]])
                for lv in range(HGRN_LEVELS + 1)] for hd in heads] for ci in chunks]

    lvl = lvl_ref[...]
    level_masks = [lvl == lv for lv in range(HGRN_LEVELS + 1)]
    for ci in chunks:
        outs = []
        for hd in heads:
            p = jnp.zeros((c, c), F32)
            for lv in range(HGRN_LEVELS + 1):
                p = jnp.where(level_masks[lv], scores[ci][hd][lv], p)
            o_h = _dot(p.astype(BF16), val[ci][:, lanes[hd]]) + o_inter[ci][hd]
            outs.append(o_h * _rms_scale(o_h) * norm_gain)
        zg = g_ref[0, rows[ci], :]
        o_ref[0, rows[ci], :] = jnp.concatenate(outs, axis=-1) * (zg * _sigmoid(zg))


def _hgrn_tables():
    c = HGRN_CHUNK
    idx = np.arange(c)
    tril = idx[:, None] >= idx[None, :]
    diff = idx[:, None] ^ idx[None, :]
    lvl = np.where(idx[:, None] > idx[None, :], np.floor(np.log2(np.maximum(diff, 1))) + 1, -1)
    lvl = np.where(idx[:, None] == idx[None, :], 0, lvl).astype(np.int32)
    return jnp.asarray(tril, BF16), jnp.asarray(lvl)


def _hgrn(qh, fh, ih, gh, lb_logits, norm_gain, layer):
    b, s, _ = qh.shape
    t = HGRN_TILE
    c = HGRN_CHUNK
    depth = lb_logits.shape[0]
    tril, lvl = _hgrn_tables()
    tok = pl.BlockSpec((1, t, HGRN_W), lambda bi, i: (bi, i, 0))
    return pl.pallas_call(
        functools.partial(_hgrn_kernel, layer=layer),
        grid=(b, s // t),
        in_specs=[tok, tok, tok, tok,
                  _const_spec((depth, HGRN_W)),
                  _const_spec((1, HGRN_D)),
                  _const_spec((c, c)),
                  _const_spec((c, c))],
        out_specs=tok,
        out_shape=jax.ShapeDtypeStruct((b, s, HGRN_W), F32),
        scratch_shapes=[pltpu.VMEM((HGRN_HEADS, HGRN_D, HGRN_D), F32)],
        compiler_params=_params(("arbitrary", "arbitrary")),
        name="hgrn2",
    )(qh, fh, ih, gh, lb_logits.astype(F32), norm_gain.reshape(1, HGRN_D), tril, lvl)


def _outproj_ffn_kernel(x_ref, mod_ref, oa_ref, od_ref, oh_ref, wo_ref, wg_ref, wu_ref, wd_ref, o_ref,
                        h_scr, acc_scr):
    y = _dot(oa_ref[0].astype(BF16), wo_ref[0:SLAB, :])
    for half in range(N_HALVES):
        lo = SLAB + half * LANES
        y += _dot(od_ref[0, half].astype(BF16), wo_ref[lo:lo + LANES, :])
    y += _dot(oh_ref[0].astype(BF16), wo_ref[2 * SLAB:, :])
    gate = mod_ref[0, 5:6, :]
    o_ref[0] = x_ref[0] + gate * y
    _ffn_half_step(o_ref, mod_ref, 6, wg_ref, wu_ref, wd_ref, h_scr, acc_scr)


def _outproj_ffn(x, mod_l, o_a, o_d, o_h, w_out, w_gate, w_up, w_down, layer, tm=512):
    b, s, d = x.shape

    def tok(width):
        return pl.BlockSpec((1, tm, width), lambda bi, i: (bi, i, 0))

    return pl.pallas_call(
        _outproj_ffn_kernel,
        grid=(b, s // tm),
        in_specs=[tok(d), pl.BlockSpec((1, N_MOD, d), lambda bi, i: (bi, 0, 0)), tok(SLAB),
                  pl.BlockSpec((1, N_HALVES, tm, LANES), lambda bi, i: (bi, 0, i, 0)),
                  tok(HGRN_W), _layer_spec(w_out.shape, layer),
                  _layer_spec(w_gate.shape, layer), _layer_spec(w_up.shape, layer),
                  _layer_spec(w_down.shape, layer)],
        out_specs=tok(d),
        out_shape=jax.ShapeDtypeStruct((b, s, d), F32),
        scratch_shapes=[pltpu.VMEM((tm, d), BF16), pltpu.VMEM((tm, d), F32)],
        compiler_params=_params(("arbitrary", "arbitrary")),
        name="outproj_ffn",
    )(x, mod_l, o_a, o_d, o_h, w_out, w_gate, w_up, w_down)


def kernel(x, c, w_mod, b_mod, ffn1_w_gate, ffn1_w_up, ffn1_w_down, w_in, w_out, q_norm_g, k_norm_g,
           hgrn_norm_g, hgrn_lb_logits, ffn2_w_gate, ffn2_w_up, ffn2_w_down):
    b, s, d = x.shape
    depth = w_mod.shape[0]
    assert s % DIL_TILE == 0 and s % HGRN_TILE == 0 and s % SB_QBLK == 0
    mod = _modulation(c, w_mod, b_mod).reshape(depth, b, N_MOD, d)
    cos, sin = _rope_tables(s)
    ffn1 = [w.astype(BF16) for w in (ffn1_w_gate, ffn1_w_up, ffn1_w_down)]
    ffn2 = [w.astype(BF16) for w in (ffn2_w_gate, ffn2_w_up, ffn2_w_down)]
    w_in_b = w_in.astype(BF16)
    w_out_b = w_out.astype(BF16)
    for l in range(depth):
        x = _ffn(x, mod[l], *ffn1, layer=l, row0=0)
        qa, ka, va, qd, kd, vd, qh, fh, ih, gh = _inproj(
            x, mod[l], w_in_b, l, q_norm_g[l], k_norm_g[l], cos, sin)
        o_a = _stick_breaking(qa, ka, va)
        o_d = _dilated(qd, kd, vd)
        o_h = _hgrn(qh, fh, ih, gh, hgrn_lb_logits, hgrn_norm_g[l], layer=l)
        x = _outproj_ffn(x, mod[l], o_a, o_d, o_h, w_out_b, *ffn2, layer=l)
    return x
```

```python
import functools

import jax
import jax.numpy as jnp
import numpy as np
from jax import lax
from jax.experimental import pallas as pl
from jax.experimental.pallas import tpu as pltpu

F32 = jnp.float32
BF16 = jnp.bfloat16

HEAD_DIM = 64
N_HEADS = 4
SLAB = N_HEADS * HEAD_DIM
LANES = 128
N_HALVES = SLAB // LANES
HEADS_PER_HALF = LANES // HEAD_DIM
HGRN_HEADS = 4
HGRN_D = 128
HGRN_W = HGRN_HEADS * HGRN_D
N_MOD = 9
EPS = 1e-6
LB_FLOOR = 1e-30
NEG_BIG = -1e30
HALF_STEP = 0.5
ROPE_THETA = 10000.0
DIL_PATTERNS = ((128, 1), (512, 4), (2048, 16))
DIL_QBLK = 128

OFF_QA, OFF_KA, OFF_VA = 0, 256, 512
OFF_QD, OFF_KD, OFF_VD = 768, 1024, 1280
OFF_QH, OFF_FH, OFF_IH, OFF_GH = 1536, 2048, 2560, 3072

VMEM_LIMIT = 56 * 1024 * 1024

SB_DEAD = 104.0


def _params(sem):
    return pltpu.CompilerParams(dimension_semantics=sem, vmem_limit_bytes=VMEM_LIMIT)


def _const_spec(shape):
    nd = len(shape)
    return pl.BlockSpec(shape, lambda *_: (0,) * nd, pipeline_mode=pl.Buffered(1))


def _split_bf16(x):
    hi = x.astype(BF16)
    lo = (x - hi.astype(F32)).astype(BF16)
    return hi, lo


def _dot(a, b):
    return jnp.dot(a, b, preferred_element_type=F32)


def _dot_nt(a, b):
    return lax.dot_general(a, b, (((1,), (1,)), ((), ())), preferred_element_type=F32)


def _dot_tn(a, b):
    return lax.dot_general(a, b, (((0,), (0,)), ((), ())), preferred_element_type=F32)


def _sigmoid(x):
    return 1.0 / (1.0 + jnp.exp(-x))


def _log_sigmoid(x):
    return jnp.minimum(x, 0.0) - jnp.log(1.0 + jnp.exp(-jnp.abs(x)))


def _rms_scale(x):
    return lax.rsqrt(jnp.mean(x * x, axis=-1, keepdims=True) + EPS)


def _head_of_lane(lane):
    return jnp.right_shift(lane, HEAD_DIM.bit_length() - 1)


def _mod_kernel(c_ref, w_ref, b_ref, o_ref):
    c = c_ref[...]
    sc = c * _sigmoid(c)
    sc_hi, sc_lo = _split_bf16(sc)
    w = w_ref[0]
    w_hi, w_lo = _split_bf16(w)
    o_ref[0] = _dot(sc_hi, w_hi) + (_dot(sc_hi, w_lo) + _dot(sc_lo, w_hi)) + b_ref[0]


def _modulation(c, w_mod, b_mod):
    depth, d, n = w_mod.shape
    rows = 16
    nb = c.shape[0]
    assert nb <= rows
    c = jnp.pad(c, ((0, rows - nb), (0, 0)))
    b = rows
    tn = 1152
    out = pl.pallas_call(
        _mod_kernel,
        grid=(depth, n // tn),
        in_specs=[
            pl.BlockSpec((b, d), lambda l, j: (0, 0)),
            pl.BlockSpec((1, d, tn), lambda l, j: (l, 0, j)),
            pl.BlockSpec((1, 1, tn), lambda l, j: (l, 0, j)),
        ],
        out_specs=pl.BlockSpec((1, b, tn), lambda l, j: (l, 0, j)),
        out_shape=jax.ShapeDtypeStruct((depth, b, n), F32),
        compiler_params=_params(("arbitrary", "arbitrary")),
        name="adaln_mod",
    )(c, w_mod, b_mod.reshape(depth, 1, n))
    return out[:, :nb]


FFN_CHUNK = 256


def _ffn_half_step(xo_ref, mod_ref, row0, wg_ref, wu_ref, wd_ref, h_scr, acc_scr, between=None):
    shift = mod_ref[0, row0:row0 + 1, :]
    scale = mod_ref[0, row0 + 1:row0 + 2, :]
    gate = mod_ref[0, row0 + 2:row0 + 3, :]
    x = xo_ref[0]
    h_scr[...] = (x * _rms_scale(x) * (1.0 + scale) + shift).astype(BF16)
    for c in range(wg_ref.shape[1] // FFN_CHUNK):
        cols = slice(c * FFN_CHUNK, (c + 1) * FFN_CHUNK)
        g = _dot(h_scr[...], wg_ref[:, cols])
        u = _dot(h_scr[...], wu_ref[:, cols])
        a = (g * _sigmoid(g) * u).astype(BF16)
        y = _dot(a, wd_ref[cols, :])
        if c == 0:
            acc_scr[...] = y
        else:
            acc_scr[...] += y
        if between is not None:
            between(c)
    xo_ref[0] = xo_ref[0] + (HALF_STEP * gate) * acc_scr[...]


def _ffn_kernel(x_ref, mod_ref, wg_ref, wu_ref, wd_ref, o_ref, h_scr, acc_scr, *, row0):
    o_ref[0] = x_ref[0]
    _ffn_half_step(o_ref, mod_ref, row0, wg_ref, wu_ref, wd_ref, h_scr, acc_scr)


def _layer_spec(shape, layer):
    nd = len(shape)
    return pl.BlockSpec((None,) + tuple(shape[1:]), lambda *_: (layer,) + (0,) * (nd - 1),
                        pipeline_mode=pl.Buffered(1))


def _ffn(x, mod_l, w_gate, w_up, w_down, layer, row0, tm=1024):
    b, s, d = x.shape
    return pl.pallas_call(
        functools.partial(_ffn_kernel, row0=row0),
        grid=(b, s // tm),
        in_specs=[
            pl.BlockSpec((1, tm, d), lambda bi, i: (bi, i, 0)),
            pl.BlockSpec((1, N_MOD, d), lambda bi, i: (bi, 0, 0)),
            _layer_spec(w_gate.shape, layer),
            _layer_spec(w_up.shape, layer),
            _layer_spec(w_down.shape, layer),
        ],
        out_specs=pl.BlockSpec((1, tm, d), lambda bi, i: (bi, i, 0)),
        out_shape=jax.ShapeDtypeStruct((b, s, d), F32),
        scratch_shapes=[pltpu.VMEM((tm, d), BF16), pltpu.VMEM((tm, d), F32)],
        compiler_params=_params(("arbitrary", "arbitrary")),
        name="ffn",
    )(x, mod_l, w_gate, w_up, w_down)


def _swap_halves(x):
    n = x.shape[-1]
    lane = lax.broadcasted_iota(jnp.int32, x.shape, x.ndim - 1)
    up = pltpu.roll(x, n - HEAD_DIM // 2, x.ndim - 1)
    down = pltpu.roll(x, HEAD_DIM // 2, x.ndim - 1)
    return jnp.where(jnp.bitwise_and(lane, HEAD_DIM - 1) < HEAD_DIM // 2, up, down)


def _qk_norm_rope(x, gain, cos, sin_signed, head_mean):
    sq_hi, sq_lo = _split_bf16(x * x)
    ms = _dot(sq_hi, head_mean) + _dot(sq_lo, head_mean)
    xn = x * lax.rsqrt(ms + EPS) * gain
    return xn * cos + _swap_halves(xn) * sin_signed


def _hgrn_gate_terms(z, logits, layer):
    ex = jnp.exp(logits - jnp.max(logits, axis=0, keepdims=True))
    sm = ex / jnp.sum(ex, axis=0, keepdims=True)
    lb = jnp.zeros((1, HGRN_W), F32)
    for l in range(1, layer + 1):
        lb = lb + sm[l:l + 1, :]
    lb = jnp.clip(lb, 0.0, 1.0 - EPS)
    lb_floor = jnp.maximum(lb, LB_FLOOR)
    log_lb = jnp.log(lb_floor)
    one_m_lb = 1.0 - lb
    e = jnp.exp(-jnp.abs(z))
    one_p_e = 1.0 + e
    bterm = jnp.log(one_m_lb) + (jnp.minimum(z, 0.0) - jnp.log(one_p_e))
    log_f = jnp.maximum(log_lb, bterm) + jnp.log(1.0 + jnp.exp(-jnp.abs(log_lb - bterm)))
    sigmoid_neg = jnp.where(z > 0.0, e, 1.0) / one_p_e
    one_m_f = one_m_lb * sigmoid_neg - (lb_floor - lb)
    return log_f, one_m_f


def _front_kernel(x_ref, mod_ref, wg_ref, wu_ref, wd_ref, w_ref, cos_ref, sin_ref, qg_ref, kg_ref, hm_ref,
                  lbl_ref, ng_ref, tril_ref, lvl_ref,
                  x1_ref, qa_ref, ka_ref, va_ref, qd_ref, kd_ref, vd_ref, oh_ref,
                  h_scr, acc_scr, qh_s, fh_s, ih_s, gh_s, state_ref, *, layer):
    i = pl.program_id(1)

    @pl.when(i == 0)
    def _():
        for ref in (qh_s, fh_s, ih_s, gh_s, state_ref):
            ref[...] = jnp.zeros_like(ref)

    new = jnp.bitwise_and(i, 1)
    old = 1 - new
    hgrn = _hgrn_stages(qh_s.at[old], fh_s.at[old], ih_s.at[old], gh_s.at[old], lbl_ref[...], layer,
                        ng_ref[...], tril_ref[...], lvl_ref[...], state_ref, oh_ref.at[0])

    def advance_hgrn(times=1):
        for _ in range(times):
            next(hgrn, None)

    after_ffn_chunk = (1, 0, 1, 2, 0, 1, 2, 0, 1, 2, 0)
    x1_ref[0] = x_ref[0]
    _ffn_half_step(x1_ref, mod_ref, 0, wg_ref, wu_ref, wd_ref, h_scr, acc_scr,
                   between=lambda c: advance_hgrn(after_ffn_chunk[c % len(after_ffn_chunk)]))

    x = x1_ref[0]
    shift = mod_ref[0, 3:4, :]
    scale = mod_ref[0, 4:5, :]
    h = (x * _rms_scale(x) * (1.0 + scale) + shift).astype(BF16)

    def proj(off, width=SLAB):
        return _dot(h, w_ref[:, off:off + width])

    def put_halves(ref, val):
        for half in range(N_HALVES):
            ref[0, half] = val[:, half * LANES:(half + 1) * LANES]

    def park(ref, off, dtype):
        def piece(p):
            ref[new, :, p * SLAB:(p + 1) * SLAB] = proj(off + p * SLAB).astype(dtype)
        return [functools.partial(piece, p) for p in range(HGRN_W // SLAB)]

    raw = {}
    cos = cos_ref[...]
    sin = sin_ref[...]
    hm = hm_ref[...]

    def dilated_qk(ref, off, gain_ref, out_scale):
        def project():
            raw[off] = proj(off)

        def finish():
            put_halves(ref, _qk_norm_rope(raw[off], gain_ref[...], cos, sin, hm) * out_scale)
        return project, finish

    qd_project, qd_finish = dilated_qk(qd_ref, OFF_QD, qg_ref, HEAD_DIM ** -0.5)
    kd_project, kd_finish = dilated_qk(kd_ref, OFF_KD, kg_ref, 1.0)

    def store(ref, off, dtype):
        def run():
            ref[0] = proj(off).astype(dtype)
        return run

    pieces = ([qd_project, kd_project] + park(qh_s, OFF_QH, BF16) + park(fh_s, OFF_FH, F32)
              + [qd_finish, kd_finish] + park(gh_s, OFF_GH, BF16) + park(ih_s, OFF_IH, BF16)
              + [store(qa_ref, OFF_QA, F32), store(ka_ref, OFF_KA, BF16), store(va_ref, OFF_VA, BF16),
                 lambda: put_halves(vd_ref, proj(OFF_VD))])
    for n, piece in enumerate(pieces):
        piece()
        if n % 6 == 4:
            advance_hgrn()
    for _ in hgrn:
        pass


def _rope_tables(s):
    half = HEAD_DIM // 2
    inv_freq = ROPE_THETA ** (-jnp.arange(half, dtype=F32) * 2.0 / HEAD_DIM)
    ang = jnp.arange(s, dtype=F32)[:, None] * inv_freq[None, :]
    cos, sin = jnp.cos(ang), jnp.sin(ang)
    cos_full = jnp.tile(jnp.concatenate([cos, cos], axis=-1), (1, N_HEADS))
    sin_signed = jnp.tile(jnp.concatenate([-sin, sin], axis=-1), (1, N_HEADS))
    return cos_full, sin_signed


def _layer_front(x, mod_l, w_gate, w_up, w_down, w_in, layer, q_gain, k_gain, lb_logits, norm_gain,
                 cos, sin, tm=512):
    b, s, d = x.shape
    depth = lb_logits.shape[0]
    n_tiles = s // tm
    last = n_tiles - 1
    qg = jnp.tile(q_gain.reshape(1, HEAD_DIM), (1, N_HEADS))
    kg = jnp.tile(k_gain.reshape(1, HEAD_DIM), (1, N_HEADS))
    head_id = np.arange(SLAB) // HEAD_DIM
    head_mean = jnp.asarray(np.where(head_id[:, None] == head_id[None, :], 1.0 / HEAD_DIM, 0.0), BF16)
    tril, lvl = _hgrn_tables()
    c = HGRN_CHUNK

    def tok(width):
        return pl.BlockSpec((1, tm, width), lambda bi, i: (bi, jnp.minimum(i, last), 0))

    def out(width, dt):
        return jax.ShapeDtypeStruct((b, s, width), dt)

    table = pl.BlockSpec((tm, SLAB), lambda bi, i: (jnp.minimum(i, last), 0))
    halves = pl.BlockSpec((1, N_HALVES, tm, LANES), lambda bi, i: (bi, 0, jnp.minimum(i, last), 0))
    halves_out = jax.ShapeDtypeStruct((b, N_HALVES, s, LANES), F32)
    behind = pl.BlockSpec((1, tm, HGRN_W), lambda bi, i: (bi, jnp.maximum(i - 1, 0), 0))

    def parked(dt):
        return pltpu.VMEM((2, tm, HGRN_W), dt)

    return pl.pallas_call(
        functools.partial(_front_kernel, layer=layer),
        grid=(b, n_tiles + 1),
        in_specs=[
            tok(d),
            pl.BlockSpec((1, N_MOD, d), lambda bi, i: (bi, 0, 0)),
            _layer_spec(w_gate.shape, layer),
            _layer_spec(w_up.shape, layer),
            _layer_spec(w_down.shape, layer),
            _layer_spec(w_in.shape, layer),
            table,
            table,
            _const_spec((1, SLAB)),
            _const_spec((1, SLAB)),
            _const_spec((SLAB, SLAB)),
            _const_spec((depth, HGRN_W)),
            _const_spec((1, HGRN_D)),
            _const_spec((c, c)),
            _const_spec((c, c)),
        ],
        out_specs=[tok(d)] + [tok(SLAB)] * 3 + [halves] * 3 + [behind],
        out_shape=[out(d, F32), out(SLAB, F32), out(SLAB, BF16), out(SLAB, BF16),
                   halves_out, halves_out, halves_out, out(HGRN_W, F32)],
        scratch_shapes=[pltpu.VMEM((tm, d), BF16), pltpu.VMEM((tm, d), F32),
                        parked(BF16), parked(F32), parked(BF16), parked(BF16),
                        pltpu.VMEM((HGRN_HEADS, HGRN_D, HGRN_D), F32)],
        compiler_params=_params(("arbitrary", "arbitrary")),
        name="layer_front",
    )(x, mod_l, w_gate, w_up, w_down, w_in, cos, sin, qg, kg, head_mean, lb_logits.astype(F32),
      norm_gain.reshape(1, HGRN_D), tril, lvl)


SB_QBLK = 256
SB_KBLK = 128


def _sb_kernel(q_ref, k_ref, v_ref, uu_ref, o_ref, acc_ref, run_ref, qm_ref, z_ref):
    i = pl.program_id(1)
    qblk, kblk = SB_QBLK, SB_KBLK
    heads = range(N_HEADS)
    lane = lax.broadcasted_iota(jnp.int32, (1, SLAB), 1)
    head_masks = [_head_of_lane(lane) == hd for hd in heads]
    qm_ref[...] = (q_ref[0] * (HEAD_DIM ** -0.5)).astype(BF16)
    acc_ref[...] = jnp.zeros_like(acc_ref)
    run_ref[...] = jnp.zeros_like(run_ref)

    def per_head_rows(x):
        return jnp.concatenate([jnp.where(head_masks[hd], x, jnp.zeros_like(x)) for hd in heads], axis=0)

    def scores(j):
        off = pl.multiple_of(jnp.maximum(j, 0) * kblk, kblk)
        z_all = _dot_nt(qm_ref[...], per_head_rows(k_ref[0, pl.ds(off, kblk), :]))
        return [z_all[:, hd * kblk:(hd + 1) * kblk] for hd in heads]

    def key_block(j, z, causal_below, first_row=0):
        off = pl.multiple_of(j * kblk, kblk)
        v = v_ref[0, pl.ds(off, kblk), :]
        uu = uu_ref[...]
        rows = slice(first_row, qblk)
        causal = None
        if causal_below is not None:
            shape = (qblk - first_row, kblk)
            col_minus_row = (lax.broadcasted_iota(jnp.int32, shape, 1)
                             - lax.broadcasted_iota(jnp.int32, shape, 0))
            causal = col_minus_row < causal_below + first_row
        zr = [z[hd][rows] for hd in heads]
        ls = [_log_sigmoid(zr[hd]) for hd in heads]
        lnb = [ls[hd] - zr[hd] for hd in heads]
        if causal is not None:
            lnb = [jnp.where(causal, x, 0.0) for x in lnb]
        tail = []
        for hd in heads:
            lnb_hi, lnb_lo = _split_bf16(lnb[hd])
            tail.append(_dot(jnp.concatenate([lnb_hi, lnb_lo], axis=1), uu))
        run = [run_ref[hd, rows, :] for hd in heads]
        w = [jnp.exp(ls[hd] + tail[hd] + run[hd]) for hd in heads]
        if causal is not None:
            w = [jnp.where(causal, x, 0.0) for x in w]
        pv = _dot(jnp.concatenate([w[hd].astype(BF16) for hd in heads], axis=1), per_head_rows(v))
        run_max = None
        for hd in heads:
            new_run = run[hd] + (tail[hd][:, 0:1] + lnb[hd][:, 0:1])
            run_ref[hd, rows, :] = new_run
            run_max = new_run if run_max is None else jnp.maximum(run_max, new_run)
        acc_ref[rows, :] += pv
        return jnp.max(run_max) > -SB_DEAD

    per_q = qblk // kblk
    top = i * per_q + per_q - 1
    z_now = scores(top)
    alive = None
    for d in range(per_q):
        z_next = scores(top - d - 1)
        if d == per_q - 1:
            for hd in heads:
                z_ref[hd] = z_next[hd]
        alive = key_block(top - d, z_now, (d + 1 - per_q) * kblk, first_row=(per_q - 1 - d) * kblk)
        z_now = z_next

    def cond(carry):
        j, alive = carry
        return jnp.logical_and(j >= 0, alive)

    def body(carry):
        j, _ = carry
        z = [z_ref[hd] for hd in heads]
        z_ahead = scores(j - 1)
        alive = key_block(j, z, None)
        for hd in heads:
            z_ref[hd] = z_ahead[hd]
        return j - 1, alive

    lax.while_loop(cond, body, (top - per_q, alive))
    o_ref[0] = acc_ref[...]


def _stick_breaking(q, k, v):
    b, s, _ = q.shape
    qblk, kblk = SB_QBLK, SB_KBLK
    idx = np.arange(kblk)
    u = (idx[:, None] > idx[None, :])
    uu = jnp.asarray(np.concatenate([u, u], axis=0), BF16)
    return pl.pallas_call(
        _sb_kernel,
        grid=(b, s // qblk),
        in_specs=[
            pl.BlockSpec((1, qblk, SLAB), lambda bi, i: (bi, i, 0)),
            pl.BlockSpec((1, s, SLAB), lambda bi, i: (bi, 0, 0), pipeline_mode=pl.Buffered(1)),
            pl.BlockSpec((1, s, SLAB), lambda bi, i: (bi, 0, 0), pipeline_mode=pl.Buffered(1)),
            _const_spec((2 * kblk, kblk)),
        ],
        out_specs=pl.BlockSpec((1, qblk, SLAB), lambda bi, i: (bi, i, 0)),
        out_shape=jax.ShapeDtypeStruct((b, s, SLAB), F32),
        scratch_shapes=[pltpu.VMEM((qblk, SLAB), F32),
                        pltpu.VMEM((N_HEADS, qblk, kblk), F32),
                        pltpu.VMEM((qblk, SLAB), BF16),
                        pltpu.VMEM((N_HEADS, qblk, kblk), F32)],
        compiler_params=_params(("arbitrary", "arbitrary")),
        name="stick_breaking",
    )(q, k, v, uu)


DIL_TILE = DIL_QBLK * max(r for _, r in DIL_PATTERNS)
DIL_UNROLL = 2


def _dil_kernel(q_ref, kp_ref, kc_ref, vp_ref, vc_ref, bias_ref, o_ref,
                k_all, v_all, num_s, den_s, mx_s):
    n = pl.program_id(1)
    t_len = DIL_TILE
    qb = DIL_QBLK
    k_all[:, 0:t_len, :] = kp_ref[0]
    k_all[:, t_len:2 * t_len, :] = kc_ref[0]
    v_all[:, 0:t_len, :] = vp_ref[0]
    v_all[:, t_len:2 * t_len, :] = vc_ref[0]
    num_s[...] = jnp.zeros_like(num_s)
    den_s[...] = jnp.zeros_like(den_s)
    mx_s[...] = jnp.full(mx_s.shape, NEG_BIG, F32)
    lane = lax.broadcasted_iota(jnp.int32, (1, LANES), 1)
    head_masks = [_head_of_lane(lane) == hh for hh in range(HEADS_PER_HALF)]

    for pi, (window, r) in enumerate(DIL_PATTERNS):
        assert window // r == qb
        last = pi == len(DIL_PATTERNS) - 1
        shift = r.bit_length() - 1
        tiles = t_len // qb

        def rows_of(start, size, r=r):
            return pl.ds(start, size) if r == 1 else pl.ds(start, size, stride=r)

        def tile_group(it, carry, r=r, shift=shift, last=last, rows_of=rows_of):
            q_rows, k_rows, bias = [], [], []
            for u in range(DIL_UNROLL):
                idx = it * DIL_UNROLL + u
                t = jnp.right_shift(idx, shift)
                j = jnp.bitwise_and(idx, r - 1)
                q0 = t * (qb * r) + j
                has_prev = jnp.logical_or(n > 0, t > 0)
                bias.append(bias_ref[jnp.where(has_prev, 0, 1)])
                q_rows.append(rows_of(q0, qb))
                k_rows.append(rows_of(t_len + q0 - qb * r, 2 * qb))
            slabs = [(u, half) for u in range(DIL_UNROLL) for half in range(N_HALVES)]
            units = [(si, hh) for si in range(len(slabs)) for hh in range(HEADS_PER_HALF)]
            qs = [q_ref[0, half, q_rows[u], :] for u, half in slabs]
            kb = [k_all[half, k_rows[u], :].astype(BF16) for u, half in slabs]
            vb = [v_all[half, k_rows[u], :].astype(BF16) for u, half in slabs]
            sc = [_dot_nt(jnp.where(head_masks[hh], qs[si], 0.0).astype(BF16), kb[si]) + bias[slabs[si][0]]
                  for si, hh in units]
            mx = [jnp.max(x, axis=-1, keepdims=True) for x in sc]
            p = [jnp.exp(x - m) for x, m in zip(sc, mx)]
            den = [jnp.sum(x, axis=-1, keepdims=True) for x in p]
            pv = [_dot(x.astype(BF16), vb[si]) for x, (si, _) in zip(p, units)]
            for si, (u, half) in enumerate(slabs):
                first = si * HEADS_PER_HALF
                num_t, den_t, mx_t = pv[first], den[first], mx[first]
                for hh in range(1, HEADS_PER_HALF):
                    hm = head_masks[hh]
                    num_t = jnp.where(hm, pv[first + hh], num_t)
                    den_t = jnp.where(hm, den[first + hh], den_t)
                    mx_t = jnp.where(hm, mx[first + hh], mx_t)
                rows = q_rows[u]
                m_old = mx_s[half, rows, :]
                m_new = jnp.maximum(m_old, mx_t)
                a_old = jnp.exp(m_old - m_new)
                a_new = jnp.exp(mx_t - m_new)
                num_new = num_s[half, rows, :] * a_old + num_t * a_new
                den_new = den_s[half, rows, :] * a_old + den_t * a_new
                if last:
                    o_ref[0, half, rows, :] = num_new / den_new
                else:
                    mx_s[half, rows, :] = jnp.broadcast_to(m_new, (qb, LANES))
                    num_s[half, rows, :] = num_new
                    den_s[half, rows, :] = jnp.broadcast_to(den_new, (qb, LANES))
            return carry

        lax.fori_loop(0, tiles // DIL_UNROLL, tile_group, 0)


def _dilated(q, k, v):
    b, _, s, _ = q.shape
    qb = DIL_QBLK
    t_len = DIL_TILE
    a_idx = np.arange(qb)[:, None]
    k_idx = np.arange(2 * qb)[None, :]
    dist = a_idx + qb - k_idx
    band = (dist >= 0) & (dist <= qb)
    bias = np.stack([np.where(band, 0.0, NEG_BIG), np.where(band & (k_idx >= qb), 0.0, NEG_BIG)])
    cur = pl.BlockSpec((1, N_HALVES, t_len, LANES), lambda bi, n: (bi, 0, n, 0))
    prev = pl.BlockSpec((1, N_HALVES, t_len, LANES), lambda bi, n: (bi, 0, jnp.maximum(n - 1, 0), 0))
    stat = pltpu.VMEM((N_HALVES, t_len, LANES), F32)
    both = pltpu.VMEM((N_HALVES, 2 * t_len, LANES), F32)
    return pl.pallas_call(
        _dil_kernel,
        grid=(b, s // t_len),
        in_specs=[cur, prev, cur, prev, cur, _const_spec((2, qb, 2 * qb))],
        out_specs=cur,
        out_shape=jax.ShapeDtypeStruct((b, N_HALVES, s, LANES), F32),
        scratch_shapes=[both, both, stat, stat, stat],
        compiler_params=_params(("arbitrary", "arbitrary")),
        name="dilated",
    )(q, k, k, v, v, jnp.asarray(bias, F32))


HGRN_CHUNK = 128
HGRN_LEVELS = HGRN_CHUNK.bit_length() - 1
HGRN_TILE = 256
SUBLANES = 8


def _neg_abs_split_distance(g, log_f, half):
    c, w = g.shape
    block = 2 * half
    if block >= SUBLANES:
        g3 = g.reshape(c // block, block, w)
        return -jnp.abs(g3 - g3[:, half - 1:half, :]).reshape(c, w)
    pos = jnp.bitwise_and(lax.broadcasted_iota(jnp.int32, (c, 1), 0), block - 1)
    if half == 1:
        return jnp.where(pos == 1, log_f, 0.0)
    assert half == 2
    nxt = pltpu.roll(log_f, c - 1, 0)
    prv = pltpu.roll(log_f, 1, 0)
    return jnp.where(pos == 0, nxt, jnp.where(pos == 1, 0.0, jnp.where(pos == 2, log_f, log_f + prv)))


def _hgrn_stages(q_ref, f_ref, i_ref, g_ref, logits, layer, norm_gain, tril, lvl, state_ref, o_ref):
    c = HGRN_CHUNK
    chunks = range(q_ref.shape[0] // c)
    heads = range(HGRN_HEADS)
    lanes = [slice(hd * HGRN_D, (hd + 1) * HGRN_D) for hd in heads]
    rows = [slice(ci * c, (ci + 1) * c) for ci in chunks]

    level_masks = [lvl == lv for lv in range(HGRN_LEVELS + 1)]
    state = [state_ref[hd] for hd in heads]
    for ci in chunks:
        log_f, one_m_f = _hgrn_gate_terms(f_ref[rows[ci], :], logits, layer)
        lf_hi, lf_lo = _split_bf16(log_f)
        g = _dot(tril, lf_hi) + _dot(tril, lf_lo)
        yield

        zq = q_ref[rows[ci], :].astype(F32)
        qq = zq * _sigmoid(zq)
        g_last = g[c - 1:c, :]
        q_dec = (qq * jnp.exp(g)).astype(BF16)
        k_dec = (one_m_f * jnp.exp(g_last - g)).astype(BF16)
        q_lvl = [qq.astype(BF16)]
        k_lvl = [one_m_f.astype(BF16)]
        for lv in range(HGRN_LEVELS):
            decay = jnp.exp(_neg_abs_split_distance(g, log_f, 1 << lv).astype(BF16))
            q_lvl.append(q_lvl[0] * decay)
            k_lvl.append(k_lvl[0] * decay)
        val = i_ref[rows[ci], :]
        o_inter = [_dot_nt(q_dec[:, lanes[hd]], state[hd].astype(BF16)) for hd in heads]
        scores = [[_dot_nt(q_lvl[lv][:, lanes[hd]], k_lvl[lv][:, lanes[hd]])
                   for lv in range(HGRN_LEVELS + 1)] for hd in heads]
        state = [state[hd] * jnp.exp(g_last[:, lanes[hd]]) + _dot_tn(val[:, lanes[hd]], k_dec[:, lanes[hd]])
                 for hd in heads]
        yield

        outs = []
        for hd in heads:
            p = jnp.zeros((c, c), F32)
            for lv in range(HGRN_LEVELS + 1):
                p = jnp.where(level_masks[lv], scores[hd][lv], p)
            o_h = _dot(p.astype(BF16), val[:, lanes[hd]]) + o_inter[hd]
            outs.append(o_h * _rms_scale(o_h) * norm_gain)
        zg = g_ref[rows[ci], :].astype(F32)
        o_ref[rows[ci], :] = jnp.concatenate(outs, axis=-1) * (zg * _sigmoid(zg))
        yield
    for hd in heads:
        state_ref[hd] = state[hd]


def _hgrn_tables():
    c = HGRN_CHUNK
    idx = np.arange(c)
    tril = idx[:, None] >= idx[None, :]
    diff = idx[:, None] ^ idx[None, :]
    lvl = np.where(idx[:, None] > idx[None, :], np.floor(np.log2(np.maximum(diff, 1))) + 1, -1)
    lvl = np.where(idx[:, None] == idx[None, :], 0, lvl).astype(np.int32)
    return jnp.asarray(tril, BF16), jnp.asarray(lvl)


def _outproj_ffn_kernel(x_ref, mod_ref, oa_ref, od_ref, oh_ref, wo_ref, wg_ref, wu_ref, wd_ref, o_ref,
                        h_scr, acc_scr):
    y = _dot(oa_ref[0].astype(BF16), wo_ref[0:SLAB, :])
    for half in range(N_HALVES):
        lo = SLAB + half * LANES
        y += _dot(od_ref[0, half].astype(BF16), wo_ref[lo:lo + LANES, :])
    y += _dot(oh_ref[0].astype(BF16), wo_ref[2 * SLAB:, :])
    gate = mod_ref[0, 5:6, :]
    o_ref[0] = x_ref[0] + gate * y
    _ffn_half_step(o_ref, mod_ref, 6, wg_ref, wu_ref, wd_ref, h_scr, acc_scr)


def _outproj_ffn(x, mod_l, o_a, o_d, o_h, w_out, w_gate, w_up, w_down, layer, tm=512):
    b, s, d = x.shape

    def tok(width):
        return pl.BlockSpec((1, tm, width), lambda bi, i: (bi, i, 0))

    return pl.pallas_call(
        _outproj_ffn_kernel,
        grid=(b, s // tm),
        in_specs=[tok(d), pl.BlockSpec((1, N_MOD, d), lambda bi, i: (bi, 0, 0)), tok(SLAB),
                  pl.BlockSpec((1, N_HALVES, tm, LANES), lambda bi, i: (bi, 0, i, 0)),
                  tok(HGRN_W), _layer_spec(w_out.shape, layer),
                  _layer_spec(w_gate.shape, layer), _layer_spec(w_up.shape, layer),
                  _layer_spec(w_down.shape, layer)],
        out_specs=tok(d),
        out_shape=jax.ShapeDtypeStruct((b, s, d), F32),
        scratch_shapes=[pltpu.VMEM((tm, d), BF16), pltpu.VMEM((tm, d), F32)],
        compiler_params=_params(("arbitrary", "arbitrary")),
        name="outproj_ffn",
    )(x, mod_l, o_a, o_d, o_h, w_out, w_gate, w_up, w_down)


def kernel(x, c, w_mod, b_mod, ffn1_w_gate, ffn1_w_up, ffn1_w_down, w_in, w_out, q_norm_g, k_norm_g,
           hgrn_norm_g, hgrn_lb_logits, ffn2_w_gate, ffn2_w_up, ffn2_w_down):
    b, s, d = x.shape
    depth = w_mod.shape[0]
    assert s % DIL_TILE == 0 and s % HGRN_TILE == 0 and s % SB_QBLK == 0
    mod = _modulation(c, w_mod, b_mod).reshape(depth, b, N_MOD, d)
    cos, sin = _rope_tables(s)
    ffn1 = [w.astype(BF16) for w in (ffn1_w_gate, ffn1_w_up, ffn1_w_down)]
    ffn2 = [w.astype(BF16) for w in (ffn2_w_gate, ffn2_w_up, ffn2_w_down)]
    w_in_b = w_in.astype(BF16)
    w_out_b = w_out.astype(BF16)
    for l in range(depth):
        x, qa, ka, va, qd, kd, vd, o_h = _layer_front(
            x, mod[l], *ffn1, w_in_b, l, q_norm_g[l], k_norm_g[l], hgrn_lb_logits, hgrn_norm_g[l], cos, sin)
        o_a = _stick_breaking(qa, ka, va)
        o_d = _dilated(qd, kd, vd)
        x = _outproj_ffn(x, mod[l], o_a, o_d, o_h, w_out_b, *ffn2, layer=l)
    return x
```

```python
import functools

import jax
import jax.numpy as jnp
import numpy as np
from jax import lax
from jax.experimental import pallas as pl
from jax.experimental.pallas import tpu as pltpu

F32 = jnp.float32
BF16 = jnp.bfloat16

HEAD_DIM = 64
N_HEADS = 4
SLAB = N_HEADS * HEAD_DIM
LANES = 128
N_HALVES = SLAB // LANES
HEADS_PER_HALF = LANES // HEAD_DIM
HGRN_HEADS = 4
HGRN_D = 128
HGRN_W = HGRN_HEADS * HGRN_D
N_MOD = 9
EPS = 1e-6
LB_FLOOR = 1e-30
NEG_BIG = -1e30
HALF_STEP = 0.5
ROPE_THETA = 10000.0
DIL_PATTERNS = ((128, 1), (512, 4), (2048, 16))
DIL_QBLK = 128

OFF_QA, OFF_KA, OFF_VA = 0, 256, 512
OFF_QD, OFF_KD, OFF_VD = 768, 1024, 1280
OFF_QH, OFF_FH, OFF_IH, OFF_GH = 1536, 2048, 2560, 3072

VMEM_LIMIT = 56 * 1024 * 1024

SB_DEAD = 104.0


def _params(sem):
    return pltpu.CompilerParams(dimension_semantics=sem, vmem_limit_bytes=VMEM_LIMIT)


def _const_spec(shape):
    nd = len(shape)
    return pl.BlockSpec(shape, lambda *_: (0,) * nd, pipeline_mode=pl.Buffered(1))


def _split_bf16(x):
    hi = x.astype(BF16)
    lo = (x - hi.astype(F32)).astype(BF16)
    return hi, lo


def _dot(a, b):
    return jnp.dot(a, b, preferred_element_type=F32)


def _dot_nt(a, b):
    return lax.dot_general(a, b, (((1,), (1,)), ((), ())), preferred_element_type=F32)


def _dot_tn(a, b):
    return lax.dot_general(a, b, (((0,), (0,)), ((), ())), preferred_element_type=F32)


def _sigmoid(x):
    return 1.0 / (1.0 + jnp.exp(-x))


def _log_sigmoid(x):
    return jnp.minimum(x, 0.0) - jnp.log(1.0 + jnp.exp(-jnp.abs(x)))


def _rms_scale(x):
    return lax.rsqrt(jnp.mean(x * x, axis=-1, keepdims=True) + EPS)


def _head_of_lane(lane):
    return jnp.right_shift(lane, HEAD_DIM.bit_length() - 1)


def _mod_kernel(c_ref, w_ref, b_ref, o_ref):
    c = c_ref[...]
    sc = c * _sigmoid(c)
    sc_hi, sc_lo = _split_bf16(sc)
    w = w_ref[0]
    w_hi, w_lo = _split_bf16(w)
    o_ref[0] = _dot(sc_hi, w_hi) + (_dot(sc_hi, w_lo) + _dot(sc_lo, w_hi)) + b_ref[0]


def _modulation(c, w_mod, b_mod):
    depth, d, n = w_mod.shape
    rows = 16
    nb = c.shape[0]
    assert nb <= rows
    c = jnp.pad(c, ((0, rows - nb), (0, 0)))
    b = rows
    tn = 1152
    out = pl.pallas_call(
        _mod_kernel,
        grid=(depth, n // tn),
        in_specs=[
            pl.BlockSpec((b, d), lambda l, j: (0, 0)),
            pl.BlockSpec((1, d, tn), lambda l, j: (l, 0, j)),
            pl.BlockSpec((1, 1, tn), lambda l, j: (l, 0, j)),
        ],
        out_specs=pl.BlockSpec((1, b, tn), lambda l, j: (l, 0, j)),
        out_shape=jax.ShapeDtypeStruct((depth, b, n), F32),
        compiler_params=_params(("arbitrary", "arbitrary")),
        name="adaln_mod",
    )(c, w_mod, b_mod.reshape(depth, 1, n))
    return out[:, :nb]


FFN_CHUNK = 256


def _ffn_half_step(xo_ref, mod_ref, row0, wg_ref, wu_ref, wd_ref, h_scr, acc_scr):
    shift = mod_ref[0, row0:row0 + 1, :]
    scale = mod_ref[0, row0 + 1:row0 + 2, :]
    gate = mod_ref[0, row0 + 2:row0 + 3, :]
    x = xo_ref[0]
    h_scr[...] = (x * _rms_scale(x) * (1.0 + scale) + shift).astype(BF16)
    for c in range(wg_ref.shape[1] // FFN_CHUNK):
        cols = slice(c * FFN_CHUNK, (c + 1) * FFN_CHUNK)
        g = _dot(h_scr[...], wg_ref[:, cols])
        u = _dot(h_scr[...], wu_ref[:, cols])
        a = (g * _sigmoid(g) * u).astype(BF16)
        y = _dot(a, wd_ref[cols, :])
        if c == 0:
            acc_scr[...] = y
        else:
            acc_scr[...] += y
    xo_ref[0] = xo_ref[0] + (HALF_STEP * gate) * acc_scr[...]


def _ffn_kernel(x_ref, mod_ref, wg_ref, wu_ref, wd_ref, o_ref, h_scr, acc_scr, *, row0):
    o_ref[0] = x_ref[0]
    _ffn_half_step(o_ref, mod_ref, row0, wg_ref, wu_ref, wd_ref, h_scr, acc_scr)


def _layer_spec(shape, layer):
    nd = len(shape)
    return pl.BlockSpec((None,) + tuple(shape[1:]), lambda *_: (layer,) + (0,) * (nd - 1),
                        pipeline_mode=pl.Buffered(1))


def _ffn(x, mod_l, w_gate, w_up, w_down, layer, row0, tm=1024):
    b, s, d = x.shape
    return pl.pallas_call(
        functools.partial(_ffn_kernel, row0=row0),
        grid=(b, s // tm),
        in_specs=[
            pl.BlockSpec((1, tm, d), lambda bi, i: (bi, i, 0)),
            pl.BlockSpec((1, N_MOD, d), lambda bi, i: (bi, 0, 0)),
            _layer_spec(w_gate.shape, layer),
            _layer_spec(w_up.shape, layer),
            _layer_spec(w_down.shape, layer),
        ],
        out_specs=pl.BlockSpec((1, tm, d), lambda bi, i: (bi, i, 0)),
        out_shape=jax.ShapeDtypeStruct((b, s, d), F32),
        scratch_shapes=[pltpu.VMEM((tm, d), BF16), pltpu.VMEM((tm, d), F32)],
        compiler_params=_params(("arbitrary", "arbitrary")),
        name="ffn",
    )(x, mod_l, w_gate, w_up, w_down)


def _swap_halves(x):
    n = x.shape[-1]
    lane = lax.broadcasted_iota(jnp.int32, x.shape, x.ndim - 1)
    up = pltpu.roll(x, n - HEAD_DIM // 2, x.ndim - 1)
    down = pltpu.roll(x, HEAD_DIM // 2, x.ndim - 1)
    return jnp.where(jnp.bitwise_and(lane, HEAD_DIM - 1) < HEAD_DIM // 2, up, down)


def _qk_norm_rope(x, gain, cos, sin_signed, head_mean):
    sq_hi, sq_lo = _split_bf16(x * x)
    ms = _dot(sq_hi, head_mean) + _dot(sq_lo, head_mean)
    xn = x * lax.rsqrt(ms + EPS) * gain
    return xn * cos + _swap_halves(xn) * sin_signed


def _hgrn_gate_terms(z, logits, layer):
    ex = jnp.exp(logits - jnp.max(logits, axis=0, keepdims=True))
    sm = ex / jnp.sum(ex, axis=0, keepdims=True)
    lb = jnp.zeros((1, HGRN_W), F32)
    for l in range(1, layer + 1):
        lb = lb + sm[l:l + 1, :]
    lb = jnp.clip(lb, 0.0, 1.0 - EPS)
    lb_floor = jnp.maximum(lb, LB_FLOOR)
    log_lb = jnp.log(lb_floor)
    one_m_lb = 1.0 - lb
    e = jnp.exp(-jnp.abs(z))
    one_p_e = 1.0 + e
    bterm = jnp.log(one_m_lb) + (jnp.minimum(z, 0.0) - jnp.log(one_p_e))
    log_f = jnp.maximum(log_lb, bterm) + jnp.log(1.0 + jnp.exp(-jnp.abs(log_lb - bterm)))
    sigmoid_neg = jnp.where(z > 0.0, e, 1.0) / one_p_e
    one_m_f = one_m_lb * sigmoid_neg - (lb_floor - lb)
    return log_f, one_m_f


def _inproj_kernel(x_ref, mod_ref, w_ref, cos_ref, sin_ref, qg_ref, kg_ref, hm_ref,
                   qa_ref, ka_ref, va_ref, qd_ref, kd_ref, vd_ref, qh_ref, fh_ref, ih_ref, gh_ref):
    x = x_ref[0]
    shift = mod_ref[0, 3:4, :]
    scale = mod_ref[0, 4:5, :]
    h = (x * _rms_scale(x) * (1.0 + scale) + shift).astype(BF16)

    def proj(off, width):
        return _dot(h, w_ref[:, off:off + width])

    def put_halves(ref, val):
        for half in range(N_HALVES):
            ref[0, half] = val[:, half * LANES:(half + 1) * LANES]

    qd_raw = proj(OFF_QD, SLAB)
    kd_raw = proj(OFF_KD, SLAB)
    qh_ref[0] = proj(OFF_QH, HGRN_W)
    cos = cos_ref[...]
    sin = sin_ref[...]
    hm = hm_ref[...]
    qd = _qk_norm_rope(qd_raw, qg_ref[...], cos, sin, hm)
    put_halves(qd_ref, qd * (HEAD_DIM ** -0.5))
    put_halves(kd_ref, _qk_norm_rope(kd_raw, kg_ref[...], cos, sin, hm))
    fh_ref[0] = proj(OFF_FH, HGRN_W)
    gh_ref[0] = proj(OFF_GH, HGRN_W)
    ih_ref[0] = proj(OFF_IH, HGRN_W).astype(BF16)
    qa_ref[0] = proj(OFF_QA, SLAB)
    ka_ref[0] = proj(OFF_KA, SLAB).astype(BF16)
    va_ref[0] = proj(OFF_VA, SLAB).astype(BF16)
    put_halves(vd_ref, proj(OFF_VD, SLAB))


def _rope_tables(s):
    half = HEAD_DIM // 2
    inv_freq = ROPE_THETA ** (-jnp.arange(half, dtype=F32) * 2.0 / HEAD_DIM)
    ang = jnp.arange(s, dtype=F32)[:, None] * inv_freq[None, :]
    cos, sin = jnp.cos(ang), jnp.sin(ang)
    cos_full = jnp.tile(jnp.concatenate([cos, cos], axis=-1), (1, N_HEADS))
    sin_signed = jnp.tile(jnp.concatenate([-sin, sin], axis=-1), (1, N_HEADS))
    return cos_full, sin_signed


def _inproj(x, mod_l, w_in, layer, q_gain, k_gain, cos, sin, tm=1024):
    b, s, d = x.shape
    qg = jnp.tile(q_gain.reshape(1, HEAD_DIM), (1, N_HEADS))
    kg = jnp.tile(k_gain.reshape(1, HEAD_DIM), (1, N_HEADS))
    head_id = np.arange(SLAB) // HEAD_DIM
    head_mean = jnp.asarray(np.where(head_id[:, None] == head_id[None, :], 1.0 / HEAD_DIM, 0.0), BF16)

    def tok(width):
        return pl.BlockSpec((1, tm, width), lambda bi, i: (bi, i, 0))

    def out(width, dt):
        return jax.ShapeDtypeStruct((b, s, width), dt)

    halves = pl.BlockSpec((1, N_HALVES, tm, LANES), lambda bi, i: (bi, 0, i, 0))
    halves_out = jax.ShapeDtypeStruct((b, N_HALVES, s, LANES), F32)

    return pl.pallas_call(
        _inproj_kernel,
        grid=(b, s // tm),
        in_specs=[
            tok(d),
            pl.BlockSpec((1, N_MOD, d), lambda bi, i: (bi, 0, 0)),
            _layer_spec(w_in.shape, layer),
            pl.BlockSpec((tm, SLAB), lambda bi, i: (i, 0)),
            pl.BlockSpec((tm, SLAB), lambda bi, i: (i, 0)),
            _const_spec((1, SLAB)),
            _const_spec((1, SLAB)),
            _const_spec((SLAB, SLAB)),
        ],
        out_specs=[tok(SLAB)] * 3 + [halves] * 3 + [tok(HGRN_W)] * 4,
        out_shape=[out(SLAB, F32), out(SLAB, BF16), out(SLAB, BF16),
                   halves_out, halves_out, halves_out,
                   out(HGRN_W, F32), out(HGRN_W, F32), out(HGRN_W, BF16), out(HGRN_W, F32)],
        compiler_params=_params(("arbitrary", "arbitrary")),
        name="mixer_inproj",
    )(x, mod_l, w_in, cos, sin, qg, kg, head_mean)


SB_QBLK = 256
SB_KBLK = 128


def _sb_kernel(q_ref, k_ref, v_ref, uu_ref, o_ref, acc_ref, run_ref, qm_ref, z_ref):
    i = pl.program_id(1)
    qblk, kblk = SB_QBLK, SB_KBLK
    heads = range(N_HEADS)
    lane = lax.broadcasted_iota(jnp.int32, (1, SLAB), 1)
    head_masks = [_head_of_lane(lane) == hd for hd in heads]
    qm_ref[...] = (q_ref[0] * (HEAD_DIM ** -0.5)).astype(BF16)
    acc_ref[...] = jnp.zeros_like(acc_ref)
    run_ref[...] = jnp.zeros_like(run_ref)

    def per_head_rows(x):
        return jnp.concatenate([jnp.where(head_masks[hd], x, jnp.zeros_like(x)) for hd in heads], axis=0)

    def scores(j):
        off = pl.multiple_of(jnp.maximum(j, 0) * kblk, kblk)
        z_all = _dot_nt(qm_ref[...], per_head_rows(k_ref[0, pl.ds(off, kblk), :]))
        return [z_all[:, hd * kblk:(hd + 1) * kblk] for hd in heads]

    def key_block(j, z, causal_below, first_row=0):
        off = pl.multiple_of(j * kblk, kblk)
        v = v_ref[0, pl.ds(off, kblk), :]
        uu = uu_ref[...]
        rows = slice(first_row, qblk)
        causal = None
        if causal_below is not None:
            shape = (qblk - first_row, kblk)
            col_minus_row = (lax.broadcasted_iota(jnp.int32, shape, 1)
                             - lax.broadcasted_iota(jnp.int32, shape, 0))
            causal = col_minus_row < causal_below + first_row
        zr = [z[hd][rows] for hd in heads]
        ls = [_log_sigmoid(zr[hd]) for hd in heads]
        lnb = [ls[hd] - zr[hd] for hd in heads]
        if causal is not None:
            lnb = [jnp.where(causal, x, 0.0) for x in lnb]
        tail = []
        for hd in heads:
            lnb_hi, lnb_lo = _split_bf16(lnb[hd])
            tail.append(_dot(jnp.concatenate([lnb_hi, lnb_lo], axis=1), uu))
        run = [run_ref[hd, rows, :] for hd in heads]
        w = [jnp.exp(ls[hd] + tail[hd] + run[hd]) for hd in heads]
        if causal is not None:
            w = [jnp.where(causal, x, 0.0) for x in w]
        pv = _dot(jnp.concatenate([w[hd].astype(BF16) for hd in heads], axis=1), per_head_rows(v))
        run_max = None
        for hd in heads:
            new_run = run[hd] + (tail[hd][:, 0:1] + lnb[hd][:, 0:1])
            run_ref[hd, rows, :] = new_run
            run_max = new_run if run_max is None else jnp.maximum(run_max, new_run)
        acc_ref[rows, :] += pv
        return jnp.max(run_max) > -SB_DEAD

    per_q = qblk // kblk
    top = i * per_q + per_q - 1
    z_now = scores(top)
    alive = None
    for d in range(per_q):
        z_next = scores(top - d - 1)
        if d == per_q - 1:
            for hd in heads:
                z_ref[hd] = z_next[hd]
        alive = key_block(top - d, z_now, (d + 1 - per_q) * kblk, first_row=(per_q - 1 - d) * kblk)
        z_now = z_next

    def cond(carry):
        j, alive = carry
        return jnp.logical_and(j >= 0, alive)

    def body(carry):
        j, _ = carry
        z = [z_ref[hd] for hd in heads]
        z_ahead = scores(j - 1)
        alive = key_block(j, z, None)
        for hd in heads:
            z_ref[hd] = z_ahead[hd]
        return j - 1, alive

    lax.while_loop(cond, body, (top - per_q, alive))
    o_ref[0] = acc_ref[...]


def _stick_breaking(q, k, v):
    b, s, _ = q.shape
    qblk, kblk = SB_QBLK, SB_KBLK
    idx = np.arange(kblk)
    u = (idx[:, None] > idx[None, :])
    uu = jnp.asarray(np.concatenate([u, u], axis=0), BF16)
    return pl.pallas_call(
        _sb_kernel,
        grid=(b, s // qblk),
        in_specs=[
            pl.BlockSpec((1, qblk, SLAB), lambda bi, i: (bi, i, 0)),
            pl.BlockSpec((1, s, SLAB), lambda bi, i: (bi, 0, 0), pipeline_mode=pl.Buffered(1)),
            pl.BlockSpec((1, s, SLAB), lambda bi, i: (bi, 0, 0), pipeline_mode=pl.Buffered(1)),
            _const_spec((2 * kblk, kblk)),
        ],
        out_specs=pl.BlockSpec((1, qblk, SLAB), lambda bi, i: (bi, i, 0)),
        out_shape=jax.ShapeDtypeStruct((b, s, SLAB), F32),
        scratch_shapes=[pltpu.VMEM((qblk, SLAB), F32),
                        pltpu.VMEM((N_HEADS, qblk, kblk), F32),
                        pltpu.VMEM((qblk, SLAB), BF16),
                        pltpu.VMEM((N_HEADS, qblk, kblk), F32)],
        compiler_params=_params(("arbitrary", "arbitrary")),
        name="stick_breaking",
    )(q, k, v, uu)


DIL_TILE = DIL_QBLK * max(r for _, r in DIL_PATTERNS)
DIL_UNROLL = 2


def _dil_kernel(q_ref, kp_ref, kc_ref, vp_ref, vc_ref, bias_ref, o_ref,
                k_all, v_all, num_s, den_s, mx_s):
    n = pl.program_id(1)
    t_len = DIL_TILE
    qb = DIL_QBLK
    k_all[:, 0:t_len, :] = kp_ref[0]
    k_all[:, t_len:2 * t_len, :] = kc_ref[0]
    v_all[:, 0:t_len, :] = vp_ref[0]
    v_all[:, t_len:2 * t_len, :] = vc_ref[0]
    num_s[...] = jnp.zeros_like(num_s)
    den_s[...] = jnp.zeros_like(den_s)
    mx_s[...] = jnp.full(mx_s.shape, NEG_BIG, F32)
    lane = lax.broadcasted_iota(jnp.int32, (1, LANES), 1)
    head_masks = [_head_of_lane(lane) == hh for hh in range(HEADS_PER_HALF)]

    for pi, (window, r) in enumerate(DIL_PATTERNS):
        assert window // r == qb
        last = pi == len(DIL_PATTERNS) - 1
        shift = r.bit_length() - 1
        tiles = t_len // qb

        def rows_of(start, size, r=r):
            return pl.ds(start, size) if r == 1 else pl.ds(start, size, stride=r)

        def tile_group(it, carry, r=r, shift=shift, last=last, rows_of=rows_of):
            q_rows, k_rows, bias = [], [], []
            for u in range(DIL_UNROLL):
                idx = it * DIL_UNROLL + u
                t = jnp.right_shift(idx, shift)
                j = jnp.bitwise_and(idx, r - 1)
                q0 = t * (qb * r) + j
                has_prev = jnp.logical_or(n > 0, t > 0)
                bias.append(bias_ref[jnp.where(has_prev, 0, 1)])
                q_rows.append(rows_of(q0, qb))
                k_rows.append(rows_of(t_len + q0 - qb * r, 2 * qb))
            slabs = [(u, half) for u in range(DIL_UNROLL) for half in range(N_HALVES)]
            units = [(si, hh) for si in range(len(slabs)) for hh in range(HEADS_PER_HALF)]
            qs = [q_ref[0, half, q_rows[u], :] for u, half in slabs]
            kb = [k_all[half, k_rows[u], :].astype(BF16) for u, half in slabs]
            vb = [v_all[half, k_rows[u], :].astype(BF16) for u, half in slabs]
            sc = [_dot_nt(jnp.where(head_masks[hh], qs[si], 0.0).astype(BF16), kb[si]) + bias[slabs[si][0]]
                  for si, hh in units]
            mx = [jnp.max(x, axis=-1, keepdims=True) for x in sc]
            p = [jnp.exp(x - m) for x, m in zip(sc, mx)]
            den = [jnp.sum(x, axis=-1, keepdims=True) for x in p]
            pv = [_dot(x.astype(BF16), vb[si]) for x, (si, _) in zip(p, units)]
            for si, (u, half) in enumerate(slabs):
                first = si * HEADS_PER_HALF
                num_t, den_t, mx_t = pv[first], den[first], mx[first]
                for hh in range(1, HEADS_PER_HALF):
                    hm = head_masks[hh]
                    num_t = jnp.where(hm, pv[first + hh], num_t)
                    den_t = jnp.where(hm, den[first + hh], den_t)
                    mx_t = jnp.where(hm, mx[first + hh], mx_t)
                rows = q_rows[u]
                m_old = mx_s[half, rows, :]
                m_new = jnp.maximum(m_old, mx_t)
                a_old = jnp.exp(m_old - m_new)
                a_new = jnp.exp(mx_t - m_new)
                num_new = num_s[half, rows, :] * a_old + num_t * a_new
                den_new = den_s[half, rows, :] * a_old + den_t * a_new
                if last:
                    o_ref[0, half, rows, :] = num_new / den_new
                else:
                    mx_s[half, rows, :] = jnp.broadcast_to(m_new, (qb, LANES))
                    num_s[half, rows, :] = num_new
                    den_s[half, rows, :] = jnp.broadcast_to(den_new, (qb, LANES))
            return carry

        lax.fori_loop(0, tiles // DIL_UNROLL, tile_group, 0)


def _dilated(q, k, v):
    b, _, s, _ = q.shape
    qb = DIL_QBLK
    t_len = DIL_TILE
    a_idx = np.arange(qb)[:, None]
    k_idx = np.arange(2 * qb)[None, :]
    dist = a_idx + qb - k_idx
    band = (dist >= 0) & (dist <= qb)
    bias = np.stack([np.where(band, 0.0, NEG_BIG), np.where(band & (k_idx >= qb), 0.0, NEG_BIG)])
    cur = pl.BlockSpec((1, N_HALVES, t_len, LANES), lambda bi, n: (bi, 0, n, 0))
    prev = pl.BlockSpec((1, N_HALVES, t_len, LANES), lambda bi, n: (bi, 0, jnp.maximum(n - 1, 0), 0))
    stat = pltpu.VMEM((N_HALVES, t_len, LANES), F32)
    both = pltpu.VMEM((N_HALVES, 2 * t_len, LANES), F32)
    return pl.pallas_call(
        _dil_kernel,
        grid=(b, s // t_len),
        in_specs=[cur, prev, cur, prev, cur, _const_spec((2, qb, 2 * qb))],
        out_specs=cur,
        out_shape=jax.ShapeDtypeStruct((b, N_HALVES, s, LANES), F32),
        scratch_shapes=[both, both, stat, stat, stat],
        compiler_params=_params(("arbitrary", "arbitrary")),
        name="dilated",
    )(q, k, k, v, v, jnp.asarray(bias, F32))


HGRN_CHUNK = 128
HGRN_LEVELS = HGRN_CHUNK.bit_length() - 1
HGRN_TILE = 512
SUBLANES = 8


def _neg_abs_split_distance(g, log_f, half):
    c, w = g.shape
    block = 2 * half
    if block >= SUBLANES:
        g3 = g.reshape(c // block, block, w)
        return -jnp.abs(g3 - g3[:, half - 1:half, :]).reshape(c, w)
    pos = jnp.bitwise_and(lax.broadcasted_iota(jnp.int32, (c, 1), 0), block - 1)
    if half == 1:
        return jnp.where(pos == 1, log_f, 0.0)
    assert half == 2
    nxt = pltpu.roll(log_f, c - 1, 0)
    prv = pltpu.roll(log_f, 1, 0)
    return jnp.where(pos == 0, nxt, jnp.where(pos == 1, 0.0, jnp.where(pos == 2, log_f, log_f + prv)))


def _hgrn_kernel(q_ref, f_ref, i_ref, g_ref, lbl_ref, ng_ref, tril_ref, lvl_ref, o_ref,
                 state_ref, *, layer):
    @pl.when(pl.program_id(1) == 0)
    def _():
        state_ref[...] = jnp.zeros_like(state_ref)

    c = HGRN_CHUNK
    chunks = range(HGRN_TILE // c)
    heads = range(HGRN_HEADS)
    lanes = [slice(hd * HGRN_D, (hd + 1) * HGRN_D) for hd in heads]
    rows = [slice(ci * c, (ci + 1) * c) for ci in chunks]
    norm_gain = ng_ref[...]
    tril = tril_ref[...]
    logits = lbl_ref[...]

    g_last, q_dec, k_dec, q_lvl, k_lvl = [], [], [], [], []
    for ci in chunks:
        zq = q_ref[0, rows[ci], :]
        log_f, one_m_f = _hgrn_gate_terms(f_ref[0, rows[ci], :], logits, layer)
        lf_hi, lf_lo = _split_bf16(log_f)
        g_c = _dot(tril, lf_hi) + _dot(tril, lf_lo)
        qq_c = zq * _sigmoid(zq)
        g_last.append(g_c[c - 1:c, :])
        q_dec.append((qq_c * jnp.exp(g_c)).astype(BF16))
        k_dec.append((one_m_f * jnp.exp(g_last[ci] - g_c)).astype(BF16))
        ql = [qq_c.astype(BF16)]
        kl = [one_m_f.astype(BF16)]
        for lv in range(HGRN_LEVELS):
            decay = jnp.exp(_neg_abs_split_distance(g_c, log_f, 1 << lv).astype(BF16))
            ql.append(ql[0] * decay)
            kl.append(kl[0] * decay)
        q_lvl.append(ql)
        k_lvl.append(kl)

    val = [i_ref[0, rows[ci], :] for ci in chunks]
    updates = [[_dot_tn(val[ci][:, lanes[hd]], k_dec[ci][:, lanes[hd]]) for hd in heads] for ci in chunks]
    states = [[state_ref[hd] for hd in heads]]
    for ci in chunks:
        states.append([states[ci][hd] * jnp.exp(g_last[ci][:, lanes[hd]]) + updates[ci][hd]
                       for hd in heads])
    for hd in heads:
        state_ref[hd] = states[-1][hd]
    o_inter = [[_dot_nt(q_dec[ci][:, lanes[hd]], states[ci][hd].astype(BF16)) for hd in heads]
               for ci in chunks]
    scores = [[[_dot_nt(q_lvl[ci][lv][:, lanes[hd]], k_lvl[ci][lv][:, lanes[hd]])
                for lv in range(HGRN_LEVELS + 1)] for hd in heads] for ci in chunks]

    lvl = lvl_ref[...]
    level_masks = [lvl == lv for lv in range(HGRN_LEVELS + 1)]
    for ci in chunks:
        outs = []
        for hd in heads:
            p = jnp.zeros((c, c), F32)
            for lv in range(HGRN_LEVELS + 1):
                p = jnp.where(level_masks[lv], scores[ci][hd][lv], p)
            o_h = _dot(p.astype(BF16), val[ci][:, lanes[hd]]) + o_inter[ci][hd]
            outs.append(o_h * _rms_scale(o_h) * norm_gain)
        zg = g_ref[0, rows[ci], :]
        o_ref[0, rows[ci], :] = jnp.concatenate(outs, axis=-1) * (zg * _sigmoid(zg))


def _hgrn_tables():
    c = HGRN_CHUNK
    idx = np.arange(c)
    tril = idx[:, None] >= idx[None, :]
    diff = idx[:, None] ^ idx[None, :]
    lvl = np.where(idx[:, None] > idx[None, :], np.floor(np.log2(np.maximum(diff, 1))) + 1, -1)
    lvl = np.where(idx[:, None] == idx[None, :], 0, lvl).astype(np.int32)
    return jnp.asarray(tril, BF16), jnp.asarray(lvl)


def _hgrn(qh, fh, ih, gh, lb_logits, norm_gain, layer):
    b, s, _ = qh.shape
    t = HGRN_TILE
    c = HGRN_CHUNK
    depth = lb_logits.shape[0]
    tril, lvl = _hgrn_tables()
    tok = pl.BlockSpec((1, t, HGRN_W), lambda bi, i: (bi, i, 0))
    return pl.pallas_call(
        functools.partial(_hgrn_kernel, layer=layer),
        grid=(b, s // t),
        in_specs=[tok, tok, tok, tok,
                  _const_spec((depth, HGRN_W)),
                  _const_spec((1, HGRN_D)),
                  _const_spec((c, c)),
                  _const_spec((c, c))],
        out_specs=tok,
        out_shape=jax.ShapeDtypeStruct((b, s, HGRN_W), F32),
        scratch_shapes=[pltpu.VMEM((HGRN_HEADS, HGRN_D, HGRN_D), F32)],
        compiler_params=_params(("arbitrary", "arbitrary")),
        name="hgrn2",
    )(qh, fh, ih, gh, lb_logits.astype(F32), norm_gain.reshape(1, HGRN_D), tril, lvl)


def _outproj_ffn_kernel(x_ref, mod_ref, oa_ref, od_ref, oh_ref, wo_ref, wg_ref, wu_ref, wd_ref, o_ref,
                        h_scr, acc_scr):
    y = _dot(oa_ref[0].astype(BF16), wo_ref[0:SLAB, :])
    for half in range(N_HALVES):
        lo = SLAB + half * LANES
        y += _dot(od_ref[0, half].astype(BF16), wo_ref[lo:lo + LANES, :])
    y += _dot(oh_ref[0].astype(BF16), wo_ref[2 * SLAB:, :])
    gate = mod_ref[0, 5:6, :]
    o_ref[0] = x_ref[0] + gate * y
    _ffn_half_step(o_ref, mod_ref, 6, wg_ref, wu_ref, wd_ref, h_scr, acc_scr)


def _outproj_ffn(x, mod_l, o_a, o_d, o_h, w_out, w_gate, w_up, w_down, layer, tm=512):
    b, s, d = x.shape

    def tok(width):
        return pl.BlockSpec((1, tm, width), lambda bi, i: (bi, i, 0))

    return pl.pallas_call(
        _outproj_ffn_kernel,
        grid=(b, s // tm),
        in_specs=[tok(d), pl.BlockSpec((1, N_MOD, d), lambda bi, i: (bi, 0, 0)), tok(SLAB),
                  pl.BlockSpec((1, N_HALVES, tm, LANES), lambda bi, i: (bi, 0, i, 0)),
                  tok(HGRN_W), _layer_spec(w_out.shape, layer),
                  _layer_spec(w_gate.shape, layer), _layer_spec(w_up.shape, layer),
                  _layer_spec(w_down.shape, layer)],
        out_specs=tok(d),
        out_shape=jax.ShapeDtypeStruct((b, s, d), F32),
        scratch_shapes=[pltpu.VMEM((tm, d), BF16), pltpu.VMEM((tm, d), F32)],
        compiler_params=_params(("arbitrary", "arbitrary")),
        name="outproj_ffn",
    )(x, mod_l, o_a, o_d, o_h, w_out, w_gate, w_up, w_down)


def kernel(x, c, w_mod, b_mod, ffn1_w_gate, ffn1_w_up, ffn1_w_down, w_in, w_out, q_norm_g, k_norm_g,
           hgrn_norm_g, hgrn_lb_logits, ffn2_w_gate, ffn2_w_up, ffn2_w_down):
    b, s, d = x.shape
    depth = w_mod.shape[0]
    assert s % DIL_TILE == 0 and s % HGRN_TILE == 0 and s % SB_QBLK == 0
    mod = _modulation(c, w_mod, b_mod).reshape(depth, b, N_MOD, d)
    cos, sin = _rope_tables(s)
    ffn1 = [w.astype(BF16) for w in (ffn1_w_gate, ffn1_w_up, ffn1_w_down)]
    ffn2 = [w.astype(BF16) for w in (ffn2_w_gate, ffn2_w_up, ffn2_w_down)]
    w_in_b = w_in.astype(BF16)
    w_out_b = w_out.astype(BF16)
    for l in range(depth):
        x = _ffn(x, mod[l], *ffn1, layer=l, row0=0)
        qa, ka, va, qd, kd, vd, qh, fh, ih, gh = _inproj(
            x, mod[l], w_in_b, l, q_norm_g[l], k_norm_g[l], cos, sin)
        o_a = _stick_breaking(qa, ka, va)
        o_d = _dilated(qd, kd, vd)
        o_h = _hgrn(qh, fh, ih, gh, hgrn_lb_logits, hgrn_norm_g[l], layer=l)
        x = _outproj_ffn(x, mod[l], o_a, o_d, o_h, w_out_b, *ffn2, layer=l)
    return x
```

```python
import functools

import jax
import jax.numpy as jnp
import numpy as np
from jax import lax
from jax.experimental import pallas as pl
from jax.experimental.pallas import tpu as pltpu

F32 = jnp.float32
BF16 = jnp.bfloat16

HEAD_DIM = 64
N_HEADS = 4
SLAB = N_HEADS * HEAD_DIM
LANES = 128
N_HALVES = SLAB // LANES
HEADS_PER_HALF = LANES // HEAD_DIM
HGRN_HEADS = 4
HGRN_D = 128
HGRN_W = HGRN_HEADS * HGRN_D
N_MOD = 9
EPS = 1e-6
LB_FLOOR = 1e-30
NEG_BIG = -1e30
HALF_STEP = 0.5
ROPE_THETA = 10000.0
DIL_PATTERNS = ((128, 1), (512, 4), (2048, 16))
DIL_QBLK = 128

OFF_QA, OFF_KA, OFF_VA = 0, 256, 512
OFF_QD, OFF_KD, OFF_VD = 768, 1024, 1280
OFF_QH, OFF_FH, OFF_IH, OFF_GH = 1536, 2048, 2560, 3072

VMEM_LIMIT = 56 * 1024 * 1024

SB_DEAD = 104.0


def _params(sem):
    return pltpu.CompilerParams(dimension_semantics=sem, vmem_limit_bytes=VMEM_LIMIT)


def _const_spec(shape):
    nd = len(shape)
    return pl.BlockSpec(shape, lambda *_: (0,) * nd, pipeline_mode=pl.Buffered(1))


def _split_bf16(x):
    hi = x.astype(BF16)
    lo = (x - hi.astype(F32)).astype(BF16)
    return hi, lo


def _dot(a, b):
    return jnp.dot(a, b, preferred_element_type=F32)


def _dot_nt(a, b):
    return lax.dot_general(a, b, (((1,), (1,)), ((), ())), preferred_element_type=F32)


def _dot_tn(a, b):
    return lax.dot_general(a, b, (((0,), (0,)), ((), ())), preferred_element_type=F32)


def _sigmoid(x):
    return 1.0 / (1.0 + jnp.exp(-x))


def _log_sigmoid(x):
    return jnp.minimum(x, 0.0) - jnp.log(1.0 + jnp.exp(-jnp.abs(x)))


def _rms_scale(x):
    return lax.rsqrt(jnp.mean(x * x, axis=-1, keepdims=True) + EPS)


def _head_of_lane(lane):
    return jnp.right_shift(lane, HEAD_DIM.bit_length() - 1)


def _mod_kernel(c_ref, w_ref, b_ref, o_ref):
    c = c_ref[...]
    sc = c * _sigmoid(c)
    sc_hi, sc_lo = _split_bf16(sc)
    w = w_ref[0]
    w_hi, w_lo = _split_bf16(w)
    o_ref[0] = _dot(sc_hi, w_hi) + (_dot(sc_hi, w_lo) + _dot(sc_lo, w_hi)) + b_ref[0]


def _modulation(c, w_mod, b_mod):
    depth, d, n = w_mod.shape
    rows = 16
    nb = c.shape[0]
    assert nb <= rows
    c = jnp.pad(c, ((0, rows - nb), (0, 0)))
    b = rows
    tn = 1152
    out = pl.pallas_call(
        _mod_kernel,
        grid=(depth, n // tn),
        in_specs=[
            pl.BlockSpec((b, d), lambda l, j: (0, 0)),
            pl.BlockSpec((1, d, tn), lambda l, j: (l, 0, j)),
            pl.BlockSpec((1, 1, tn), lambda l, j: (l, 0, j)),
        ],
        out_specs=pl.BlockSpec((1, b, tn), lambda l, j: (l, 0, j)),
        out_shape=jax.ShapeDtypeStruct((depth, b, n), F32),
        compiler_params=_params(("arbitrary", "arbitrary")),
        name="adaln_mod",
    )(c, w_mod, b_mod.reshape(depth, 1, n))
    return out[:, :nb]


FFN_CHUNK = 256


def _ffn_half_step(xo_ref, mod_ref, row0, wg_ref, wu_ref, wd_ref, h_scr, acc_scr):
    shift = mod_ref[0, row0:row0 + 1, :]
    scale = mod_ref[0, row0 + 1:row0 + 2, :]
    gate = mod_ref[0, row0 + 2:row0 + 3, :]
    x = xo_ref[0]
    h_scr[...] = (x * _rms_scale(x) * (1.0 + scale) + shift).astype(BF16)
    for c in range(wg_ref.shape[1] // FFN_CHUNK):
        cols = slice(c * FFN_CHUNK, (c + 1) * FFN_CHUNK)
        g = _dot(h_scr[...], wg_ref[:, cols])
        u = _dot(h_scr[...], wu_ref[:, cols])
        a = (g * _sigmoid(g) * u).astype(BF16)
        y = _dot(a, wd_ref[cols, :])
        if c == 0:
            acc_scr[...] = y
        else:
            acc_scr[...] += y
    xo_ref[0] = xo_ref[0] + (HALF_STEP * gate) * acc_scr[...]


def _ffn_kernel(x_ref, mod_ref, wg_ref, wu_ref, wd_ref, o_ref, h_scr, acc_scr, *, row0):
    o_ref[0] = x_ref[0]
    _ffn_half_step(o_ref, mod_ref, row0, wg_ref, wu_ref, wd_ref, h_scr, acc_scr)


def _layer_spec(shape, layer):
    nd = len(shape)
    return pl.BlockSpec((None,) + tuple(shape[1:]), lambda *_: (layer,) + (0,) * (nd - 1),
                        pipeline_mode=pl.Buffered(1))


def _ffn(x, mod_l, w_gate, w_up, w_down, layer, row0, tm=1024):
    b, s, d = x.shape
    return pl.pallas_call(
        functools.partial(_ffn_kernel, row0=row0),
        grid=(b, s // tm),
        in_specs=[
            pl.BlockSpec((1, tm, d), lambda bi, i: (bi, i, 0)),
            pl.BlockSpec((1, N_MOD, d), lambda bi, i: (bi, 0, 0)),
            _layer_spec(w_gate.shape, layer),
            _layer_spec(w_up.shape, layer),
            _layer_spec(w_down.shape, layer),
        ],
        out_specs=pl.BlockSpec((1, tm, d), lambda bi, i: (bi, i, 0)),
        out_shape=jax.ShapeDtypeStruct((b, s, d), F32),
        scratch_shapes=[pltpu.VMEM((tm, d), BF16), pltpu.VMEM((tm, d), F32)],
        compiler_params=_params(("arbitrary", "arbitrary")),
        name="ffn",
    )(x, mod_l, w_gate, w_up, w_down)


def _swap_halves(x):
    n = x.shape[-1]
    lane = lax.broadcasted_iota(jnp.int32, x.shape, x.ndim - 1)
    up = pltpu.roll(x, n - HEAD_DIM // 2, x.ndim - 1)
    down = pltpu.roll(x, HEAD_DIM // 2, x.ndim - 1)
    return jnp.where(jnp.bitwise_and(lane, HEAD_DIM - 1) < HEAD_DIM // 2, up, down)


def _qk_norm_rope(x, gain, cos, sin_signed, head_mean):
    sq_hi, sq_lo = _split_bf16(x * x)
    ms = _dot(sq_hi, head_mean) + _dot(sq_lo, head_mean)
    xn = x * lax.rsqrt(ms + EPS) * gain
    return xn * cos + _swap_halves(xn) * sin_signed


def _hgrn_gate_terms(z, logits, layer):
    ex = jnp.exp(logits - jnp.max(logits, axis=0, keepdims=True))
    sm = ex / jnp.sum(ex, axis=0, keepdims=True)
    lb = jnp.zeros((1, HGRN_W), F32)
    for l in range(1, layer + 1):
        lb = lb + sm[l:l + 1, :]
    lb = jnp.clip(lb, 0.0, 1.0 - EPS)
    lb_floor = jnp.maximum(lb, LB_FLOOR)
    log_lb = jnp.log(lb_floor)
    one_m_lb = 1.0 - lb
    e = jnp.exp(-jnp.abs(z))
    one_p_e = 1.0 + e
    bterm = jnp.log(one_m_lb) + (jnp.minimum(z, 0.0) - jnp.log(one_p_e))
    log_f = jnp.maximum(log_lb, bterm) + jnp.log(1.0 + jnp.exp(-jnp.abs(log_lb - bterm)))
    sigmoid_neg = jnp.where(z > 0.0, e, 1.0) / one_p_e
    one_m_f = one_m_lb * sigmoid_neg - (lb_floor - lb)
    return log_f, one_m_f


def _inproj_kernel(x_ref, mod_ref, w_ref, cos_ref, sin_ref, qg_ref, kg_ref, hm_ref,
                   qa_ref, ka_ref, va_ref, qd_ref, kd_ref, vd_ref, qh_ref, fh_ref, ih_ref, gh_ref):
    x = x_ref[0]
    shift = mod_ref[0, 3:4, :]
    scale = mod_ref[0, 4:5, :]
    h = (x * _rms_scale(x) * (1.0 + scale) + shift).astype(BF16)

    def proj(off, width):
        return _dot(h, w_ref[:, off:off + width])

    def put_halves(ref, val):
        for half in range(N_HALVES):
            ref[0, half] = val[:, half * LANES:(half + 1) * LANES]

    qd_raw = proj(OFF_QD, SLAB)
    kd_raw = proj(OFF_KD, SLAB)
    qh_ref[0] = proj(OFF_QH, HGRN_W)
    cos = cos_ref[...]
    sin = sin_ref[...]
    hm = hm_ref[...]
    qd = _qk_norm_rope(qd_raw, qg_ref[...], cos, sin, hm)
    put_halves(qd_ref, qd * (HEAD_DIM ** -0.5))
    put_halves(kd_ref, _qk_norm_rope(kd_raw, kg_ref[...], cos, sin, hm))
    fh_ref[0] = proj(OFF_FH, HGRN_W)
    gh_ref[0] = proj(OFF_GH, HGRN_W)
    ih_ref[0] = proj(OFF_IH, HGRN_W).astype(BF16)
    qa_ref[0] = proj(OFF_QA, SLAB)
    ka_ref[0] = proj(OFF_KA, SLAB).astype(BF16)
    va_ref[0] = proj(OFF_VA, SLAB).astype(BF16)
    put_halves(vd_ref, proj(OFF_VD, SLAB))


def _rope_tables(s):
    half = HEAD_DIM // 2
    inv_freq = ROPE_THETA ** (-jnp.arange(half, dtype=F32) * 2.0 / HEAD_DIM)
    ang = jnp.arange(s, dtype=F32)[:, None] * inv_freq[None, :]
    cos, sin = jnp.cos(ang), jnp.sin(ang)
    cos_full = jnp.tile(jnp.concatenate([cos, cos], axis=-1), (1, N_HEADS))
    sin_signed = jnp.tile(jnp.concatenate([-sin, sin], axis=-1), (1, N_HEADS))
    return cos_full, sin_signed


def _inproj(x, mod_l, w_in, layer, q_gain, k_gain, cos, sin, tm=1024):
    b, s, d = x.shape
    qg = jnp.tile(q_gain.reshape(1, HEAD_DIM), (1, N_HEADS))
    kg = jnp.tile(k_gain.reshape(1, HEAD_DIM), (1, N_HEADS))
    head_id = np.arange(SLAB) // HEAD_DIM
    head_mean = jnp.asarray(np.where(head_id[:, None] == head_id[None, :], 1.0 / HEAD_DIM, 0.0), BF16)

    def tok(width):
        return pl.BlockSpec((1, tm, width), lambda bi, i: (bi, i, 0))

    def out(width, dt):
        return jax.ShapeDtypeStruct((b, s, width), dt)

    halves = pl.BlockSpec((1, N_HALVES, tm, LANES), lambda bi, i: (bi, 0, i, 0))
    halves_out = jax.ShapeDtypeStruct((b, N_HALVES, s, LANES), F32)

    return pl.pallas_call(
        _inproj_kernel,
        grid=(b, s // tm),
        in_specs=[
            tok(d),
            pl.BlockSpec((1, N_MOD, d), lambda bi, i: (bi, 0, 0)),
            _layer_spec(w_in.shape, layer),
            pl.BlockSpec((tm, SLAB), lambda bi, i: (i, 0)),
            pl.BlockSpec((tm, SLAB), lambda bi, i: (i, 0)),
            _const_spec((1, SLAB)),
            _const_spec((1, SLAB)),
            _const_spec((SLAB, SLAB)),
        ],
        out_specs=[tok(SLAB)] * 3 + [halves] * 3 + [tok(HGRN_W)] * 4,
        out_shape=[out(SLAB, F32), out(SLAB, BF16), out(SLAB, BF16),
                   halves_out, halves_out, halves_out,
                   out(HGRN_W, F32), out(HGRN_W, F32), out(HGRN_W, BF16), out(HGRN_W, F32)],
        compiler_params=_params(("arbitrary", "arbitrary")),
        name="mixer_inproj",
    )(x, mod_l, w_in, cos, sin, qg, kg, head_mean)


SB_QBLK = 256
SB_KBLK = 128


def _sb_kernel(q_ref, k_ref, v_ref, uu_ref, o_ref, acc_ref, run_ref, qm_ref, z_ref):
    i = pl.program_id(1)
    qblk, kblk = SB_QBLK, SB_KBLK
    heads = range(N_HEADS)
    lane = lax.broadcasted_iota(jnp.int32, (1, SLAB), 1)
    head_masks = [_head_of_lane(lane) == hd for hd in heads]
    qm_ref[...] = (q_ref[0] * (HEAD_DIM ** -0.5)).astype(BF16)
    acc_ref[...] = jnp.zeros_like(acc_ref)
    run_ref[...] = jnp.zeros_like(run_ref)

    def per_head_rows(x):
        return jnp.concatenate([jnp.where(head_masks[hd], x, jnp.zeros_like(x)) for hd in heads], axis=0)

    def scores(j):
        off = pl.multiple_of(jnp.maximum(j, 0) * kblk, kblk)
        z_all = _dot_nt(qm_ref[...], per_head_rows(k_ref[0, pl.ds(off, kblk), :]))
        return [z_all[:, hd * kblk:(hd + 1) * kblk] for hd in heads]

    def key_block(j, z, causal_below, first_row=0):
        off = pl.multiple_of(j * kblk, kblk)
        v = v_ref[0, pl.ds(off, kblk), :]
        uu = uu_ref[...]
        rows = slice(first_row, qblk)
        causal = None
        if causal_below is not None:
            shape = (qblk - first_row, kblk)
            col_minus_row = (lax.broadcasted_iota(jnp.int32, shape, 1)
                             - lax.broadcasted_iota(jnp.int32, shape, 0))
            causal = col_minus_row < causal_below + first_row
        zr = [z[hd][rows] for hd in heads]
        ls = [_log_sigmoid(zr[hd]) for hd in heads]
        lnb = [ls[hd] - zr[hd] for hd in heads]
        if causal is not None:
            lnb = [jnp.where(causal, x, 0.0) for x in lnb]
        tail = []
        for hd in heads:
            lnb_hi, lnb_lo = _split_bf16(lnb[hd])
            tail.append(_dot(jnp.concatenate([lnb_hi, lnb_lo], axis=1), uu))
        run = [run_ref[hd, rows, :] for hd in heads]
        w = [jnp.exp(ls[hd] + tail[hd] + run[hd]) for hd in heads]
        if causal is not None:
            w = [jnp.where(causal, x, 0.0) for x in w]
        pv = _dot(jnp.concatenate([w[hd].astype(BF16) for hd in heads], axis=1), per_head_rows(v))
        run_max = None
        for hd in heads:
            new_run = run[hd] + (tail[hd][:, 0:1] + lnb[hd][:, 0:1])
            run_ref[hd, rows, :] = new_run
            run_max = new_run if run_max is None else jnp.maximum(run_max, new_run)
        acc_ref[rows, :] += pv
        return jnp.max(run_max) > -SB_DEAD

    per_q = qblk // kblk
    top = i * per_q + per_q - 1
    z_now = scores(top)
    alive = None
    for d in range(per_q):
        z_next = scores(top - d - 1)
        if d == per_q - 1:
            for hd in heads:
                z_ref[hd] = z_next[hd]
        alive = key_block(top - d, z_now, (d + 1 - per_q) * kblk, first_row=(per_q - 1 - d) * kblk)
        z_now = z_next

    def cond(carry):
        j, alive = carry
        return jnp.logical_and(j >= 0, alive)

    def body(carry):
        j, _ = carry
        z = [z_ref[hd] for hd in heads]
        z_ahead = scores(j - 1)
        alive = key_block(j, z, None)
        for hd in heads:
            z_ref[hd] = z_ahead[hd]
        return j - 1, alive

    lax.while_loop(cond, body, (top - per_q, alive))
    o_ref[0] = acc_ref[...]


def _stick_breaking(q, k, v):
    b, s, _ = q.shape
    qblk, kblk = SB_QBLK, SB_KBLK
    idx = np.arange(kblk)
    u = (idx[:, None] > idx[None, :])
    uu = jnp.asarray(np.concatenate([u, u], axis=0), BF16)
    return pl.pallas_call(
        _sb_kernel,
        grid=(b, s // qblk),
        in_specs=[
            pl.BlockSpec((1, qblk, SLAB), lambda bi, i: (bi, i, 0)),
            pl.BlockSpec((1, s, SLAB), lambda bi, i: (bi, 0, 0), pipeline_mode=pl.Buffered(1)),
            pl.BlockSpec((1, s, SLAB), lambda bi, i: (bi, 0, 0), pipeline_mode=pl.Buffered(1)),
            _const_spec((2 * kblk, kblk)),
        ],
        out_specs=pl.BlockSpec((1, qblk, SLAB), lambda bi, i: (bi, i, 0)),
        out_shape=jax.ShapeDtypeStruct((b, s, SLAB), F32),
        scratch_shapes=[pltpu.VMEM((qblk, SLAB), F32),
                        pltpu.VMEM((N_HEADS, qblk, kblk), F32),
                        pltpu.VMEM((qblk, SLAB), BF16),
                        pltpu.VMEM((N_HEADS, qblk, kblk), F32)],
        compiler_params=_params(("arbitrary", "arbitrary")),
        name="stick_breaking",
    )(q, k, v, uu)


DIL_TILE = DIL_QBLK * max(r for _, r in DIL_PATTERNS)
DIL_UNROLL = 2


def _dil_kernel(q_ref, kp_ref, kc_ref, vp_ref, vc_ref, bias_ref, o_ref,
                k_all, v_all, num_s, den_s, mx_s):
    n = pl.program_id(1)
    t_len = DIL_TILE
    qb = DIL_QBLK
    k_all[:, 0:t_len, :] = kp_ref[0]
    k_all[:, t_len:2 * t_len, :] = kc_ref[0]
    v_all[:, 0:t_len, :] = vp_ref[0]
    v_all[:, t_len:2 * t_len, :] = vc_ref[0]
    num_s[...] = jnp.zeros_like(num_s)
    den_s[...] = jnp.zeros_like(den_s)
    mx_s[...] = jnp.full(mx_s.shape, NEG_BIG, F32)
    lane = lax.broadcasted_iota(jnp.int32, (1, LANES), 1)
    head_masks = [_head_of_lane(lane) == hh for hh in range(HEADS_PER_HALF)]

    for pi, (window, r) in enumerate(DIL_PATTERNS):
        assert window // r == qb
        last = pi == len(DIL_PATTERNS) - 1
        shift = r.bit_length() - 1
        tiles = t_len // qb

        def rows_of(start, size, r=r):
            return pl.ds(start, size) if r == 1 else pl.ds(start, size, stride=r)

        def tile_group(it, carry, r=r, shift=shift, last=last, rows_of=rows_of):
            q_rows, k_rows, bias = [], [], []
            for u in range(DIL_UNROLL):
                idx = it * DIL_UNROLL + u
                t = jnp.right_shift(idx, shift)
                j = jnp.bitwise_and(idx, r - 1)
                q0 = t * (qb * r) + j
                has_prev = jnp.logical_or(n > 0, t > 0)
                bias.append(bias_ref[jnp.where(has_prev, 0, 1)])
                q_rows.append(rows_of(q0, qb))
                k_rows.append(rows_of(t_len + q0 - qb * r, 2 * qb))
            slabs = [(u, half) for u in range(DIL_UNROLL) for half in range(N_HALVES)]
            units = [(si, hh) for si in range(len(slabs)) for hh in range(HEADS_PER_HALF)]
            qs = [q_ref[0, half, q_rows[u], :] for u, half in slabs]
            kb = [k_all[half, k_rows[u], :].astype(BF16) for u, half in slabs]
            vb = [v_all[half, k_rows[u], :].astype(BF16) for u, half in slabs]
            sc = [_dot_nt(jnp.where(head_masks[hh], qs[si], 0.0).astype(BF16), kb[si]) + bias[slabs[si][0]]
                  for si, hh in units]
            mx = [jnp.max(x, axis=-1, keepdims=True) for x in sc]
            p = [jnp.exp(x - m) for x, m in zip(sc, mx)]
            den = [jnp.sum(x, axis=-1, keepdims=True) for x in p]
            pv = [_dot(x.astype(BF16), vb[si]) for x, (si, _) in zip(p, units)]
            for si, (u, half) in enumerate(slabs):
                first = si * HEADS_PER_HALF
                num_t, den_t, mx_t = pv[first], den[first], mx[first]
                for hh in range(1, HEADS_PER_HALF):
                    hm = head_masks[hh]
                    num_t = jnp.where(hm, pv[first + hh], num_t)
                    den_t = jnp.where(hm, den[first + hh], den_t)
                    mx_t = jnp.where(hm, mx[first + hh], mx_t)
                rows = q_rows[u]
                m_old = mx_s[half, rows, :]
                m_new = jnp.maximum(m_old, mx_t)
                a_old = jnp.exp(m_old - m_new)
                a_new = jnp.exp(mx_t - m_new)
                num_new = num_s[half, rows, :] * a_old + num_t * a_new
                den_new = den_s[half, rows, :] * a_old + den_t * a_new
                if last:
                    o_ref[0, half, rows, :] = num_new / den_new
                else:
                    mx_s[half, rows, :] = jnp.broadcast_to(m_new, (qb, LANES))
                    num_s[half, rows, :] = num_new
                    den_s[half, rows, :] = jnp.broadcast_to(den_new, (qb, LANES))
            return carry

        lax.fori_loop(0, tiles // DIL_UNROLL, tile_group, 0)


def _dilated(q, k, v):
    b, _, s, _ = q.shape
    qb = DIL_QBLK
    t_len = DIL_TILE
    a_idx = np.arange(qb)[:, None]
    k_idx = np.arange(2 * qb)[None, :]
    dist = a_idx + qb - k_idx
    band = (dist >= 0) & (dist <= qb)
    bias = np.stack([np.where(band, 0.0, NEG_BIG), np.where(band & (k_idx >= qb), 0.0, NEG_BIG)])
    cur = pl.BlockSpec((1, N_HALVES, t_len, LANES), lambda bi, n: (bi, 0, n, 0))
    prev = pl.BlockSpec((1, N_HALVES, t_len, LANES), lambda bi, n: (bi, 0, jnp.maximum(n - 1, 0), 0))
    stat = pltpu.VMEM((N_HALVES, t_len, LANES), F32)
    both = pltpu.VMEM((N_HALVES, 2 * t_len, LANES), F32)
    return pl.pallas_call(
        _dil_kernel,
        grid=(b, s // t_len),
        in_specs=[cur, prev, cur, prev, cur, _const_spec((2, qb, 2 * qb))],
        out_specs=cur,
        out_shape=jax.ShapeDtypeStruct((b, N_HALVES, s, LANES), F32),
        scratch_shapes=[both, both, stat, stat, stat],
        compiler_params=_params(("arbitrary", "arbitrary")),
        name="dilated",
    )(q, k, k, v, v, jnp.asarray(bias, F32))


HGRN_CHUNK = 128
HGRN_LEVELS = HGRN_CHUNK.bit_length() - 1
HGRN_TILE = 1024
SUBLANES = 8


LOG2_E = 1.4426950408889634


def _log2_split_decay(g, log_f, half):
    c, w = g.shape
    block = 2 * half
    if block >= SUBLANES:
        g3 = g.reshape(c // block, block, w)
        upper = lax.broadcasted_iota(jnp.int32, (1, block, 1), 1) >= half
        return ((g3 - g3[:, half - 1:half, :]) * jnp.where(upper, LOG2_E, -LOG2_E)).reshape(c, w)
    pos = jnp.bitwise_and(lax.broadcasted_iota(jnp.int32, (c, 1), 0), block - 1)
    if half == 1:
        return jnp.where(pos == 1, log_f * LOG2_E, 0.0)
    assert half == 2
    nxt = pltpu.roll(log_f, c - 1, 0)
    prv = pltpu.roll(log_f, 1, 0)
    dist = jnp.where(pos == 0, nxt, jnp.where(pos == 1, 0.0, jnp.where(pos == 2, log_f, log_f + prv)))
    return dist * LOG2_E


def _hgrn_kernel(q_ref, f_ref, i_ref, g_ref, lbl_ref, ng_ref, tril_ref, lvl_ref, o_ref,
                 state_ref, *, layer):
    @pl.when(pl.program_id(1) == 0)
    def _():
        state_ref[...] = jnp.zeros_like(state_ref)

    c = HGRN_CHUNK
    chunks = range(HGRN_TILE // c)
    heads = range(HGRN_HEADS)
    lanes = [slice(hd * HGRN_D, (hd + 1) * HGRN_D) for hd in heads]
    rows = [slice(ci * c, (ci + 1) * c) for ci in chunks]
    norm_gain = ng_ref[...]
    tril = tril_ref[...]
    logits = lbl_ref[...]

    g_last, q_dec, k_dec, q_lvl, k_lvl = [], [], [], [], []
    for ci in chunks:
        zq = q_ref[0, rows[ci], :]
        log_f, one_m_f = _hgrn_gate_terms(f_ref[0, rows[ci], :], logits, layer)
        lf_hi, lf_lo = _split_bf16(log_f)
        g_c = _dot(tril, lf_hi) + _dot(tril, lf_lo)
        qq_c = zq * _sigmoid(zq)
        g_last.append(g_c[c - 1:c, :])
        q_dec.append((qq_c * jnp.exp(g_c)).astype(BF16))
        k_dec.append((one_m_f * jnp.exp(g_last[ci] - g_c)).astype(BF16))
        ql = [qq_c.astype(BF16)]
        kl = [one_m_f.astype(BF16)]
        for lv in range(HGRN_LEVELS):
            decay = jnp.exp2(_log2_split_decay(g_c, log_f, 1 << lv).astype(BF16))
            ql.append(ql[0] * decay)
            kl.append(kl[0] * decay)
        q_lvl.append(ql)
        k_lvl.append(kl)

    val = [i_ref[0, rows[ci], :] for ci in chunks]
    updates = [[_dot_tn(val[ci][:, lanes[hd]], k_dec[ci][:, lanes[hd]]) for hd in heads] for ci in chunks]
    states = [[state_ref[hd] for hd in heads]]
    for ci in chunks:
        states.append([states[ci][hd] * jnp.exp(g_last[ci][:, lanes[hd]]) + updates[ci][hd]
                       for hd in heads])
    for hd in heads:
        state_ref[hd] = states[-1][hd]
    o_inter = [[_dot_nt(q_dec[ci][:, lanes[hd]], states[ci][hd].astype(BF16)) for hd in heads]
               for ci in chunks]
    scores = [[[_dot_nt(q_lvl[ci][lv][:, lanes[hd]], k_lvl[ci][lv][:, lanes[hd]])
                for lv in range(HGRN_LEVELS + 1)] for hd in heads] for ci in chunks]

    lvl = lvl_ref[...]
    level_masks = [lvl == lv for lv in range(HGRN_LEVELS + 1)]
    for ci in chunks:
        outs = []
        for hd in heads:
            p = jnp.zeros((c, c), F32)
            for lv in range(HGRN_LEVELS + 1):
                p = jnp.where(level_masks[lv], scores[ci][hd][lv], p)
            o_h = _dot(p.astype(BF16), val[ci][:, lanes[hd]]) + o_inter[ci][hd]
            outs.append(o_h * _rms_scale(o_h) * norm_gain)
        zg = g_ref[0, rows[ci], :]
        o_ref[0, rows[ci], :] = jnp.concatenate(outs, axis=-1) * (zg * _sigmoid(zg))


def _hgrn_tables():
    c = HGRN_CHUNK
    idx = np.arange(c)
    tril = idx[:, None] >= idx[None, :]
    diff = idx[:, None] ^ idx[None, :]
    lvl = np.where(idx[:, None] > idx[None, :], np.floor(np.log2(np.maximum(diff, 1))) + 1, -1)
    lvl = np.where(idx[:, None] == idx[None, :], 0, lvl).astype(np.int32)
    return jnp.asarray(tril, BF16), jnp.asarray(lvl)


def _hgrn(qh, fh, ih, gh, lb_logits, norm_gain, layer):
    b, s, _ = qh.shape
    t = HGRN_TILE
    c = HGRN_CHUNK
    depth = lb_logits.shape[0]
    tril, lvl = _hgrn_tables()
    tok = pl.BlockSpec((1, t, HGRN_W), lambda bi, i: (bi, i, 0))
    return pl.pallas_call(
        functools.partial(_hgrn_kernel, layer=layer),
        grid=(b, s // t),
        in_specs=[tok, tok, tok, tok,
                  _const_spec((depth, HGRN_W)),
                  _const_spec((1, HGRN_D)),
                  _const_spec((c, c)),
                  _const_spec((c, c))],
        out_specs=tok,
        out_shape=jax.ShapeDtypeStruct((b, s, HGRN_W), F32),
        scratch_shapes=[pltpu.VMEM((HGRN_HEADS, HGRN_D, HGRN_D), F32)],
        compiler_params=_params(("arbitrary", "arbitrary")),
        name="hgrn2",
    )(qh, fh, ih, gh, lb_logits.astype(F32), norm_gain.reshape(1, HGRN_D), tril, lvl)


def _outproj_ffn_kernel(x_ref, mod_ref, oa_ref, od_ref, oh_ref, wo_ref, wg_ref, wu_ref, wd_ref, o_ref,
                        h_scr, acc_scr):
    y = _dot(oa_ref[0].astype(BF16), wo_ref[0:SLAB, :])
    for half in range(N_HALVES):
        lo = SLAB + half * LANES
        y += _dot(od_ref[0, half].astype(BF16), wo_ref[lo:lo + LANES, :])
    y += _dot(oh_ref[0].astype(BF16), wo_ref[2 * SLAB:, :])
    gate = mod_ref[0, 5:6, :]
    o_ref[0] = x_ref[0] + gate * y
    _ffn_half_step(o_ref, mod_ref, 6, wg_ref, wu_ref, wd_ref, h_scr, acc_scr)


def _outproj_ffn(x, mod_l, o_a, o_d, o_h, w_out, w_gate, w_up, w_down, layer, tm=512):
    b, s, d = x.shape

    def tok(width):
        return pl.BlockSpec((1, tm, width), lambda bi, i: (bi, i, 0))

    return pl.pallas_call(
        _outproj_ffn_kernel,
        grid=(b, s // tm),
        in_specs=[tok(d), pl.BlockSpec((1, N_MOD, d), lambda bi, i: (bi, 0, 0)), tok(SLAB),
                  pl.BlockSpec((1, N_HALVES, tm, LANES), lambda bi, i: (bi, 0, i, 0)),
                  tok(HGRN_W), _layer_spec(w_out.shape, layer),
                  _layer_spec(w_gate.shape, layer), _layer_spec(w_up.shape, layer),
                  _layer_spec(w_down.shape, layer)],
        out_specs=tok(d),
        out_shape=jax.ShapeDtypeStruct((b, s, d), F32),
        scratch_shapes=[pltpu.VMEM((tm, d), BF16), pltpu.VMEM((tm, d), F32)],
        compiler_params=_params(("arbitrary", "arbitrary")),
        name="outproj_ffn",
    )(x, mod_l, o_a, o_d, o_h, w_out, w_gate, w_up, w_down)


def kernel(x, c, w_mod, b_mod, ffn1_w_gate, ffn1_w_up, ffn1_w_down, w_in, w_out, q_norm_g, k_norm_g,
           hgrn_norm_g, hgrn_lb_logits, ffn2_w_gate, ffn2_w_up, ffn2_w_down):
    b, s, d = x.shape
    depth = w_mod.shape[0]
    assert s % DIL_TILE == 0 and s % HGRN_TILE == 0 and s % SB_QBLK == 0
    mod = _modulation(c, w_mod, b_mod).reshape(depth, b, N_MOD, d)
    cos, sin = _rope_tables(s)
    ffn1 = [w.astype(BF16) for w in (ffn1_w_gate, ffn1_w_up, ffn1_w_down)]
    ffn2 = [w.astype(BF16) for w in (ffn2_w_gate, ffn2_w_up, ffn2_w_down)]
    w_in_b = w_in.astype(BF16)
    w_out_b = w_out.astype(BF16)
    for l in range(depth):
        x = _ffn(x, mod[l], *ffn1, layer=l, row0=0)
        qa, ka, va, qd, kd, vd, qh, fh, ih, gh = _inproj(
            x, mod[l], w_in_b, l, q_norm_g[l], k_norm_g[l], cos, sin)
        o_a = _stick_breaking(qa, ka, va)
        o_d = _dilated(qd, kd, vd)
        o_h = _hgrn(qh, fh, ih, gh, hgrn_lb_logits, hgrn_norm_g[l], layer=l)
        x = _outproj_ffn(x, mod[l], o_a, o_d, o_h, w_out_b, *ffn2, layer=l)
    return x
```

```python
import functools

import jax
import jax.numpy as jnp
import numpy as np
from jax import lax
from jax.experimental import pallas as pl
from jax.experimental.pallas import tpu as pltpu

F32 = jnp.float32
BF16 = jnp.bfloat16

HEAD_DIM = 64
N_HEADS = 4
SLAB = N_HEADS * HEAD_DIM
LANES = 128
N_HALVES = SLAB // LANES
HEADS_PER_HALF = LANES // HEAD_DIM
HGRN_HEADS = 4
HGRN_D = 128
HGRN_W = HGRN_HEADS * HGRN_D
N_MOD = 9
EPS = 1e-6
LB_FLOOR = 1e-30
NEG_BIG = -1e30
HALF_STEP = 0.5
ROPE_THETA = 10000.0
DIL_PATTERNS = ((128, 1), (512, 4), (2048, 16))
DIL_QBLK = 128

OFF_QA, OFF_KA, OFF_VA = 0, 256, 512
OFF_QD, OFF_KD, OFF_VD = 768, 1024, 1280
OFF_QH, OFF_FH, OFF_IH, OFF_GH = 1536, 2048, 2560, 3072

VMEM_LIMIT = 56 * 1024 * 1024

SB_DEAD = 104.0


def _params(sem):
    return pltpu.CompilerParams(dimension_semantics=sem, vmem_limit_bytes=VMEM_LIMIT)


def _const_spec(shape):
    nd = len(shape)
    return pl.BlockSpec(shape, lambda *_: (0,) * nd, pipeline_mode=pl.Buffered(1))


def _split_bf16(x):
    hi = x.astype(BF16)
    lo = (x - hi.astype(F32)).astype(BF16)
    return hi, lo


def _dot(a, b):
    return jnp.dot(a, b, preferred_element_type=F32)


def _dot_nt(a, b):
    return lax.dot_general(a, b, (((1,), (1,)), ((), ())), preferred_element_type=F32)


def _dot_tn(a, b):
    return lax.dot_general(a, b, (((0,), (0,)), ((), ())), preferred_element_type=F32)


def _sigmoid(x):
    return 1.0 / (1.0 + jnp.exp(-x))


def _log_sigmoid(x):
    return jnp.minimum(x, 0.0) - jnp.log(1.0 + jnp.exp(-jnp.abs(x)))


def _rms_scale(x):
    return lax.rsqrt(jnp.mean(x * x, axis=-1, keepdims=True) + EPS)


def _head_of_lane(lane):
    return jnp.right_shift(lane, HEAD_DIM.bit_length() - 1)


def _mod_kernel(c_ref, w_ref, b_ref, o_ref):
    c = c_ref[...]
    sc = c * _sigmoid(c)
    sc_hi, sc_lo = _split_bf16(sc)
    w = w_ref[0]
    w_hi, w_lo = _split_bf16(w)
    o_ref[0] = _dot(sc_hi, w_hi) + (_dot(sc_hi, w_lo) + _dot(sc_lo, w_hi)) + b_ref[0]


def _modulation(c, w_mod, b_mod):
    depth, d, n = w_mod.shape
    rows = 16
    nb = c.shape[0]
    assert nb <= rows
    c = jnp.pad(c, ((0, rows - nb), (0, 0)))
    b = rows
    tn = 1152
    out = pl.pallas_call(
        _mod_kernel,
        grid=(depth, n // tn),
        in_specs=[
            pl.BlockSpec((b, d), lambda l, j: (0, 0)),
            pl.BlockSpec((1, d, tn), lambda l, j: (l, 0, j)),
            pl.BlockSpec((1, 1, tn), lambda l, j: (l, 0, j)),
        ],
        out_specs=pl.BlockSpec((1, b, tn), lambda l, j: (l, 0, j)),
        out_shape=jax.ShapeDtypeStruct((depth, b, n), F32),
        compiler_params=_params(("arbitrary", "arbitrary")),
        name="adaln_mod",
    )(c, w_mod, b_mod.reshape(depth, 1, n))
    return out[:, :nb]


FFN_CHUNK = 256


def _ffn_half_step(xo_ref, mod_ref, row0, wg_ref, wu_ref, wd_ref, h_scr, acc_scr):
    shift = mod_ref[0, row0:row0 + 1, :]
    scale = mod_ref[0, row0 + 1:row0 + 2, :]
    gate = mod_ref[0, row0 + 2:row0 + 3, :]
    x = xo_ref[0]
    h_scr[...] = (x * _rms_scale(x) * (1.0 + scale) + shift).astype(BF16)
    for c in range(wg_ref.shape[1] // FFN_CHUNK):
        cols = slice(c * FFN_CHUNK, (c + 1) * FFN_CHUNK)
        g = _dot(h_scr[...], wg_ref[:, cols])
        u = _dot(h_scr[...], wu_ref[:, cols])
        a = (g * _sigmoid(g) * u).astype(BF16)
        y = _dot(a, wd_ref[cols, :])
        if c == 0:
            acc_scr[...] = y
        else:
            acc_scr[...] += y
    xo_ref[0] = xo_ref[0] + (HALF_STEP * gate) * acc_scr[...]


def _ffn_kernel(x_ref, mod_ref, wg_ref, wu_ref, wd_ref, o_ref, h_scr, acc_scr, *, row0):
    o_ref[0] = x_ref[0]
    _ffn_half_step(o_ref, mod_ref, row0, wg_ref, wu_ref, wd_ref, h_scr, acc_scr)


def _layer_spec(shape, layer):
    nd = len(shape)
    return pl.BlockSpec((None,) + tuple(shape[1:]), lambda *_: (layer,) + (0,) * (nd - 1),
                        pipeline_mode=pl.Buffered(1))


def _ffn(x, mod_l, w_gate, w_up, w_down, layer, row0, tm=1024):
    b, s, d = x.shape
    return pl.pallas_call(
        functools.partial(_ffn_kernel, row0=row0),
        grid=(b, s // tm),
        in_specs=[
            pl.BlockSpec((1, tm, d), lambda bi, i: (bi, i, 0)),
            pl.BlockSpec((1, N_MOD, d), lambda bi, i: (bi, 0, 0)),
            _layer_spec(w_gate.shape, layer),
            _layer_spec(w_up.shape, layer),
            _layer_spec(w_down.shape, layer),
        ],
        out_specs=pl.BlockSpec((1, tm, d), lambda bi, i: (bi, i, 0)),
        out_shape=jax.ShapeDtypeStruct((b, s, d), F32),
        scratch_shapes=[pltpu.VMEM((tm, d), BF16), pltpu.VMEM((tm, d), F32)],
        compiler_params=_params(("arbitrary", "arbitrary")),
        name="ffn",
    )(x, mod_l, w_gate, w_up, w_down)


def _swap_halves(x):
    n = x.shape[-1]
    lane = lax.broadcasted_iota(jnp.int32, x.shape, x.ndim - 1)
    up = pltpu.roll(x, n - HEAD_DIM // 2, x.ndim - 1)
    down = pltpu.roll(x, HEAD_DIM // 2, x.ndim - 1)
    return jnp.where(jnp.bitwise_and(lane, HEAD_DIM - 1) < HEAD_DIM // 2, up, down)


def _qk_norm_rope(x, gain, cos, sin_signed, head_mean):
    sq_hi, sq_lo = _split_bf16(x * x)
    ms = _dot(sq_hi, head_mean) + _dot(sq_lo, head_mean)
    xn = x * lax.rsqrt(ms + EPS) * gain
    return xn * cos + _swap_halves(xn) * sin_signed


def _hgrn_gate_terms(z, logits, layer):
    ex = jnp.exp(logits - jnp.max(logits, axis=0, keepdims=True))
    sm = ex / jnp.sum(ex, axis=0, keepdims=True)
    lb = jnp.zeros((1, HGRN_W), F32)
    for l in range(1, layer + 1):
        lb = lb + sm[l:l + 1, :]
    lb = jnp.clip(lb, 0.0, 1.0 - EPS)
    lb_floor = jnp.maximum(lb, LB_FLOOR)
    log_lb = jnp.log(lb_floor)
    one_m_lb = 1.0 - lb
    e = jnp.exp(-jnp.abs(z))
    one_p_e = 1.0 + e
    bterm = jnp.log(one_m_lb) + (jnp.minimum(z, 0.0) - jnp.log(one_p_e))
    log_f = jnp.maximum(log_lb, bterm) + jnp.log(1.0 + jnp.exp(-jnp.abs(log_lb - bterm)))
    sigmoid_neg = jnp.where(z > 0.0, e, 1.0) / one_p_e
    one_m_f = one_m_lb * sigmoid_neg - (lb_floor - lb)
    return log_f, one_m_f


def _inproj_kernel(x_ref, mod_ref, w_ref, cos_ref, sin_ref, qg_ref, kg_ref, hm_ref,
                   qa_ref, ka_ref, va_ref, qd_ref, kd_ref, vd_ref, qh_ref, fh_ref, ih_ref, gh_ref):
    x = x_ref[0]
    shift = mod_ref[0, 3:4, :]
    scale = mod_ref[0, 4:5, :]
    h = (x * _rms_scale(x) * (1.0 + scale) + shift).astype(BF16)

    def proj(off, width):
        return _dot(h, w_ref[:, off:off + width])

    def put_halves(ref, val):
        for half in range(N_HALVES):
            ref[0, half] = val[:, half * LANES:(half + 1) * LANES]

    qd_raw = proj(OFF_QD, SLAB)
    kd_raw = proj(OFF_KD, SLAB)
    qh_ref[0] = proj(OFF_QH, HGRN_W)
    cos = cos_ref[...]
    sin = sin_ref[...]
    hm = hm_ref[...]
    qd = _qk_norm_rope(qd_raw, qg_ref[...], cos, sin, hm)
    put_halves(qd_ref, qd * (HEAD_DIM ** -0.5))
    put_halves(kd_ref, _qk_norm_rope(kd_raw, kg_ref[...], cos, sin, hm))
    fh_ref[0] = proj(OFF_FH, HGRN_W)
    gh_ref[0] = proj(OFF_GH, HGRN_W)
    ih_ref[0] = proj(OFF_IH, HGRN_W).astype(BF16)
    qa_ref[0] = proj(OFF_QA, SLAB)
    ka_ref[0] = proj(OFF_KA, SLAB).astype(BF16)
    va_ref[0] = proj(OFF_VA, SLAB).astype(BF16)
    put_halves(vd_ref, proj(OFF_VD, SLAB))


def _rope_tables(s):
    half = HEAD_DIM // 2
    inv_freq = ROPE_THETA ** (-jnp.arange(half, dtype=F32) * 2.0 / HEAD_DIM)
    ang = jnp.arange(s, dtype=F32)[:, None] * inv_freq[None, :]
    cos, sin = jnp.cos(ang), jnp.sin(ang)
    cos_full = jnp.tile(jnp.concatenate([cos, cos], axis=-1), (1, N_HEADS))
    sin_signed = jnp.tile(jnp.concatenate([-sin, sin], axis=-1), (1, N_HEADS))
    return cos_full, sin_signed


def _inproj(x, mod_l, w_in, layer, q_gain, k_gain, cos, sin, tm=1024):
    b, s, d = x.shape
    qg = jnp.tile(q_gain.reshape(1, HEAD_DIM), (1, N_HEADS))
    kg = jnp.tile(k_gain.reshape(1, HEAD_DIM), (1, N_HEADS))
    head_id = np.arange(SLAB) // HEAD_DIM
    head_mean = jnp.asarray(np.where(head_id[:, None] == head_id[None, :], 1.0 / HEAD_DIM, 0.0), BF16)

    def tok(width):
        return pl.BlockSpec((1, tm, width), lambda bi, i: (bi, i, 0))

    def out(width, dt):
        return jax.ShapeDtypeStruct((b, s, width), dt)

    halves = pl.BlockSpec((1, N_HALVES, tm, LANES), lambda bi, i: (bi, 0, i, 0))
    halves_out = jax.ShapeDtypeStruct((b, N_HALVES, s, LANES), F32)

    return pl.pallas_call(
        _inproj_kernel,
        grid=(b, s // tm),
        in_specs=[
            tok(d),
            pl.BlockSpec((1, N_MOD, d), lambda bi, i: (bi, 0, 0)),
            _layer_spec(w_in.shape, layer),
            pl.BlockSpec((tm, SLAB), lambda bi, i: (i, 0)),
            pl.BlockSpec((tm, SLAB), lambda bi, i: (i, 0)),
            _const_spec((1, SLAB)),
            _const_spec((1, SLAB)),
            _const_spec((SLAB, SLAB)),
        ],
        out_specs=[tok(SLAB)] * 3 + [halves] * 3 + [tok(HGRN_W)] * 4,
        out_shape=[out(SLAB, F32), out(SLAB, BF16), out(SLAB, BF16),
                   halves_out, halves_out, halves_out,
                   out(HGRN_W, F32), out(HGRN_W, F32), out(HGRN_W, BF16), out(HGRN_W, F32)],
        compiler_params=_params(("arbitrary", "arbitrary")),
        name="mixer_inproj",
    )(x, mod_l, w_in, cos, sin, qg, kg, head_mean)


SB_QBLK = 256
SB_KBLK = 128


def _sb_kernel(q_ref, k_ref, v_ref, uu_ref, o_ref, acc_ref, run_ref, qm_ref, z_ref):
    i = pl.program_id(1)
    qblk, kblk = SB_QBLK, SB_KBLK
    heads = range(N_HEADS)
    lane = lax.broadcasted_iota(jnp.int32, (1, SLAB), 1)
    head_masks = [_head_of_lane(lane) == hd for hd in heads]
    qm_ref[...] = (q_ref[0] * (HEAD_DIM ** -0.5)).astype(BF16)
    acc_ref[...] = jnp.zeros_like(acc_ref)
    run_ref[...] = jnp.zeros_like(run_ref)

    def per_head_rows(x):
        return jnp.concatenate([jnp.where(head_masks[hd], x, jnp.zeros_like(x)) for hd in heads], axis=0)

    def scores(j):
        off = pl.multiple_of(jnp.maximum(j, 0) * kblk, kblk)
        z_all = _dot_nt(qm_ref[...], per_head_rows(k_ref[0, pl.ds(off, kblk), :]))
        return [z_all[:, hd * kblk:(hd + 1) * kblk] for hd in heads]

    def key_block(j, z, causal_below, first_row=0):
        off = pl.multiple_of(j * kblk, kblk)
        v = v_ref[0, pl.ds(off, kblk), :]
        uu = uu_ref[...]
        rows = slice(first_row, qblk)
        causal = None
        if causal_below is not None:
            shape = (qblk - first_row, kblk)
            col_minus_row = (lax.broadcasted_iota(jnp.int32, shape, 1)
                             - lax.broadcasted_iota(jnp.int32, shape, 0))
            causal = col_minus_row < causal_below + first_row
        zr = [z[hd][rows] for hd in heads]
        ls = [_log_sigmoid(zr[hd]) for hd in heads]
        lnb = [ls[hd] - zr[hd] for hd in heads]
        if causal is not None:
            lnb = [jnp.where(causal, x, 0.0) for x in lnb]
        tail = []
        for hd in heads:
            lnb_hi, lnb_lo = _split_bf16(lnb[hd])
            tail.append(_dot(jnp.concatenate([lnb_hi, lnb_lo], axis=1), uu))
        run = [run_ref[hd, rows, :] for hd in heads]
        w = [jnp.exp(ls[hd] + tail[hd] + run[hd]) for hd in heads]
        if causal is not None:
            w = [jnp.where(causal, x, 0.0) for x in w]
        pv = _dot(jnp.concatenate([w[hd].astype(BF16) for hd in heads], axis=1), per_head_rows(v))
        run_max = None
        for hd in heads:
            new_run = run[hd] + (tail[hd][:, 0:1] + lnb[hd][:, 0:1])
            run_ref[hd, rows, :] = new_run
            run_max = new_run if run_max is None else jnp.maximum(run_max, new_run)
        acc_ref[rows, :] += pv
        return jnp.max(run_max) > -SB_DEAD

    per_q = qblk // kblk
    top = i * per_q + per_q - 1
    z_now = scores(top)
    alive = None
    for d in range(per_q):
        z_next = scores(top - d - 1)
        if d == per_q - 1:
            for hd in heads:
                z_ref[hd] = z_next[hd]
        alive = key_block(top - d, z_now, (d + 1 - per_q) * kblk, first_row=(per_q - 1 - d) * kblk)
        z_now = z_next

    def cond(carry):
        j, alive = carry
        return jnp.logical_and(j >= 0, alive)

    def body(carry):
        j, _ = carry
        z = [z_ref[hd] for hd in heads]
        z_ahead = scores(j - 1)
        alive = key_block(j, z, None)
        for hd in heads:
            z_ref[hd] = z_ahead[hd]
        return j - 1, alive

    lax.while_loop(cond, body, (top - per_q, alive))
    o_ref[0] = acc_ref[...]


def _stick_breaking(q, k, v):
    b, s, _ = q.shape
    qblk, kblk = SB_QBLK, SB_KBLK
    idx = np.arange(kblk)
    u = (idx[:, None] > idx[None, :])
    uu = jnp.asarray(np.concatenate([u, u], axis=0), BF16)
    return pl.pallas_call(
        _sb_kernel,
        grid=(b, s // qblk),
        in_specs=[
            pl.BlockSpec((1, qblk, SLAB), lambda bi, i: (bi, i, 0)),
            pl.BlockSpec((1, s, SLAB), lambda bi, i: (bi, 0, 0), pipeline_mode=pl.Buffered(1)),
            pl.BlockSpec((1, s, SLAB), lambda bi, i: (bi, 0, 0), pipeline_mode=pl.Buffered(1)),
            _const_spec((2 * kblk, kblk)),
        ],
        out_specs=pl.BlockSpec((1, qblk, SLAB), lambda bi, i: (bi, i, 0)),
        out_shape=jax.ShapeDtypeStruct((b, s, SLAB), F32),
        scratch_shapes=[pltpu.VMEM((qblk, SLAB), F32),
                        pltpu.VMEM((N_HEADS, qblk, kblk), F32),
                        pltpu.VMEM((qblk, SLAB), BF16),
                        pltpu.VMEM((N_HEADS, qblk, kblk), F32)],
        compiler_params=_params(("arbitrary", "arbitrary")),
        name="stick_breaking",
    )(q, k, v, uu)


DIL_TILE = DIL_QBLK * max(r for _, r in DIL_PATTERNS)
DIL_UNROLL = 2


def _dil_kernel(q_ref, kp_ref, kc_ref, vp_ref, vc_ref, bias_ref, o_ref,
                k_all, v_all, num_s, den_s, mx_s):
    n = pl.program_id(1)
    t_len = DIL_TILE
    qb = DIL_QBLK
    k_all[:, 0:t_len, :] = kp_ref[0]
    k_all[:, t_len:2 * t_len, :] = kc_ref[0]
    v_all[:, 0:t_len, :] = vp_ref[0]
    v_all[:, t_len:2 * t_len, :] = vc_ref[0]
    lane =lax.broadcasted_iota(jnp.int32, (1, LANES), 1)
    head_masks = [_head_of_lane(lane) == hh for hh in range(HEADS_PER_HALF)]

    for pi, (window, r) in enumerate(DIL_PATTERNS):
        assert window // r == qb
        opening = pi == 0
        last = pi == len(DIL_PATTERNS) - 1
        shift = r.bit_length() - 1
        tiles = t_len // qb

        def rows_of(start, size, r=r):
            return pl.ds(start, size) if r == 1 else pl.ds(start, size, stride=r)

        def tile_group(it, carry, r=r, shift=shift, opening=opening, last=last, rows_of=rows_of):
            q_rows, k_rows, bias = [], [], []
            for u in range(DIL_UNROLL):
                idx = it * DIL_UNROLL + u
                t = jnp.right_shift(idx, shift)
                j = jnp.bitwise_and(idx, r - 1)
                q0 = t * (qb * r) + j
                has_prev = jnp.logical_or(n > 0, t > 0)
                bias.append(bias_ref[jnp.where(has_prev, 0, 1)])
                q_rows.append(rows_of(q0, qb))
                k_rows.append(rows_of(t_len + q0 - qb * r, 2 * qb))
            slabs = [(u, half) for u in range(DIL_UNROLL) for half in range(N_HALVES)]
            units = [(si, hh) for si in range(len(slabs)) for hh in range(HEADS_PER_HALF)]
            qs = [q_ref[0, half, q_rows[u], :] for u, half in slabs]
            kb = [k_all[half, k_rows[u], :].astype(BF16) for u, half in slabs]
            vb = [v_all[half, k_rows[u], :].astype(BF16) for u, half in slabs]
            sc = [_dot_nt(jnp.where(head_masks[hh], qs[si], 0.0).astype(BF16), kb[si]) + bias[slabs[si][0]]
                  for si, hh in units]
            mx = [jnp.max(x, axis=-1, keepdims=True) for x in sc]
            p = [jnp.exp(x - m) for x, m in zip(sc, mx)]
            den = [jnp.sum(x, axis=-1, keepdims=True) for x in p]
            pv = [_dot(x.astype(BF16), vb[si]) for x, (si, _) in zip(p, units)]
            for si, (u, half) in enumerate(slabs):
                first = si * HEADS_PER_HALF
                num_t, den_t, mx_t = pv[first], den[first], mx[first]
                for hh in range(1, HEADS_PER_HALF):
                    hm = head_masks[hh]
                    num_t = jnp.where(hm, pv[first + hh], num_t)
                    den_t = jnp.where(hm, den[first + hh], den_t)
                    mx_t = jnp.where(hm, mx[first + hh], mx_t)
                rows = q_rows[u]
                if opening:
                    m_new, num_new, den_new = mx_t, num_t, den_t
                else:
                    m_old = mx_s[half, rows, :]
                    m_new = jnp.maximum(m_old, mx_t)
                    a_old = jnp.exp(m_old - m_new)
                    a_new = jnp.exp(mx_t - m_new)
                    num_new = num_s[half, rows, :] * a_old + num_t * a_new
                    den_new = den_s[half, rows, :] * a_old + den_t * a_new
                if last:
                    o_ref[0, half, rows, :] = num_new / den_new
                else:
                    mx_s[half, rows, :] = jnp.broadcast_to(m_new, (qb, LANES))
                    num_s[half, rows, :] = num_new
                    den_s[half, rows, :] = jnp.broadcast_to(den_new, (qb, LANES))
            return carry

        lax.fori_loop(0, tiles // DIL_UNROLL, tile_group, 0)


def _dilated(q, k, v):
    b, _, s, _ = q.shape
    qb = DIL_QBLK
    t_len = DIL_TILE
    a_idx = np.arange(qb)[:, None]
    k_idx = np.arange(2 * qb)[None, :]
    dist = a_idx + qb - k_idx
    band = (dist >= 0) & (dist <= qb)
    bias = np.stack([np.where(band, 0.0, NEG_BIG), np.where(band & (k_idx >= qb), 0.0, NEG_BIG)])
    cur = pl.BlockSpec((1, N_HALVES, t_len, LANES), lambda bi, n: (bi, 0, n, 0))
    prev = pl.BlockSpec((1, N_HALVES, t_len, LANES), lambda bi, n: (bi, 0, jnp.maximum(n - 1, 0), 0))
    stat = pltpu.VMEM((N_HALVES, t_len, LANES), F32)
    both = pltpu.VMEM((N_HALVES, 2 * t_len, LANES), F32)
    return pl.pallas_call(
        _dil_kernel,
        grid=(b, s // t_len),
        in_specs=[cur, prev, cur, prev, cur, _const_spec((2, qb, 2 * qb))],
        out_specs=cur,
        out_shape=jax.ShapeDtypeStruct((b, N_HALVES, s, LANES), F32),
        scratch_shapes=[both, both, stat, stat, stat],
        compiler_params=_params(("arbitrary", "arbitrary")),
        name="dilated",
    )(q, k, k, v, v, jnp.asarray(bias, F32))


HGRN_CHUNK = 128
HGRN_LEVELS = HGRN_CHUNK.bit_length() - 1
HGRN_TILE = 1024
SUBLANES = 8


LOG2_E = 1.4426950408889634


def _log2_split_decay(g, log_f, half):
    c, w = g.shape
    block = 2 * half
    if block >= SUBLANES:
        g3 = g.reshape(c // block, block, w)
        upper = lax.broadcasted_iota(jnp.int32, (1, block, 1), 1) >= half
        return ((g3 - g3[:, half - 1:half, :]) * jnp.where(upper, LOG2_E, -LOG2_E)).reshape(c, w)
    pos = jnp.bitwise_and(lax.broadcasted_iota(jnp.int32, (c, 1), 0), block - 1)
    if half == 1:
        return jnp.where(pos == 1, log_f * LOG2_E, 0.0)
    assert half == 2
    nxt = pltpu.roll(log_f, c - 1, 0)
    prv = pltpu.roll(log_f, 1, 0)
    dist = jnp.where(pos == 0, nxt, jnp.where(pos == 1, 0.0, jnp.where(pos == 2, log_f, log_f + prv)))
    return dist * LOG2_E


def _hgrn_kernel(q_ref, f_ref, i_ref, g_ref, lbl_ref, ng_ref, tril_ref, lvl_ref, o_ref,
                 state_ref, *, layer):
    @pl.when(pl.program_id(1) == 0)
    def _():
        state_ref[...] = jnp.zeros_like(state_ref)

    c = HGRN_CHUNK
    chunks = range(HGRN_TILE // c)
    heads = range(HGRN_HEADS)
    lanes = [slice(hd * HGRN_D, (hd + 1) * HGRN_D) for hd in heads]
    rows = [slice(ci * c, (ci + 1) * c) for ci in chunks]
    norm_gain = ng_ref[...]
    tril = tril_ref[...]
    logits = lbl_ref[...]

    g_last, q_dec, k_dec, q_lvl, k_lvl = [], [], [], [], []
    for ci in chunks:
        zq = q_ref[0, rows[ci], :]
        log_f, one_m_f = _hgrn_gate_terms(f_ref[0, rows[ci], :], logits, layer)
        lf_hi, lf_lo = _split_bf16(log_f)
        g_c = _dot(tril, lf_hi) + _dot(tril, lf_lo)
        qq_c = zq * _sigmoid(zq)
        g_last.append(g_c[c - 1:c, :])
        q_dec.append((qq_c * jnp.exp(g_c)).astype(BF16))
        k_dec.append((one_m_f * jnp.exp(g_last[ci] - g_c)).astype(BF16))
        ql = [qq_c.astype(BF16)]
        kl = [one_m_f.astype(BF16)]
        for lv in range(HGRN_LEVELS):
            decay = jnp.exp2(_log2_split_decay(g_c, log_f, 1 << lv).astype(BF16))
            ql.append(ql[0] * decay)
            kl.append(kl[0] * decay)
        q_lvl.append(ql)
        k_lvl.append(kl)

    val = [i_ref[0, rows[ci], :] for ci in chunks]
    updates = [[_dot_tn(val[ci][:, lanes[hd]], k_dec[ci][:, lanes[hd]]) for hd in heads] for ci in chunks]
    states = [[state_ref[hd] for hd in heads]]
    for ci in chunks:
        states.append([states[ci][hd] * jnp.exp(g_last[ci][:, lanes[hd]]) + updates[ci][hd]
                       for hd in heads])
    for hd in heads:
        state_ref[hd] = states[-1][hd]
    o_inter = [[_dot_nt(q_dec[ci][:, lanes[hd]], states[ci][hd].astype(BF16)) for hd in heads]
               for ci in chunks]
    scores = [[[_dot_nt(q_lvl[ci][lv][:, lanes[hd]], k_lvl[ci][lv][:, lanes[hd]])
                for lv in range(HGRN_LEVELS + 1)] for hd in heads] for ci in chunks]

    lvl = lvl_ref[...]
    level_masks = [lvl == lv for lv in range(HGRN_LEVELS + 1)]
    for ci in chunks:
        outs = []
        for hd in heads:
            p = jnp.zeros((c, c), F32)
            for lv in range(HGRN_LEVELS + 1):
                p = jnp.where(level_masks[lv], scores[ci][hd][lv], p)
            o_h = _dot(p.astype(BF16), val[ci][:, lanes[hd]]) + o_inter[ci][hd]
            outs.append(o_h * _rms_scale(o_h) * norm_gain)
        zg = g_ref[0, rows[ci], :]
        o_ref[0, rows[ci], :] = jnp.concatenate(outs, axis=-1) * (zg * _sigmoid(zg))


def _hgrn_tables():
    c = HGRN_CHUNK
    idx = np.arange(c)
    tril = idx[:, None] >= idx[None, :]
    diff = idx[:, None] ^ idx[None, :]
    lvl = np.where(idx[:, None] > idx[None, :], np.floor(np.log2(np.maximum(diff, 1))) + 1, -1)
    lvl = np.where(idx[:, None] == idx[None, :], 0, lvl).astype(np.int32)
    return jnp.asarray(tril, BF16), jnp.asarray(lvl)


def _hgrn(qh, fh, ih, gh, lb_logits, norm_gain, layer):
    b, s, _ = qh.shape
    t = HGRN_TILE
    c = HGRN_CHUNK
    depth = lb_logits.shape[0]
    tril, lvl = _hgrn_tables()
    tok = pl.BlockSpec((1, t, HGRN_W), lambda bi, i: (bi, i, 0))
    return pl.pallas_call(
        functools.partial(_hgrn_kernel, layer=layer),
        grid=(b, s // t),
        in_specs=[tok, tok, tok, tok,
                  _const_spec((depth, HGRN_W)),
                  _const_spec((1, HGRN_D)),
                  _const_spec((c, c)),
                  _const_spec((c, c))],
        out_specs=tok,
        out_shape=jax.ShapeDtypeStruct((b, s, HGRN_W), F32),
        scratch_shapes=[pltpu.VMEM((HGRN_HEADS, HGRN_D, HGRN_D), F32)],
        compiler_params=_params(("arbitrary", "arbitrary")),
        name="hgrn2",
    )(qh, fh, ih, gh, lb_logits.astype(F32), norm_gain.reshape(1, HGRN_D), tril, lvl)


def _outproj_ffn_kernel(x_ref, mod_ref, oa_ref, od_ref, oh_ref, wo_ref, wg_ref, wu_ref, wd_ref, o_ref,
                        h_scr, acc_scr):
    y = _dot(oa_ref[0].astype(BF16), wo_ref[0:SLAB, :])
    for half in range(N_HALVES):
        lo = SLAB + half * LANES
        y += _dot(od_ref[0, half].astype(BF16), wo_ref[lo:lo + LANES, :])
    y += _dot(oh_ref[0].astype(BF16), wo_ref[2 * SLAB:, :])
    gate = mod_ref[0, 5:6, :]
    o_ref[0] = x_ref[0] + gate * y
    _ffn_half_step(o_ref, mod_ref, 6, wg_ref, wu_ref, wd_ref, h_scr, acc_scr)


def _outproj_ffn(x, mod_l, o_a, o_d, o_h, w_out, w_gate, w_up, w_down, layer, tm=512):
    b, s, d = x.shape

    def tok(width):
        return pl.BlockSpec((1, tm, width), lambda bi, i: (bi, i, 0))

    return pl.pallas_call(
        _outproj_ffn_kernel,
        grid=(b, s // tm),
        in_specs=[tok(d), pl.BlockSpec((1, N_MOD, d), lambda bi, i: (bi, 0, 0)), tok(SLAB),
                  pl.BlockSpec((1, N_HALVES, tm, LANES), lambda bi, i: (bi, 0, i, 0)),
                  tok(HGRN_W), _layer_spec(w_out.shape, layer),
                  _layer_spec(w_gate.shape, layer), _layer_spec(w_up.shape, layer),
                  _layer_spec(w_down.shape, layer)],
        out_specs=tok(d),
        out_shape=jax.ShapeDtypeStruct((b, s, d), F32),
        scratch_shapes=[pltpu.VMEM((tm, d), BF16), pltpu.VMEM((tm, d), F32)],
        compiler_params=_params(("arbitrary", "arbitrary")),
        name="outproj_ffn",
    )(x, mod_l, o_a, o_d, o_h, w_out, w_gate, w_up, w_down)


def kernel(x, c, w_mod, b_mod, ffn1_w_gate, ffn1_w_up, ffn1_w_down, w_in, w_out, q_norm_g, k_norm_g,
           hgrn_norm_g, hgrn_lb_logits, ffn2_w_gate, ffn2_w_up, ffn2_w_down):
    b, s, d = x.shape
    depth = w_mod.shape[0]
    assert s % DIL_TILE == 0 and s % HGRN_TILE == 0 and s % SB_QBLK == 0
    mod = _modulation(c, w_mod, b_mod).reshape(depth, b, N_MOD, d)
    cos, sin = _rope_tables(s)
    ffn1 = [w.astype(BF16) for w in (ffn1_w_gate, ffn1_w_up, ffn1_w_down)]
    ffn2 = [w.astype(BF16) for w in (ffn2_w_gate, ffn2_w_up, ffn2_w_down)]
    w_in_b = w_in.astype(BF16)
    w_out_b = w_out.astype(BF16)
    for l in range(depth):
        x = _ffn(x, mod[l], *ffn1, layer=l, row0=0)
        qa, ka, va, qd, kd, vd, qh, fh, ih, gh = _inproj(
            x, mod[l], w_in_b, l, q_norm_g[l], k_norm_g[l], cos, sin)
        o_a = _stick_breaking(qa, ka, va)
        o_d = _dilated(qd, kd, vd)
        o_h = _hgrn(qh, fh, ih, gh, hgrn_lb_logits, hgrn_norm_g[l], layer=l)
        x = _outproj_ffn(x, mod[l], o_a, o_d, o_h, w_out_b, *ffn2, layer=l)
    return x
```

```python
import functools

import jax
import jax.numpy as jnp
import numpy as np
from jax import lax
from jax.experimental import pallas as pl
from jax.experimental.pallas import tpu as pltpu

F32 = jnp.float32
BF16 = jnp.bfloat16

HEAD_DIM = 64
N_HEADS = 4
SLAB = N_HEADS * HEAD_DIM
LANES = 128
N_HALVES = SLAB // LANES
HEADS_PER_HALF = LANES // HEAD_DIM
HGRN_HEADS = 4
HGRN_D = 128
HGRN_W = HGRN_HEADS * HGRN_D
N_MOD = 9
EPS = 1e-6
LB_FLOOR = 1e-30
NEG_BIG = -1e30
HALF_STEP = 0.5
ROPE_THETA = 10000.0
DIL_PATTERNS = ((128, 1), (512, 4), (2048, 16))
DIL_QBLK = 128

OFF_QA, OFF_KA, OFF_VA = 0, 256, 512
OFF_QD, OFF_KD, OFF_VD = 768, 1024, 1280
OFF_QH, OFF_FH, OFF_IH, OFF_GH = 1536, 2048, 2560, 3072

VMEM_LIMIT = 56 * 1024 * 1024

SB_DEAD = 104.0


def _params(sem):
    return pltpu.CompilerParams(dimension_semantics=sem, vmem_limit_bytes=VMEM_LIMIT)


def _const_spec(shape):
    nd = len(shape)
    return pl.BlockSpec(shape, lambda *_: (0,) * nd, pipeline_mode=pl.Buffered(1))


def _split_bf16(x):
    hi = x.astype(BF16)
    lo = (x - hi.astype(F32)).astype(BF16)
    return hi, lo


def _dot(a, b):
    return jnp.dot(a, b, preferred_element_type=F32)


def _dot_nt(a, b):
    return lax.dot_general(a, b, (((1,), (1,)), ((), ())), preferred_element_type=F32)


def _dot_tn(a, b):
    return lax.dot_general(a, b, (((0,), (0,)), ((), ())), preferred_element_type=F32)


def _sigmoid(x):
    return 1.0 / (1.0 + jnp.exp(-x))


def _log_sigmoid(x):
    return jnp.minimum(x, 0.0) - jnp.log(1.0 + jnp.exp(-jnp.abs(x)))


def _rms_scale(x):
    return lax.rsqrt(jnp.mean(x * x, axis=-1, keepdims=True) + EPS)


def _head_of_lane(lane):
    return jnp.right_shift(lane, HEAD_DIM.bit_length() - 1)


def _mod_kernel(c_ref, w_ref, b_ref, o_ref):
    c = c_ref[...]
    sc = c * _sigmoid(c)
    sc_hi, sc_lo = _split_bf16(sc)
    w = w_ref[0]
    w_hi, w_lo = _split_bf16(w)
    o_ref[0] = _dot(sc_hi, w_hi) + (_dot(sc_hi, w_lo) + _dot(sc_lo, w_hi)) + b_ref[0]


def _modulation(c, w_mod, b_mod):
    depth, d, n = w_mod.shape
    rows = 16
    nb = c.shape[0]
    assert nb <= rows
    c = jnp.pad(c, ((0, rows - nb), (0, 0)))
    b = rows
    tn = 1152
    out = pl.pallas_call(
        _mod_kernel,
        grid=(depth, n // tn),
        in_specs=[
            pl.BlockSpec((b, d), lambda l, j: (0, 0)),
            pl.BlockSpec((1, d, tn), lambda l, j: (l, 0, j)),
            pl.BlockSpec((1, 1, tn), lambda l, j: (l, 0, j)),
        ],
        out_specs=pl.BlockSpec((1, b, tn), lambda l, j: (l, 0, j)),
        out_shape=jax.ShapeDtypeStruct((depth, b, n), F32),
        compiler_params=_params(("arbitrary", "arbitrary")),
        name="adaln_mod",
    )(c, w_mod, b_mod.reshape(depth, 1, n))
    return out[:, :nb]


FFN_CHUNK = 256


def _ffn_half_step(xo_ref, mod_ref, row0, wg_ref, wu_ref, wd_ref, h_scr, acc_scr):
    shift = mod_ref[0, row0:row0 + 1, :]
    scale = mod_ref[0, row0 + 1:row0 + 2, :]
    gate = mod_ref[0, row0 + 2:row0 + 3, :]
    x = xo_ref[0]
    h_scr[...] = (x * _rms_scale(x) * (1.0 + scale) + shift).astype(BF16)
    for c in range(wg_ref.shape[1] // FFN_CHUNK):
        cols = slice(c * FFN_CHUNK, (c + 1) * FFN_CHUNK)
        g = _dot(h_scr[...], wg_ref[:, cols])
        u = _dot(h_scr[...], wu_ref[:, cols])
        a = (g * _sigmoid(g) * u).astype(BF16)
        y = _dot(a, wd_ref[cols, :])
        if c == 0:
            acc_scr[...] = y
        else:
            acc_scr[...] += y
    xo_ref[0] = xo_ref[0] + (HALF_STEP * gate) * acc_scr[...]


def _ffn_kernel(x_ref, mod_ref, wg_ref, wu_ref, wd_ref, o_ref, h_scr, acc_scr, *, row0):
    o_ref[0] = x_ref[0]
    _ffn_half_step(o_ref, mod_ref, row0, wg_ref, wu_ref, wd_ref, h_scr, acc_scr)


def _layer_spec(shape, layer):
    nd = len(shape)
    return pl.BlockSpec((None,) + tuple(shape[1:]), lambda *_: (layer,) + (0,) * (nd - 1),
                        pipeline_mode=pl.Buffered(1))


def _ffn(x, mod_l, w_gate, w_up, w_down, layer, row0, tm=1024):
    b, s, d = x.shape
    return pl.pallas_call(
        functools.partial(_ffn_kernel, row0=row0),
        grid=(b, s // tm),
        in_specs=[
            pl.BlockSpec((1, tm, d), lambda bi, i: (bi, i, 0)),
            pl.BlockSpec((1, N_MOD, d), lambda bi, i: (bi, 0, 0)),
            _layer_spec(w_gate.shape, layer),
            _layer_spec(w_up.shape, layer),
            _layer_spec(w_down.shape, layer),
        ],
        out_specs=pl.BlockSpec((1, tm, d), lambda bi, i: (bi, i, 0)),
        out_shape=jax.ShapeDtypeStruct((b, s, d), F32),
        scratch_shapes=[pltpu.VMEM((tm, d), BF16), pltpu.VMEM((tm, d), F32)],
        compiler_params=_params(("arbitrary", "arbitrary")),
        name="ffn",
    )(x, mod_l, w_gate, w_up, w_down)


def _swap_halves(x):
    n = x.shape[-1]
    lane = lax.broadcasted_iota(jnp.int32, x.shape, x.ndim - 1)
    up = pltpu.roll(x, n - HEAD_DIM // 2, x.ndim - 1)
    down = pltpu.roll(x, HEAD_DIM // 2, x.ndim - 1)
    return jnp.where(jnp.bitwise_and(lane, HEAD_DIM - 1) < HEAD_DIM // 2, up, down)


def _qk_norm_rope(x, gain, cos, sin_signed, head_mean):
    sq_hi, sq_lo = _split_bf16(x * x)
    ms = _dot(sq_hi, head_mean) + _dot(sq_lo, head_mean)
    xn = x * lax.rsqrt(ms + EPS) * gain
    return xn * cos + _swap_halves(xn) * sin_signed


def _hgrn_gate_terms(z, logits, layer):
    ex = jnp.exp(logits - jnp.max(logits, axis=0, keepdims=True))
    sm = ex / jnp.sum(ex, axis=0, keepdims=True)
    lb = jnp.zeros((1, HGRN_W), F32)
    for l in range(1, layer + 1):
        lb = lb + sm[l:l + 1, :]
    lb = jnp.clip(lb, 0.0, 1.0 - EPS)
    lb_floor = jnp.maximum(lb, LB_FLOOR)
    log_lb = jnp.log(lb_floor)
    one_m_lb = 1.0 - lb
    e = jnp.exp(-jnp.abs(z))
    one_p_e = 1.0 + e
    bterm = jnp.log(one_m_lb) + (jnp.minimum(z, 0.0) - jnp.log(one_p_e))
    log_f = jnp.maximum(log_lb, bterm) + jnp.log(1.0 + jnp.exp(-jnp.abs(log_lb - bterm)))
    sigmoid_neg = jnp.where(z > 0.0, e, 1.0) / one_p_e
    one_m_f = one_m_lb * sigmoid_neg - (lb_floor - lb)
    return log_f, one_m_f


def _inproj_kernel(x_ref, mod_ref, w_ref, cos_ref, sin_ref, qg_ref, kg_ref, hm_ref,
                   qa_ref, ka_ref, va_ref, qd_ref, kd_ref, vd_ref, qh_ref, fh_ref, ih_ref, gh_ref):
    x = x_ref[0]
    shift = mod_ref[0, 3:4, :]
    scale = mod_ref[0, 4:5, :]
    h = (x * _rms_scale(x) * (1.0 + scale) + shift).astype(BF16)

    def proj(off, width):
        return _dot(h, w_ref[:, off:off + width])

    def put_halves(ref, val):
        for half in range(N_HALVES):
            ref[0, half] = val[:, half * LANES:(half + 1) * LANES]

    qd_raw = proj(OFF_QD, SLAB)
    kd_raw = proj(OFF_KD, SLAB)
    qh_ref[0] = proj(OFF_QH, HGRN_W)
    cos = cos_ref[...]
    sin = sin_ref[...]
    hm = hm_ref[...]
    qd = _qk_norm_rope(qd_raw, qg_ref[...], cos, sin, hm)
    put_halves(qd_ref, qd * (HEAD_DIM ** -0.5))
    put_halves(kd_ref, _qk_norm_rope(kd_raw, kg_ref[...], cos, sin, hm))
    fh_ref[0] = proj(OFF_FH, HGRN_W)
    gh_ref[0] = proj(OFF_GH, HGRN_W)
    ih_ref[0] = proj(OFF_IH, HGRN_W).astype(BF16)
    qa_ref[0] = proj(OFF_QA, SLAB)
    ka_ref[0] = proj(OFF_KA, SLAB).astype(BF16)
    va_ref[0] = proj(OFF_VA, SLAB).astype(BF16)
    put_halves(vd_ref, proj(OFF_VD, SLAB))


def _rope_tables(s):
    half = HEAD_DIM // 2
    inv_freq = ROPE_THETA ** (-jnp.arange(half, dtype=F32) * 2.0 / HEAD_DIM)
    ang = jnp.arange(s, dtype=F32)[:, None] * inv_freq[None, :]
    cos, sin = jnp.cos(ang), jnp.sin(ang)
    cos_full = jnp.tile(jnp.concatenate([cos, cos], axis=-1), (1, N_HEADS))
    sin_signed = jnp.tile(jnp.concatenate([-sin, sin], axis=-1), (1, N_HEADS))
    return cos_full, sin_signed


def _inproj(x, mod_l, w_in, layer, q_gain, k_gain, cos, sin, tm=1024):
    b, s, d = x.shape
    qg = jnp.tile(q_gain.reshape(1, HEAD_DIM), (1, N_HEADS))
    kg = jnp.tile(k_gain.reshape(1, HEAD_DIM), (1, N_HEADS))
    head_id = np.arange(SLAB) // HEAD_DIM
    head_mean = jnp.asarray(np.where(head_id[:, None] == head_id[None, :], 1.0 / HEAD_DIM, 0.0), BF16)

    def tok(width):
        return pl.BlockSpec((1, tm, width), lambda bi, i: (bi, i, 0))

    def out(width, dt):
        return jax.ShapeDtypeStruct((b, s, width), dt)

    halves = pl.BlockSpec((1, N_HALVES, tm, LANES), lambda bi, i: (bi, 0, i, 0))
    halves_out = jax.ShapeDtypeStruct((b, N_HALVES, s, LANES), F32)

    return pl.pallas_call(
        _inproj_kernel,
        grid=(b, s // tm),
        in_specs=[
            tok(d),
            pl.BlockSpec((1, N_MOD, d), lambda bi, i: (bi, 0, 0)),
            _layer_spec(w_in.shape, layer),
            pl.BlockSpec((tm, SLAB), lambda bi, i: (i, 0)),
            pl.BlockSpec((tm, SLAB), lambda bi, i: (i, 0)),
            _const_spec((1, SLAB)),
            _const_spec((1, SLAB)),
            _const_spec((SLAB, SLAB)),
        ],
        out_specs=[tok(SLAB)] * 3 + [halves] * 3 + [tok(HGRN_W)] * 4,
        out_shape=[out(SLAB, F32), out(SLAB, BF16), out(SLAB, BF16),
                   halves_out, halves_out, halves_out,
                   out(HGRN_W, F32), out(HGRN_W, F32), out(HGRN_W, BF16), out(HGRN_W, F32)],
        compiler_params=_params(("arbitrary", "arbitrary")),
        name="mixer_inproj",
    )(x, mod_l, w_in, cos, sin, qg, kg, head_mean)


SB_QBLK = 256
SB_KBLK = 128


def _sb_kernel(q_ref, k_ref, v_ref, uu_ref, o_ref, acc_ref, run_ref, qm_ref, z_ref):
    i = pl.program_id(1)
    qblk, kblk = SB_QBLK, SB_KBLK
    heads = range(N_HEADS)
    lane = lax.broadcasted_iota(jnp.int32, (1, SLAB), 1)
    head_masks = [_head_of_lane(lane) == hd for hd in heads]
    qm_ref[...] = (q_ref[0] * (HEAD_DIM ** -0.5)).astype(BF16)
    acc_ref[...] = jnp.zeros_like(acc_ref)
    run_ref[...] = jnp.zeros_like(run_ref)

    def per_head_rows(x):
        return jnp.concatenate([jnp.where(head_masks[hd], x, jnp.zeros_like(x)) for hd in heads], axis=0)

    def scores(j):
        off = pl.multiple_of(jnp.maximum(j, 0) * kblk, kblk)
        z_all = _dot_nt(qm_ref[...], per_head_rows(k_ref[0, pl.ds(off, kblk), :]))
        return [z_all[:, hd * kblk:(hd + 1) * kblk] for hd in heads]

    def key_block(j, z, causal_below, first_row=0):
        off = pl.multiple_of(j * kblk, kblk)
        v = v_ref[0, pl.ds(off, kblk), :]
        uu = uu_ref[...]
        rows = slice(first_row, qblk)
        causal = None
        if causal_below is not None:
            shape = (qblk - first_row, kblk)
            col_minus_row = (lax.broadcasted_iota(jnp.int32, shape, 1)
                             - lax.broadcasted_iota(jnp.int32, shape, 0))
            causal = col_minus_row < causal_below + first_row
        zr = [z[hd][rows] for hd in heads]
        ls = [_log_sigmoid(zr[hd]) for hd in heads]
        lnb = [ls[hd] - zr[hd] for hd in heads]
        if causal is not None:
            lnb = [jnp.where(causal, x, 0.0) for x in lnb]
        tail = []
        for hd in heads:
            lnb_hi, lnb_lo = _split_bf16(lnb[hd])
            tail.append(_dot(jnp.concatenate([lnb_hi, lnb_lo], axis=1), uu))
        run = [run_ref[hd, rows, :] for hd in heads]
        w = [jnp.exp(ls[hd] + tail[hd] + run[hd]) for hd in heads]
        if causal is not None:
            w = [jnp.where(causal, x, 0.0) for x in w]
        pv = _dot(jnp.concatenate([w[hd].astype(BF16) for hd in heads], axis=1), per_head_rows(v))
        run_max = None
        for hd in heads:
            new_run = run[hd] + (tail[hd][:, 0:1] + lnb[hd][:, 0:1])
            run_ref[hd, rows, :] = new_run
            run_max = new_run if run_max is None else jnp.maximum(run_max, new_run)
        acc_ref[rows, :] += pv
        return jnp.max(run_max) > -SB_DEAD

    per_q = qblk // kblk
    top = i * per_q + per_q - 1
    z_now = scores(top)
    alive = None
    for d in range(per_q):
        z_next = scores(top - d - 1)
        if d == per_q - 1:
            for hd in heads:
                z_ref[hd] = z_next[hd]
        alive = key_block(top - d, z_now, (d + 1 - per_q) * kblk, first_row=(per_q - 1 - d) * kblk)
        z_now = z_next

    def cond(carry):
        j, alive = carry
        return jnp.logical_and(j >= 0, alive)

    def body(carry):
        j, _ = carry
        z = [z_ref[hd] for hd in heads]
        z_ahead = scores(j - 1)
        alive = key_block(j, z, None)
        for hd in heads:
            z_ref[hd] = z_ahead[hd]
        return j - 1, alive

    lax.while_loop(cond, body, (top - per_q, alive))
    o_ref[0] = acc_ref[...]


def _stick_breaking(q, k, v):
    b, s, _ = q.shape
    qblk, kblk = SB_QBLK, SB_KBLK
    idx = np.arange(kblk)
    u = (idx[:, None] > idx[None, :])
    uu = jnp.asarray(np.concatenate([u, u], axis=0), BF16)
    return pl.pallas_call(
        _sb_kernel,
        grid=(b, s // qblk),
        in_specs=[
            pl.BlockSpec((1, qblk, SLAB), lambda bi, i: (bi, i, 0)),
            pl.BlockSpec((1, s, SLAB), lambda bi, i: (bi, 0, 0), pipeline_mode=pl.Buffered(1)),
            pl.BlockSpec((1, s, SLAB), lambda bi, i: (bi, 0, 0), pipeline_mode=pl.Buffered(1)),
            _const_spec((2 * kblk, kblk)),
        ],
        out_specs=pl.BlockSpec((1, qblk, SLAB), lambda bi, i: (bi, i, 0)),
        out_shape=jax.ShapeDtypeStruct((b, s, SLAB), F32),
        scratch_shapes=[pltpu.VMEM((qblk, SLAB), F32),
                        pltpu.VMEM((N_HEADS, qblk, kblk), F32),
                        pltpu.VMEM((qblk, SLAB), BF16),
                        pltpu.VMEM((N_HEADS, qblk, kblk), F32)],
        compiler_params=_params(("arbitrary", "arbitrary")),
        name="stick_breaking",
    )(q, k, v, uu)


DIL_TILE = DIL_QBLK * max(r for _, r in DIL_PATTERNS)
DIL_UNROLL = 2


def _dil_kernel(q_ref, kp_ref, kc_ref, vp_ref, vc_ref, bias_ref, o_ref,
                k_all, v_all, num_s, den_s, mx_s):
    n = pl.program_id(1)
    t_len = DIL_TILE
    qb = DIL_QBLK
    k_all[:, 0:t_len, :] = kp_ref[0]
    k_all[:, t_len:2 * t_len, :] = kc_ref[0]
    v_all[:, 0:t_len, :] = vp_ref[0]
    v_all[:, t_len:2 * t_len, :] = vc_ref[0]
    lane =lax.broadcasted_iota(jnp.int32, (1, LANES), 1)
    head_masks = [_head_of_lane(lane) == hh for hh in range(HEADS_PER_HALF)]

    for pi, (window, r) in enumerate(reversed(DIL_PATTERNS)):
        assert window // r == qb
        opening = pi == 0
        last = pi == len(DIL_PATTERNS) - 1
        shift = r.bit_length() - 1
        tiles = t_len // qb

        def rows_of(start, size, r=r):
            return pl.ds(start, size) if r == 1 else pl.ds(start, size, stride=r)

        def tile_group(it, carry, r=r, shift=shift, opening=opening, last=last, rows_of=rows_of):
            q_rows, k_rows, bias = [], [], []
            for u in range(DIL_UNROLL):
                idx = it * DIL_UNROLL + u
                t = jnp.right_shift(idx, shift)
                j = jnp.bitwise_and(idx, r - 1)
                q0 = t * (qb * r) + j
                has_prev = jnp.logical_or(n > 0, t > 0)
                bias.append(bias_ref[jnp.where(has_prev, 0, 1)])
                q_rows.append(rows_of(q0, qb))
                k_rows.append(rows_of(t_len + q0 - qb * r, 2 * qb))
            slabs = [(u, half) for u in range(DIL_UNROLL) for half in range(N_HALVES)]
            units = [(si, hh) for si in range(len(slabs)) for hh in range(HEADS_PER_HALF)]
            qs = [q_ref[0, half, q_rows[u], :] for u, half in slabs]
            kb = [k_all[half, k_rows[u], :].astype(BF16) for u, half in slabs]
            vb = [v_all[half, k_rows[u], :].astype(BF16) for u, half in slabs]
            sc = [_dot_nt(jnp.where(head_masks[hh], qs[si], 0.0).astype(BF16), kb[si]) + bias[slabs[si][0]]
                  for si, hh in units]
            mx = [jnp.max(x, axis=-1, keepdims=True) for x in sc]
            p = [jnp.exp(x - m) for x, m in zip(sc, mx)]
            den = [jnp.sum(x, axis=-1, keepdims=True) for x in p]
            pv = [_dot(x.astype(BF16), vb[si]) for x, (si, _) in zip(p, units)]
            for si, (u, half) in enumerate(slabs):
                first = si * HEADS_PER_HALF
                num_t, den_t, mx_t = pv[first], den[first], mx[first]
                for hh in range(1, HEADS_PER_HALF):
                    hm = head_masks[hh]
                    num_t = jnp.where(hm, pv[first + hh], num_t)
                    den_t = jnp.where(hm, den[first + hh], den_t)
                    mx_t = jnp.where(hm, mx[first + hh], mx_t)
                rows = q_rows[u]
                if opening:
                    m_new, num_new, den_new = mx_t, num_t, den_t
                else:
                    m_old = mx_s[half, rows, :]
                    m_new = jnp.maximum(m_old, mx_t)
                    a_old = jnp.exp(m_old - m_new)
                    a_new = jnp.exp(mx_t - m_new)
                    num_new = num_s[half, rows, :] * a_old + num_t * a_new
                    den_new = den_s[half, rows, :] * a_old + den_t * a_new
                if last:
                    o_ref[0, half, rows, :] = num_new / den_new
                else:
                    mx_s[half, rows, :] = jnp.broadcast_to(m_new, (qb, LANES))
                    num_s[half, rows, :] = num_new
                    den_s[half, rows, :] = jnp.broadcast_to(den_new, (qb, LANES))
            return carry

        lax.fori_loop(0, tiles // DIL_UNROLL, tile_group, 0)


def _dilated(q, k, v):
    b, _, s, _ = q.shape
    qb = DIL_QBLK
    t_len = DIL_TILE
    a_idx = np.arange(qb)[:, None]
    k_idx = np.arange(2 * qb)[None, :]
    dist = a_idx + qb - k_idx
    band = (dist >= 0) & (dist <= qb)
    bias = np.stack([np.where(band, 0.0, NEG_BIG), np.where(band & (k_idx >= qb), 0.0, NEG_BIG)])
    cur = pl.BlockSpec((1, N_HALVES, t_len, LANES), lambda bi, n: (bi, 0, n, 0))
    prev = pl.BlockSpec((1, N_HALVES, t_len, LANES), lambda bi, n: (bi, 0, jnp.maximum(n - 1, 0), 0))
    stat = pltpu.VMEM((N_HALVES, t_len, LANES), F32)
    both = pltpu.VMEM((N_HALVES, 2 * t_len, LANES), F32)
    return pl.pallas_call(
        _dil_kernel,
        grid=(b, s // t_len),
        in_specs=[cur, prev, cur, prev, cur, _const_spec((2, qb, 2 * qb))],
        out_specs=cur,
        out_shape=jax.ShapeDtypeStruct((b, N_HALVES, s, LANES), F32),
        scratch_shapes=[both, both, stat, stat, stat],
        compiler_params=_params(("arbitrary", "arbitrary")),
        name="dilated",
    )(q, k, k, v, v, jnp.asarray(bias, F32))


HGRN_CHUNK = 128
HGRN_LEVELS = HGRN_CHUNK.bit_length() - 1
HGRN_TILE = 1024
SUBLANES = 8


LOG2_E = 1.4426950408889634


def _log2_split_decay(g, log_f, half):
    c, w = g.shape
    block = 2 * half
    if block >= SUBLANES:
        g3 = g.reshape(c // block, block, w)
        upper = lax.broadcasted_iota(jnp.int32, (1, block, 1), 1) >= half
        return ((g3 - g3[:, half - 1:half, :]) * jnp.where(upper, LOG2_E, -LOG2_E)).reshape(c, w)
    pos = jnp.bitwise_and(lax.broadcasted_iota(jnp.int32, (c, 1), 0), block - 1)
    if half == 1:
        return jnp.where(pos == 1, log_f * LOG2_E, 0.0)
    assert half == 2
    nxt = pltpu.roll(log_f, c - 1, 0)
    prv = pltpu.roll(log_f, 1, 0)
    dist = jnp.where(pos == 0, nxt, jnp.where(pos == 1, 0.0, jnp.where(pos == 2, log_f, log_f + prv)))
    return dist * LOG2_E


def _hgrn_kernel(q_ref, f_ref, i_ref, g_ref, lbl_ref, ng_ref, tril_ref, lvl_ref, o_ref,
                 state_ref, *, layer):
    @pl.when(pl.program_id(1) == 0)
    def _():
        state_ref[...] = jnp.zeros_like(state_ref)

    c = HGRN_CHUNK
    chunks = range(HGRN_TILE // c)
    heads = range(HGRN_HEADS)
    lanes = [slice(hd * HGRN_D, (hd + 1) * HGRN_D) for hd in heads]
    rows = [slice(ci * c, (ci + 1) * c) for ci in chunks]
    norm_gain = ng_ref[...]
    tril = tril_ref[...]
    logits = lbl_ref[...]

    g_last, q_dec, k_dec, q_lvl, k_lvl = [], [], [], [], []
    for ci in chunks:
        zq = q_ref[0, rows[ci], :]
        log_f, one_m_f = _hgrn_gate_terms(f_ref[0, rows[ci], :], logits, layer)
        lf_hi, lf_lo = _split_bf16(log_f)
        g_c = _dot(tril, lf_hi) + _dot(tril, lf_lo)
        qq_c = zq * _sigmoid(zq)
        g_last.append(g_c[c - 1:c, :])
        q_dec.append((qq_c * jnp.exp(g_c)).astype(BF16))
        k_dec.append((one_m_f * jnp.exp(g_last[ci] - g_c)).astype(BF16))
        ql = [qq_c.astype(BF16)]
        kl = [one_m_f.astype(BF16)]
        for lv in range(HGRN_LEVELS):
            decay = jnp.exp2(_log2_split_decay(g_c, log_f, 1 << lv).astype(BF16))
            ql.append(ql[0] * decay)
            kl.append(kl[0] * decay)
        q_lvl.append(ql)
        k_lvl.append(kl)

    val = [i_ref[0, rows[ci], :] for ci in chunks]
    updates = [[_dot_tn(val[ci][:, lanes[hd]], k_dec[ci][:, lanes[hd]]) for hd in heads] for ci in chunks]
    states = [[state_ref[hd] for hd in heads]]
    for ci in chunks:
        states.append([states[ci][hd] * jnp.exp(g_last[ci][:, lanes[hd]]) + updates[ci][hd]
                       for hd in heads])
    for hd in heads:
        state_ref[hd] = states[-1][hd]
    o_inter = [[_dot_nt(q_dec[ci][:, lanes[hd]], states[ci][hd].astype(BF16)) for hd in heads]
               for ci in chunks]
    scores = [[[_dot_nt(q_lvl[ci][lv][:, lanes[hd]], k_lvl[ci][lv][:, lanes[hd]])
                for lv in range(HGRN_LEVELS + 1)] for hd in heads] for ci in chunks]

    lvl = lvl_ref[...]
    level_masks = [lvl == lv for lv in range(HGRN_LEVELS + 1)]
    for ci in chunks:
        outs = []
        for hd in heads:
            p = jnp.zeros((c, c), F32)
            for lv in range(HGRN_LEVELS + 1):
                p = jnp.where(level_masks[lv], scores[ci][hd][lv], p)
            o_h = _dot(p.astype(BF16), val[ci][:, lanes[hd]]) + o_inter[ci][hd]
            outs.append(o_h * _rms_scale(o_h) * norm_gain)
        zg = g_ref[0, rows[ci], :]
        o_ref[0, rows[ci], :] = jnp.concatenate(outs, axis=-1) * (zg * _sigmoid(zg))


def _hgrn_tables():
    c = HGRN_CHUNK
    idx = np.arange(c)
    tril = idx[:, None] >= idx[None, :]
    diff = idx[:, None] ^ idx[None, :]
    lvl = np.where(idx[:, None] > idx[None, :], np.floor(np.log2(np.maximum(diff, 1))) + 1, -1)
    lvl = np.where(idx[:, None] == idx[None, :], 0, lvl).astype(np.int32)
    return jnp.asarray(tril, BF16), jnp.asarray(lvl)


def _hgrn(qh, fh, ih, gh, lb_logits, norm_gain, layer):
    b, s, _ = qh.shape
    t = HGRN_TILE
    c = HGRN_CHUNK
    depth = lb_logits.shape[0]
    tril, lvl = _hgrn_tables()
    tok = pl.BlockSpec((1, t, HGRN_W), lambda bi, i: (bi, i, 0))
    return pl.pallas_call(
        functools.partial(_hgrn_kernel, layer=layer),
        grid=(b, s // t),
        in_specs=[tok, tok, tok, tok,
                  _const_spec((depth, HGRN_W)),
                  _const_spec((1, HGRN_D)),
                  _const_spec((c, c)),
                  _const_spec((c, c))],
        out_specs=tok,
        out_shape=jax.ShapeDtypeStruct((b, s, HGRN_W), F32),
        scratch_shapes=[pltpu.VMEM((HGRN_HEADS, HGRN_D, HGRN_D), F32)],
        compiler_params=_params(("arbitrary", "arbitrary")),
        name="hgrn2",
    )(qh, fh, ih, gh, lb_logits.astype(F32), norm_gain.reshape(1, HGRN_D), tril, lvl)


def _outproj_ffn_kernel(x_ref, mod_ref, oa_ref, od_ref, oh_ref, wo_ref, wg_ref, wu_ref, wd_ref, o_ref,
                        h_scr, acc_scr):
    y = _dot(oa_ref[0].astype(BF16), wo_ref[0:SLAB, :])
    for half in range(N_HALVES):
        lo = SLAB + half * LANES
        y += _dot(od_ref[0, half].astype(BF16), wo_ref[lo:lo + LANES, :])
    y += _dot(oh_ref[0].astype(BF16), wo_ref[2 * SLAB:, :])
    gate = mod_ref[0, 5:6, :]
    o_ref[0] = x_ref[0] + gate * y
    _ffn_half_step(o_ref, mod_ref, 6, wg_ref, wu_ref, wd_ref, h_scr, acc_scr)


def _outproj_ffn(x, mod_l, o_a, o_d, o_h, w_out, w_gate, w_up, w_down, layer, tm=512):
    b, s, d = x.shape

    def tok(width):
        return pl.BlockSpec((1, tm, width), lambda bi, i: (bi, i, 0))

    return pl.pallas_call(
        _outproj_ffn_kernel,
        grid=(b, s // tm),
        in_specs=[tok(d), pl.BlockSpec((1, N_MOD, d), lambda bi, i: (bi, 0, 0)), tok(SLAB),
                  pl.BlockSpec((1, N_HALVES, tm, LANES), lambda bi, i: (bi, 0, i, 0)),
                  tok(HGRN_W), _layer_spec(w_out.shape, layer),
                  _layer_spec(w_gate.shape, layer), _layer_spec(w_up.shape, layer),
                  _layer_spec(w_down.shape, layer)],
        out_specs=tok(d),
        out_shape=jax.ShapeDtypeStruct((b, s, d), F32),
        scratch_shapes=[pltpu.VMEM((tm, d), BF16), pltpu.VMEM((tm, d), F32)],
        compiler_params=_params(("arbitrary", "arbitrary")),
        name="outproj_ffn",
    )(x, mod_l, o_a, o_d, o_h, w_out, w_gate, w_up, w_down)


def kernel(x, c, w_mod, b_mod, ffn1_w_gate, ffn1_w_up, ffn1_w_down, w_in, w_out, q_norm_g, k_norm_g,
           hgrn_norm_g, hgrn_lb_logits, ffn2_w_gate, ffn2_w_up, ffn2_w_down):
    b, s, d = x.shape
    depth = w_mod.shape[0]
    assert s % DIL_TILE == 0 and s % HGRN_TILE == 0 and s % SB_QBLK == 0
    mod = _modulation(c, w_mod, b_mod).reshape(depth, b, N_MOD, d)
    cos, sin = _rope_tables(s)
    ffn1 = [w.astype(BF16) for w in (ffn1_w_gate, ffn1_w_up, ffn1_w_down)]
    ffn2 = [w.astype(BF16) for w in (ffn2_w_gate, ffn2_w_up, ffn2_w_down)]
    w_in_b = w_in.astype(BF16)
    w_out_b = w_out.astype(BF16)
    for l in range(depth):
        x = _ffn(x, mod[l], *ffn1, layer=l, row0=0)
        qa, ka, va, qd, kd, vd, qh, fh, ih, gh = _inproj(
            x, mod[l], w_in_b, l, q_norm_g[l], k_norm_g[l], cos, sin)
        o_a = _stick_breaking(qa, ka, va)
        o_d = _dilated(qd, kd, vd)
        o_h = _hgrn(qh, fh, ih, gh, hgrn_lb_logits, hgrn_norm_g[l], layer=l)
        x = _outproj_ffn(x, mod[l], o_a, o_d, o_h, w_out_b, *ffn2, layer=l)
    return x
```

```python
import functools

import jax
import jax.numpy as jnp
import numpy as np
from jax import lax
from jax.experimental import pallas as pl
from jax.experimental.pallas import tpu as pltpu

F32 = jnp.float32
BF16 = jnp.bfloat16

HEAD_DIM = 64
N_HEADS = 4
SLAB = N_HEADS * HEAD_DIM
LANES = 128
N_HALVES = SLAB // LANES
HEADS_PER_HALF = LANES // HEAD_DIM
HGRN_HEADS = 4
HGRN_D = 128
HGRN_W = HGRN_HEADS * HGRN_D
N_MOD = 9
EPS = 1e-6
LB_FLOOR = 1e-30
NEG_BIG = -1e30
HALF_STEP = 0.5
ROPE_THETA = 10000.0
DIL_PATTERNS = ((128, 1), (512, 4), (2048, 16))
DIL_QBLK = 128

OFF_QA, OFF_KA, OFF_VA = 0, 256, 512
OFF_QD, OFF_KD, OFF_VD = 768, 1024, 1280
OFF_QH, OFF_FH, OFF_IH, OFF_GH = 1536, 2048, 2560, 3072

VMEM_LIMIT = 56 * 1024 * 1024

SB_DEAD = 104.0


def _params(sem):
    return pltpu.CompilerParams(dimension_semantics=sem, vmem_limit_bytes=VMEM_LIMIT)


def _const_spec(shape):
    nd = len(shape)
    return pl.BlockSpec(shape, lambda *_: (0,) * nd, pipeline_mode=pl.Buffered(1))


def _split_bf16(x):
    hi = x.astype(BF16)
    lo = (x - hi.astype(F32)).astype(BF16)
    return hi, lo


def _dot(a, b):
    return jnp.dot(a, b, preferred_element_type=F32)


def _dot_nt(a, b):
    return lax.dot_general(a, b, (((1,), (1,)), ((), ())), preferred_element_type=F32)


def _dot_tn(a, b):
    return lax.dot_general(a, b, (((0,), (0,)), ((), ())), preferred_element_type=F32)


def _sigmoid(x):
    return 1.0 / (1.0 + jnp.exp(-x))


def _log_sigmoid(x):
    return jnp.minimum(x, 0.0) - jnp.log(1.0 + jnp.exp(-jnp.abs(x)))


def _rms_scale(x):
    return lax.rsqrt(jnp.mean(x * x, axis=-1, keepdims=True) + EPS)


def _head_of_lane(lane):
    return jnp.right_shift(lane, HEAD_DIM.bit_length() - 1)


def _mod_kernel(c_ref, w_ref, b_ref, o_ref):
    c = c_ref[...]
    sc = c * _sigmoid(c)
    sc_hi, sc_lo = _split_bf16(sc)
    w = w_ref[0]
    w_hi, w_lo = _split_bf16(w)
    o_ref[0] = _dot(sc_hi, w_hi) + (_dot(sc_hi, w_lo) + _dot(sc_lo, w_hi)) + b_ref[0]


def _modulation(c, w_mod, b_mod):
    depth, d, n = w_mod.shape
    rows = 16
    nb = c.shape[0]
    assert nb <= rows
    c = jnp.pad(c, ((0, rows - nb), (0, 0)))
    b = rows
    tn = 1152
    out = pl.pallas_call(
        _mod_kernel,
        grid=(depth, n // tn),
        in_specs=[
            pl.BlockSpec((b, d), lambda l, j: (0, 0)),
            pl.BlockSpec((1, d, tn), lambda l, j: (l, 0, j)),
            pl.BlockSpec((1, 1, tn), lambda l, j: (l, 0, j)),
        ],
        out_specs=pl.BlockSpec((1, b, tn), lambda l, j: (l, 0, j)),
        out_shape=jax.ShapeDtypeStruct((depth, b, n), F32),
        compiler_params=_params(("arbitrary", "arbitrary")),
        name="adaln_mod",
    )(c, w_mod, b_mod.reshape(depth, 1, n))
    return out[:, :nb]


FFN_CHUNK = 256


def _ffn_half_step(xo_ref, mod_ref, row0, wg_ref, wu_ref, wd_ref, h_scr, acc_scr):
    shift = mod_ref[0, row0:row0 + 1, :]
    scale = mod_ref[0, row0 + 1:row0 + 2, :]
    gate = mod_ref[0, row0 + 2:row0 + 3, :]
    x = xo_ref[0]
    h_scr[...] = (x * _rms_scale(x) * (1.0 + scale) + shift).astype(BF16)
    for c in range(wg_ref.shape[1] // FFN_CHUNK):
        cols = slice(c * FFN_CHUNK, (c + 1) * FFN_CHUNK)
        g = _dot(h_scr[...], wg_ref[:, cols])
        u = _dot(h_scr[...], wu_ref[:, cols])
        a = (g * _sigmoid(g) * u).astype(BF16)
        y = _dot(a, wd_ref[cols, :])
        if c == 0:
            acc_scr[...] = y
        else:
            acc_scr[...] += y
    xo_ref[0] = xo_ref[0] + (HALF_STEP * gate) * acc_scr[...]


def _ffn_kernel(x_ref, mod_ref, wg_ref, wu_ref, wd_ref, o_ref, h_scr, acc_scr, *, row0):
    o_ref[0] = x_ref[0]
    _ffn_half_step(o_ref, mod_ref, row0, wg_ref, wu_ref, wd_ref, h_scr, acc_scr)


def _layer_spec(shape, layer):
    nd = len(shape)
    return pl.BlockSpec((None,) + tuple(shape[1:]), lambda *_: (layer,) + (0,) * (nd - 1),
                        pipeline_mode=pl.Buffered(1))


def _ffn(x, mod_l, w_gate, w_up, w_down, layer, row0, tm=1024):
    b, s, d = x.shape
    return pl.pallas_call(
        functools.partial(_ffn_kernel, row0=row0),
        grid=(b, s // tm),
        in_specs=[
            pl.BlockSpec((1, tm, d), lambda bi, i: (bi, i, 0)),
            pl.BlockSpec((1, N_MOD, d), lambda bi, i: (bi, 0, 0)),
            _layer_spec(w_gate.shape, layer),
            _layer_spec(w_up.shape, layer),
            _layer_spec(w_down.shape, layer),
        ],
        out_specs=pl.BlockSpec((1, tm, d), lambda bi, i: (bi, i, 0)),
        out_shape=jax.ShapeDtypeStruct((b, s, d), F32),
        scratch_shapes=[pltpu.VMEM((tm, d), BF16), pltpu.VMEM((tm, d), F32)],
        compiler_params=_params(("arbitrary", "arbitrary")),
        name="ffn",
    )(x, mod_l, w_gate, w_up, w_down)


def _swap_halves(x):
    n = x.shape[-1]
    lane = lax.broadcasted_iota(jnp.int32, x.shape, x.ndim - 1)
    up = pltpu.roll(x, n - HEAD_DIM // 2, x.ndim - 1)
    down = pltpu.roll(x, HEAD_DIM // 2, x.ndim - 1)
    return jnp.where(jnp.bitwise_and(lane, HEAD_DIM - 1) < HEAD_DIM // 2, up, down)


def _qk_norm_rope(x, gain, cos, sin_signed, head_mean):
    sq_hi, sq_lo = _split_bf16(x * x)
    ms = _dot(sq_hi, head_mean) + _dot(sq_lo, head_mean)
    xn = x * lax.rsqrt(ms + EPS) * gain
    return xn * cos + _swap_halves(xn) * sin_signed


def _hgrn_gate_terms(z, logits, layer):
    ex = jnp.exp(logits - jnp.max(logits, axis=0, keepdims=True))
    sm = ex / jnp.sum(ex, axis=0, keepdims=True)
    lb = jnp.zeros((1, HGRN_W), F32)
    for l in range(1, layer + 1):
        lb = lb + sm[l:l + 1, :]
    lb = jnp.clip(lb, 0.0, 1.0 - EPS)
    lb_floor = jnp.maximum(lb, LB_FLOOR)
    log_lb = jnp.log(lb_floor)
    one_m_lb = 1.0 - lb
    e = jnp.exp(-jnp.abs(z))
    one_p_e = 1.0 + e
    bterm = jnp.log(one_m_lb) + (jnp.minimum(z, 0.0) - jnp.log(one_p_e))
    log_f = jnp.maximum(log_lb, bterm) + jnp.log(1.0 + jnp.exp(-jnp.abs(log_lb - bterm)))
    sigmoid_neg = jnp.where(z > 0.0, e, 1.0) / one_p_e
    one_m_f = one_m_lb * sigmoid_neg - (lb_floor - lb)
    return log_f, one_m_f


def _inproj_kernel(x_ref, mod_ref, w_ref, cos_ref, sin_ref, qg_ref, kg_ref, hm_ref,
                   qa_ref, ka_ref, va_ref, qd_ref, kd_ref, vd_ref, qh_ref, fh_ref, ih_ref, gh_ref):
    x = x_ref[0]
    shift = mod_ref[0, 3:4, :]
    scale = mod_ref[0, 4:5, :]
    h = (x * _rms_scale(x) * (1.0 + scale) + shift).astype(BF16)

    def proj(off, width):
        return _dot(h, w_ref[:, off:off + width])

    def put_halves(ref, val):
        for half in range(N_HALVES):
            ref[0, half] = val[:, half * LANES:(half + 1) * LANES]

    qd_raw = proj(OFF_QD, SLAB)
    kd_raw = proj(OFF_KD, SLAB)
    qh_ref[0] = proj(OFF_QH, HGRN_W)
    cos = cos_ref[...]
    sin = sin_ref[...]
    hm = hm_ref[...]
    qd = _qk_norm_rope(qd_raw, qg_ref[...], cos, sin, hm)
    put_halves(qd_ref, qd * (HEAD_DIM ** -0.5))
    put_halves(kd_ref, _qk_norm_rope(kd_raw, kg_ref[...], cos, sin, hm))
    fh_ref[0] = proj(OFF_FH, HGRN_W)
    gh_ref[0] = proj(OFF_GH, HGRN_W)
    ih_ref[0] = proj(OFF_IH, HGRN_W).astype(BF16)
    qa_ref[0] = proj(OFF_QA, SLAB)
    ka_ref[0] = proj(OFF_KA, SLAB).astype(BF16)
    va_ref[0] = proj(OFF_VA, SLAB).astype(BF16)
    put_halves(vd_ref, proj(OFF_VD, SLAB))


def _rope_tables(s):
    half = HEAD_DIM // 2
    inv_freq = ROPE_THETA ** (-jnp.arange(half, dtype=F32) * 2.0 / HEAD_DIM)
    ang = jnp.arange(s, dtype=F32)[:, None] * inv_freq[None, :]
    cos, sin = jnp.cos(ang), jnp.sin(ang)
    cos_full = jnp.tile(jnp.concatenate([cos, cos], axis=-1), (1, N_HEADS))
    sin_signed = jnp.tile(jnp.concatenate([-sin, sin], axis=-1), (1, N_HEADS))
    return cos_full, sin_signed


def _inproj(x, mod_l, w_in, layer, q_gain, k_gain, cos, sin, tm=1024):
    b, s, d = x.shape
    qg = jnp.tile(q_gain.reshape(1, HEAD_DIM), (1, N_HEADS))
    kg = jnp.tile(k_gain.reshape(1, HEAD_DIM), (1, N_HEADS))
    head_id = np.arange(SLAB) // HEAD_DIM
    head_mean = jnp.asarray(np.where(head_id[:, None] == head_id[None, :], 1.0 / HEAD_DIM, 0.0), BF16)

    def tok(width):
        return pl.BlockSpec((1, tm, width), lambda bi, i: (bi, i, 0))

    def out(width, dt):
        return jax.ShapeDtypeStruct((b, s, width), dt)

    halves = pl.BlockSpec((1, N_HALVES, tm, LANES), lambda bi, i: (bi, 0, i, 0))
    halves_out = jax.ShapeDtypeStruct((b, N_HALVES, s, LANES), F32)

    return pl.pallas_call(
        _inproj_kernel,
        grid=(b, s // tm),
        in_specs=[
            tok(d),
            pl.BlockSpec((1, N_MOD, d), lambda bi, i: (bi, 0, 0)),
            _layer_spec(w_in.shape, layer),
            pl.BlockSpec((tm, SLAB), lambda bi, i: (i, 0)),
            pl.BlockSpec((tm, SLAB), lambda bi, i: (i, 0)),
            _const_spec((1, SLAB)),
            _const_spec((1, SLAB)),
            _const_spec((SLAB, SLAB)),
        ],
        out_specs=[tok(SLAB)] * 3 + [halves] * 3 + [tok(HGRN_W)] * 4,
        out_shape=[out(SLAB, F32), out(SLAB, BF16), out(SLAB, BF16),
                   halves_out, halves_out, halves_out,
                   out(HGRN_W, F32), out(HGRN_W, F32), out(HGRN_W, BF16), out(HGRN_W, F32)],
        compiler_params=_params(("arbitrary", "arbitrary")),
        name="mixer_inproj",
    )(x, mod_l, w_in, cos, sin, qg, kg, head_mean)


SB_QBLK = 256
SB_KBLK = 128


def _sb_kernel(q_ref, k_ref, v_ref, uu_ref, o_ref, acc_ref, run_ref, qm_ref, z_ref):
    i = pl.program_id(1)
    qblk, kblk = SB_QBLK, SB_KBLK
    heads = range(N_HEADS)
    lane = lax.broadcasted_iota(jnp.int32, (1, SLAB), 1)
    head_masks = [_head_of_lane(lane) == hd for hd in heads]
    qm_ref[...] = (q_ref[0] * (HEAD_DIM ** -0.5)).astype(BF16)
    acc_ref[...] = jnp.zeros_like(acc_ref)
    run_ref[...] = jnp.zeros_like(run_ref)

    def per_head_rows(x):
        return jnp.concatenate([jnp.where(head_masks[hd], x, jnp.zeros_like(x)) for hd in heads], axis=0)

    def scores(j):
        off = pl.multiple_of(jnp.maximum(j, 0) * kblk, kblk)
        z_all = _dot_nt(qm_ref[...], per_head_rows(k_ref[0, pl.ds(off, kblk), :]))
        return [z_all[:, hd * kblk:(hd + 1) * kblk] for hd in heads]

    def key_block(j, z, causal_below, first_row=0):
        off = pl.multiple_of(j * kblk, kblk)
        v = v_ref[0, pl.ds(off, kblk), :]
        uu = uu_ref[...]
        rows = slice(first_row, qblk)
        causal = None
        if causal_below is not None:
            shape = (qblk - first_row, kblk)
            col_minus_row = (lax.broadcasted_iota(jnp.int32, shape, 1)
                             - lax.broadcasted_iota(jnp.int32, shape, 0))
            causal = col_minus_row < causal_below + first_row
        zr = [z[hd][rows] for hd in heads]
        ls = [_log_sigmoid(zr[hd]) for hd in heads]
        lnb = [ls[hd] - zr[hd] for hd in heads]
        if causal is not None:
            lnb = [jnp.where(causal, x, 0.0) for x in lnb]
        tail = []
        for hd in heads:
            lnb_hi, lnb_lo = _split_bf16(lnb[hd])
            tail.append(_dot(jnp.concatenate([lnb_hi, lnb_lo], axis=1), uu))
        run = [run_ref[hd, rows, :] for hd in heads]
        w = [jnp.exp(ls[hd] + tail[hd] + run[hd]) for hd in heads]
        if causal is not None:
            w = [jnp.where(causal, x, 0.0) for x in w]
        pv = _dot(jnp.concatenate([w[hd].astype(BF16) for hd in heads], axis=1), per_head_rows(v))
        run_max = None
        for hd in heads:
            new_run = run[hd] + (tail[hd][:, 0:1] + lnb[hd][:, 0:1])
            run_ref[hd, rows, :] = new_run
            run_max = new_run if run_max is None else jnp.maximum(run_max, new_run)
        acc_ref[rows, :] += pv
        return jnp.max(run_max) > -SB_DEAD

    per_q = qblk // kblk
    top = i * per_q + per_q - 1
    z_now = scores(top)
    alive = None
    for d in range(per_q):
        z_next = scores(top - d - 1)
        if d == per_q - 1:
            for hd in heads:
                z_ref[hd] = z_next[hd]
        alive = key_block(top - d, z_now, (d + 1 - per_q) * kblk, first_row=(per_q - 1 - d) * kblk)
        z_now = z_next

    def cond(carry):
        j, alive = carry
        return jnp.logical_and(j >= 0, alive)

    def body(carry):
        j, _ = carry
        z = [z_ref[hd] for hd in heads]
        z_ahead = scores(j - 1)
        alive = key_block(j, z, None)
        for hd in heads:
            z_ref[hd] = z_ahead[hd]
        return j - 1, alive

    lax.while_loop(cond, body, (top - per_q, alive))
    o_ref[0] = acc_ref[...]


def _stick_breaking(q, k, v):
    b, s, _ = q.shape
    qblk, kblk = SB_QBLK, SB_KBLK
    idx = np.arange(kblk)
    u = (idx[:, None] > idx[None, :])
    uu = jnp.asarray(np.concatenate([u, u], axis=0), BF16)
    return pl.pallas_call(
        _sb_kernel,
        grid=(b, s // qblk),
        in_specs=[
            pl.BlockSpec((1, qblk, SLAB), lambda bi, i: (bi, i, 0)),
            pl.BlockSpec((1, s, SLAB), lambda bi, i: (bi, 0, 0), pipeline_mode=pl.Buffered(1)),
            pl.BlockSpec((1, s, SLAB), lambda bi, i: (bi, 0, 0), pipeline_mode=pl.Buffered(1)),
            _const_spec((2 * kblk, kblk)),
        ],
        out_specs=pl.BlockSpec((1, qblk, SLAB), lambda bi, i: (bi, i, 0)),
        out_shape=jax.ShapeDtypeStruct((b, s, SLAB), F32),
        scratch_shapes=[pltpu.VMEM((qblk, SLAB), F32),
                        pltpu.VMEM((N_HEADS, qblk, kblk), F32),
                        pltpu.VMEM((qblk, SLAB), BF16),
                        pltpu.VMEM((N_HEADS, qblk, kblk), F32)],
        compiler_params=_params(("arbitrary", "arbitrary")),
        name="stick_breaking",
    )(q, k, v, uu)


DIL_TILE = DIL_QBLK * max(r for _, r in DIL_PATTERNS)
DIL_UNROLL = 2
DIL_FOLD = 4


def _dil_kernel(q_ref, kp_ref, kc_ref, vp_ref, vc_ref, bias_ref, o_ref,
                k_all, v_all, q_fold, k_fold, v_fold, num_s, den_s, mx_s):
    n = pl.program_id(1)
    t_len = DIL_TILE
    qb = DIL_QBLK
    k_all[:, 0:t_len, :] = kp_ref[0]
    k_all[:, t_len:2 * t_len, :] = kc_ref[0]
    v_all[:, 0:t_len, :] = vp_ref[0]
    v_all[:, t_len:2 * t_len, :] = vc_ref[0]
    for half in range(N_HALVES):
        for f in range(DIL_FOLD):
            q_fold[half, f * (t_len // DIL_FOLD):(f + 1) * (t_len // DIL_FOLD), :] = (
                q_ref[0, half, pl.ds(f, t_len // DIL_FOLD, stride=DIL_FOLD), :])
            span = 2 * t_len // DIL_FOLD
            k_fold[half, f * span:(f + 1) * span, :] = k_all[half, pl.ds(f, span, stride=DIL_FOLD), :]
            v_fold[half, f * span:(f + 1) * span, :] = v_all[half, pl.ds(f, span, stride=DIL_FOLD), :]
    lane = lax.broadcasted_iota(jnp.int32, (1, LANES), 1)
    head_masks = [_head_of_lane(lane) == hh for hh in range(HEADS_PER_HALF)]

    for pi, (window, r) in enumerate(reversed(DIL_PATTERNS)):
        assert window // r == qb
        opening = pi == 0
        last = pi == len(DIL_PATTERNS) - 1
        shift = r.bit_length() - 1
        tiles = t_len // qb

        def rows_of(start, size, r=r):
            return pl.ds(start, size) if r == 1 else pl.ds(start, size, stride=r)

        def tile_group(it, carry, r=r, shift=shift, opening=opening, last=last, rows_of=rows_of):
            folded = r > DIL_FOLD
            q_rows, k_rows, q_src, k_src, bias = [], [], [], [], []
            for u in range(DIL_UNROLL):
                idx = it * DIL_UNROLL + u
                t = jnp.right_shift(idx, shift)
                j = jnp.bitwise_and(idx, r - 1)
                q0 = t * (qb * r) + j
                has_prev = jnp.logical_or(n > 0, t > 0)
                bias.append(bias_ref[jnp.where(has_prev, 0, 1)])
                q_rows.append(rows_of(q0, qb))
                k_rows.append(rows_of(t_len + q0 - qb * r, 2 * qb))
                if folded:
                    assert r * qb == t_len
                    f = jnp.bitwise_and(j, DIL_FOLD - 1)
                    within = jnp.right_shift(j, DIL_FOLD.bit_length() - 1)
                    q_src.append(pl.ds(f * (t_len // DIL_FOLD) + within, qb, stride=r // DIL_FOLD))
                    k_src.append(pl.ds(f * (2 * t_len // DIL_FOLD) + within, 2 * qb, stride=r // DIL_FOLD))
            slabs = [(u, half) for u in range(DIL_UNROLL) for half in range(N_HALVES)]
            units = [(si, hh) for si in range(len(slabs)) for hh in range(HEADS_PER_HALF)]
            if folded:
                qs = [q_fold[half, q_src[u], :] for u, half in slabs]
                kb = [k_fold[half, k_src[u], :].astype(BF16) for u, half in slabs]
                vb = [v_fold[half, k_src[u], :].astype(BF16) for u, half in slabs]
            else:
                qs = [q_ref[0, half, q_rows[u], :] for u, half in slabs]
                kb = [k_all[half, k_rows[u], :].astype(BF16) for u, half in slabs]
                vb = [v_all[half, k_rows[u], :].astype(BF16) for u, half in slabs]
            sc = [_dot_nt(jnp.where(head_masks[hh], qs[si], 0.0).astype(BF16), kb[si]) + bias[slabs[si][0]]
                  for si, hh in units]
            mx = [jnp.max(x, axis=-1, keepdims=True) for x in sc]
            p = [jnp.exp(x - m) for x, m in zip(sc, mx)]
            den = [jnp.sum(x, axis=-1, keepdims=True) for x in p]
            pv = [_dot(x.astype(BF16), vb[si]) for x, (si, _) in zip(p, units)]
            for si, (u, half) in enumerate(slabs):
                first = si * HEADS_PER_HALF
                num_t, den_t, mx_t = pv[first], den[first], mx[first]
                for hh in range(1, HEADS_PER_HALF):
                    hm = head_masks[hh]
                    num_t = jnp.where(hm, pv[first + hh], num_t)
                    den_t = jnp.where(hm, den[first + hh], den_t)
                    mx_t = jnp.where(hm, mx[first + hh], mx_t)
                rows = q_rows[u]
                if opening:
                    m_new, num_new, den_new = mx_t, num_t, den_t
                else:
                    m_old = mx_s[half, rows, :]
                    m_new = jnp.maximum(m_old, mx_t)
                    a_old = jnp.exp(m_old - m_new)
                    a_new = jnp.exp(mx_t - m_new)
                    num_new = num_s[half, rows, :] * a_old + num_t * a_new
                    den_new = den_s[half, rows, :] * a_old + den_t * a_new
                if last:
                    o_ref[0, half, rows, :] = num_new / den_new
                else:
                    mx_s[half, rows, :] = jnp.broadcast_to(m_new, (qb, LANES))
                    num_s[half, rows, :] = num_new
                    den_s[half, rows, :] = jnp.broadcast_to(den_new, (qb, LANES))
            return carry

        lax.fori_loop(0, tiles // DIL_UNROLL, tile_group, 0)


def _dilated(q, k, v):
    b, _, s, _ = q.shape
    qb = DIL_QBLK
    t_len = DIL_TILE
    a_idx = np.arange(qb)[:, None]
    k_idx = np.arange(2 * qb)[None, :]
    dist = a_idx + qb - k_idx
    band = (dist >= 0) & (dist <= qb)
    bias = np.stack([np.where(band, 0.0, NEG_BIG), np.where(band & (k_idx >= qb), 0.0, NEG_BIG)])
    cur = pl.BlockSpec((1, N_HALVES, t_len, LANES), lambda bi, n: (bi, 0, n, 0))
    prev = pl.BlockSpec((1, N_HALVES, t_len, LANES), lambda bi, n: (bi, 0, jnp.maximum(n - 1, 0), 0))
    stat = pltpu.VMEM((N_HALVES, t_len, LANES), F32)
    both = pltpu.VMEM((N_HALVES, 2 * t_len, LANES), F32)
    return pl.pallas_call(
        _dil_kernel,
        grid=(b, s // t_len),
        in_specs=[cur, prev, cur, prev, cur, _const_spec((2, qb, 2 * qb))],
        out_specs=cur,
        out_shape=jax.ShapeDtypeStruct((b, N_HALVES, s, LANES), F32),
        scratch_shapes=[both, both, stat, both, both, stat, stat, stat],
        compiler_params=_params(("arbitrary", "arbitrary")),
        name="dilated",
    )(q, k, k, v, v, jnp.asarray(bias, F32))


HGRN_CHUNK = 128
HGRN_LEVELS = HGRN_CHUNK.bit_length() - 1
HGRN_TILE = 1024
SUBLANES = 8


LOG2_E = 1.4426950408889634


def _log2_split_decay(g, log_f, half):
    c, w = g.shape
    block = 2 * half
    if block >= SUBLANES:
        g3 = g.reshape(c // block, block, w)
        upper = lax.broadcasted_iota(jnp.int32, (1, block, 1), 1) >= half
        return ((g3 - g3[:, half - 1:half, :]) * jnp.where(upper, LOG2_E, -LOG2_E)).reshape(c, w)
    pos = jnp.bitwise_and(lax.broadcasted_iota(jnp.int32, (c, 1), 0), block - 1)
    if half == 1:
        return jnp.where(pos == 1, log_f * LOG2_E, 0.0)
    assert half == 2
    nxt = pltpu.roll(log_f, c - 1, 0)
    prv = pltpu.roll(log_f, 1, 0)
    dist = jnp.where(pos == 0, nxt, jnp.where(pos == 1, 0.0, jnp.where(pos == 2, log_f, log_f + prv)))
    return dist * LOG2_E


def _hgrn_kernel(q_ref, f_ref, i_ref, g_ref, lbl_ref, ng_ref, tril_ref, lvl_ref, o_ref,
                 state_ref, *, layer):
    @pl.when(pl.program_id(1) == 0)
    def _():
        state_ref[...] = jnp.zeros_like(state_ref)

    c = HGRN_CHUNK
    chunks = range(HGRN_TILE // c)
    heads = range(HGRN_HEADS)
    lanes = [slice(hd * HGRN_D, (hd + 1) * HGRN_D) for hd in heads]
    rows = [slice(ci * c, (ci + 1) * c) for ci in chunks]
    norm_gain = ng_ref[...]
    tril = tril_ref[...]
    logits = lbl_ref[...]

    g_last, q_dec, k_dec, q_lvl, k_lvl = [], [], [], [], []
    for ci in chunks:
        zq = q_ref[0, rows[ci], :]
        log_f, one_m_f = _hgrn_gate_terms(f_ref[0, rows[ci], :], logits, layer)
        lf_hi, lf_lo = _split_bf16(log_f)
        g_c = _dot(tril, lf_hi) + _dot(tril, lf_lo)
        qq_c = zq * _sigmoid(zq)
        g_last.append(g_c[c - 1:c, :])
        q_dec.append((qq_c * jnp.exp(g_c)).astype(BF16))
        k_dec.append((one_m_f * jnp.exp(g_last[ci] - g_c)).astype(BF16))
        ql = [qq_c.astype(BF16)]
        kl = [one_m_f.astype(BF16)]
        for lv in range(HGRN_LEVELS):
            decay = jnp.exp2(_log2_split_decay(g_c, log_f, 1 << lv).astype(BF16))
            ql.append(ql[0] * decay)
            kl.append(kl[0] * decay)
        q_lvl.append(ql)
        k_lvl.append(kl)

    val = [i_ref[0, rows[ci], :] for ci in chunks]
    updates = [[_dot_tn(val[ci][:, lanes[hd]], k_dec[ci][:, lanes[hd]]) for hd in heads] for ci in chunks]
    states = [[state_ref[hd] for hd in heads]]
    for ci in chunks:
        states.append([states[ci][hd] * jnp.exp(g_last[ci][:, lanes[hd]]) + updates[ci][hd]
                       for hd in heads])
    for hd in heads:
        state_ref[hd] = states[-1][hd]
    o_inter = [[_dot_nt(q_dec[ci][:, lanes[hd]], states[ci][hd].astype(BF16)) for hd in heads]
               for ci in chunks]
    scores = [[[_dot_nt(q_lvl[ci][lv][:, lanes[hd]], k_lvl[ci][lv][:, lanes[hd]])
                for lv in range(HGRN_LEVELS + 1)] for hd in heads] for ci in chunks]

    lvl = lvl_ref[...]
    level_masks = [lvl == lv for lv in range(HGRN_LEVELS + 1)]
    for ci in chunks:
        outs = []
        for hd in heads:
            p = jnp.zeros((c, c), F32)
            for lv in range(HGRN_LEVELS + 1):
                p = jnp.where(level_masks[lv], scores[ci][hd][lv], p)
            o_h = _dot(p.astype(BF16), val[ci][:, lanes[hd]]) + o_inter[ci][hd]
            outs.append(o_h * _rms_scale(o_h) * norm_gain)
        zg = g_ref[0, rows[ci], :]
        o_ref[0, rows[ci], :] = jnp.concatenate(outs, axis=-1) * (zg * _sigmoid(zg))


def _hgrn_tables():
    c = HGRN_CHUNK
    idx = np.arange(c)
    tril = idx[:, None] >= idx[None, :]
    diff = idx[:, None] ^ idx[None, :]
    lvl = np.where(idx[:, None] > idx[None, :], np.floor(np.log2(np.maximum(diff, 1))) + 1, -1)
    lvl = np.where(idx[:, None] == idx[None, :], 0, lvl).astype(np.int32)
    return jnp.asarray(tril, BF16), jnp.asarray(lvl)


def _hgrn(qh, fh, ih, gh, lb_logits, norm_gain, layer):
    b, s, _ = qh.shape
    t = HGRN_TILE
    c = HGRN_CHUNK
    depth = lb_logits.shape[0]
    tril, lvl = _hgrn_tables()
    tok = pl.BlockSpec((1, t, HGRN_W), lambda bi, i: (bi, i, 0))
    return pl.pallas_call(
        functools.partial(_hgrn_kernel, layer=layer),
        grid=(b, s // t),
        in_specs=[tok, tok, tok, tok,
                  _const_spec((depth, HGRN_W)),
                  _const_spec((1, HGRN_D)),
                  _const_spec((c, c)),
                  _const_spec((c, c))],
        out_specs=tok,
        out_shape=jax.ShapeDtypeStruct((b, s, HGRN_W), F32),
        scratch_shapes=[pltpu.VMEM((HGRN_HEADS, HGRN_D, HGRN_D), F32)],
        compiler_params=_params(("arbitrary", "arbitrary")),
        name="hgrn2",
    )(qh, fh, ih, gh, lb_logits.astype(F32), norm_gain.reshape(1, HGRN_D), tril, lvl)


def _outproj_ffn_kernel(x_ref, mod_ref, oa_ref, od_ref, oh_ref, wo_ref, wg_ref, wu_ref, wd_ref, o_ref,
                        h_scr, acc_scr):
    y = _dot(oa_ref[0].astype(BF16), wo_ref[0:SLAB, :])
    for half in range(N_HALVES):
        lo = SLAB + half * LANES
        y += _dot(od_ref[0, half].astype(BF16), wo_ref[lo:lo + LANES, :])
    y += _dot(oh_ref[0].astype(BF16), wo_ref[2 * SLAB:, :])
    gate = mod_ref[0, 5:6, :]
    o_ref[0] = x_ref[0] + gate * y
    _ffn_half_step(o_ref, mod_ref, 6, wg_ref, wu_ref, wd_ref, h_scr, acc_scr)


def _outproj_ffn(x, mod_l, o_a, o_d, o_h, w_out, w_gate, w_up, w_down, layer, tm=512):
    b, s, d = x.shape

    def tok(width):
        return pl.BlockSpec((1, tm, width), lambda bi, i: (bi, i, 0))

    return pl.pallas_call(
        _outproj_ffn_kernel,
        grid=(b, s // tm),
        in_specs=[tok(d), pl.BlockSpec((1, N_MOD, d), lambda bi, i: (bi, 0, 0)), tok(SLAB),
                  pl.BlockSpec((1, N_HALVES, tm, LANES), lambda bi, i: (bi, 0, i, 0)),
                  tok(HGRN_W), _layer_spec(w_out.shape, layer),
                  _layer_spec(w_gate.shape, layer), _layer_spec(w_up.shape, layer),
                  _layer_spec(w_down.shape, layer)],
        out_specs=tok(d),
        out_shape=jax.ShapeDtypeStruct((b, s, d), F32),
        scratch_shapes=[pltpu.VMEM((tm, d), BF16), pltpu.VMEM((tm, d), F32)],
        compiler_params=_params(("arbitrary", "arbitrary")),
        name="outproj_ffn",
    )(x, mod_l, o_a, o_d, o_h, w_out, w_gate, w_up, w_down)


def kernel(x, c, w_mod, b_mod, ffn1_w_gate, ffn1_w_up, ffn1_w_down, w_in, w_out, q_norm_g, k_norm_g,
           hgrn_norm_g, hgrn_lb_logits, ffn2_w_gate, ffn2_w_up, ffn2_w_down):
    b, s, d = x.shape
    depth = w_mod.shape[0]
    assert s % DIL_TILE == 0 and s % HGRN_TILE == 0 and s % SB_QBLK == 0
    mod = _modulation(c, w_mod, b_mod).reshape(depth, b, N_MOD, d)
    cos, sin = _rope_tables(s)
    ffn1 = [w.astype(BF16) for w in (ffn1_w_gate, ffn1_w_up, ffn1_w_down)]
    ffn2 = [w.astype(BF16) for w in (ffn2_w_gate, ffn2_w_up, ffn2_w_down)]
    w_in_b = w_in.astype(BF16)
    w_out_b = w_out.astype(BF16)
    for l in range(depth):
        x = _ffn(x, mod[l], *ffn1, layer=l, row0=0)
        qa, ka, va, qd, kd, vd, qh, fh, ih, gh = _inproj(
            x, mod[l], w_in_b, l, q_norm_g[l], k_norm_g[l], cos, sin)
        o_a = _stick_breaking(qa, ka, va)
        o_d = _dilated(qd, kd, vd)
        o_h = _hgrn(qh, fh, ih, gh, hgrn_lb_logits, hgrn_norm_g[l], layer=l)
        x = _outproj_ffn(x, mod[l], o_a, o_d, o_h, w_out_b, *ffn2, layer=l)
    return x
```

```python
import functools

import jax
import jax.numpy as jnp
import numpy as np
from jax import lax
from jax.experimental import pallas as pl
from jax.experimental.pallas import tpu as pltpu

F32 = jnp.float32
BF16 = jnp.bfloat16

HEAD_DIM = 64
N_HEADS = 4
SLAB = N_HEADS * HEAD_DIM
LANES = 128
N_HALVES = SLAB // LANES
HEADS_PER_HALF = LANES // HEAD_DIM
HGRN_HEADS = 4
HGRN_D = 128
HGRN_W = HGRN_HEADS * HGRN_D
N_MOD = 9
EPS = 1e-6
LB_FLOOR = 1e-30
NEG_BIG = -1e30
HALF_STEP = 0.5
ROPE_THETA = 10000.0
DIL_PATTERNS = ((128, 1), (512, 4), (2048, 16))
DIL_QBLK = 128

OFF_QA, OFF_KA, OFF_VA = 0, 256, 512
OFF_QD, OFF_KD, OFF_VD = 768, 1024, 1280
OFF_QH, OFF_FH, OFF_IH, OFF_GH = 1536, 2048, 2560, 3072

VMEM_LIMIT = 56 * 1024 * 1024

SB_DEAD = 104.0


def _params(sem):
    return pltpu.CompilerParams(dimension_semantics=sem, vmem_limit_bytes=VMEM_LIMIT)


def _const_spec(shape):
    nd = len(shape)
    return pl.BlockSpec(shape, lambda *_: (0,) * nd, pipeline_mode=pl.Buffered(1))


def _split_bf16(x):
    hi = x.astype(BF16)
    lo = (x - hi.astype(F32)).astype(BF16)
    return hi, lo


def _dot(a, b):
    return jnp.dot(a, b, preferred_element_type=F32)


def _dot_nt(a, b):
    return lax.dot_general(a, b, (((1,), (1,)), ((), ())), preferred_element_type=F32)


def _dot_tn(a, b):
    return lax.dot_general(a, b, (((0,), (0,)), ((), ())), preferred_element_type=F32)


def _sigmoid(x):
    return 1.0 / (1.0 + jnp.exp(-x))


def _log_sigmoid(x):
    return jnp.minimum(x, 0.0) - jnp.log(1.0 + jnp.exp(-jnp.abs(x)))


def _rms_scale(x):
    return lax.rsqrt(jnp.mean(x * x, axis=-1, keepdims=True) + EPS)


def _head_of_lane(lane):
    return jnp.right_shift(lane, HEAD_DIM.bit_length() - 1)


def _mod_kernel(c_ref, w_ref, b_ref, o_ref):
    c = c_ref[...]
    sc = c * _sigmoid(c)
    sc_hi, sc_lo = _split_bf16(sc)
    w = w_ref[0]
    w_hi, w_lo = _split_bf16(w)
    o_ref[0] = _dot(sc_hi, w_hi) + (_dot(sc_hi, w_lo) + _dot(sc_lo, w_hi)) + b_ref[0]


def _modulation(c, w_mod, b_mod):
    depth, d, n = w_mod.shape
    rows = 16
    nb = c.shape[0]
    assert nb <= rows
    c = jnp.pad(c, ((0, rows - nb), (0, 0)))
    b = rows
    tn = 1152
    out = pl.pallas_call(
        _mod_kernel,
        grid=(depth, n // tn),
        in_specs=[
            pl.BlockSpec((b, d), lambda l, j: (0, 0)),
            pl.BlockSpec((1, d, tn), lambda l, j: (l, 0, j)),
            pl.BlockSpec((1, 1, tn), lambda l, j: (l, 0, j)),
        ],
        out_specs=pl.BlockSpec((1, b, tn), lambda l, j: (l, 0, j)),
        out_shape=jax.ShapeDtypeStruct((depth, b, n), F32),
        compiler_params=_params(("arbitrary", "arbitrary")),
        name="adaln_mod",
    )(c, w_mod, b_mod.reshape(depth, 1, n))
    return out[:, :nb]


FFN_CHUNK = 256


def _ffn_half_step(xo_ref, mod_ref, row0, wg_ref, wu_ref, wd_ref, h_scr, acc_scr):
    shift = mod_ref[0, row0:row0 + 1, :]
    scale = mod_ref[0, row0 + 1:row0 + 2, :]
    gate = mod_ref[0, row0 + 2:row0 + 3, :]
    x = xo_ref[0]
    h_scr[...] = (x * _rms_scale(x) * (1.0 + scale) + shift).astype(BF16)
    for c in range(wg_ref.shape[1] // FFN_CHUNK):
        cols = slice(c * FFN_CHUNK, (c + 1) * FFN_CHUNK)
        g = _dot(h_scr[...], wg_ref[:, cols])
        u = _dot(h_scr[...], wu_ref[:, cols])
        a = (g * _sigmoid(g) * u).astype(BF16)
        y = _dot(a, wd_ref[cols, :])
        if c == 0:
            acc_scr[...] = y
        else:
            acc_scr[...] += y
    xo_ref[0] = xo_ref[0] + (HALF_STEP * gate) * acc_scr[...]


def _ffn_kernel(x_ref, mod_ref, wg_ref, wu_ref, wd_ref, o_ref, h_scr, acc_scr, *, row0):
    o_ref[0] = x_ref[0]
    _ffn_half_step(o_ref, mod_ref, row0, wg_ref, wu_ref, wd_ref, h_scr, acc_scr)


def _layer_spec(shape, layer):
    nd = len(shape)
    return pl.BlockSpec((None,) + tuple(shape[1:]), lambda *_: (layer,) + (0,) * (nd - 1),
                        pipeline_mode=pl.Buffered(1))


def _ffn(x, mod_l, w_gate, w_up, w_down, layer, row0, tm=1024):
    b, s, d = x.shape
    return pl.pallas_call(
        functools.partial(_ffn_kernel, row0=row0),
        grid=(b, s // tm),
        in_specs=[
            pl.BlockSpec((1, tm, d), lambda bi, i: (bi, i, 0)),
            pl.BlockSpec((1, N_MOD, d), lambda bi, i: (bi, 0, 0)),
            _layer_spec(w_gate.shape, layer),
            _layer_spec(w_up.shape, layer),
            _layer_spec(w_down.shape, layer),
        ],
        out_specs=pl.BlockSpec((1, tm, d), lambda bi, i: (bi, i, 0)),
        out_shape=jax.ShapeDtypeStruct((b, s, d), F32),
        scratch_shapes=[pltpu.VMEM((tm, d), BF16), pltpu.VMEM((tm, d), F32)],
        compiler_params=_params(("arbitrary", "arbitrary")),
        name="ffn",
    )(x, mod_l, w_gate, w_up, w_down)


def _swap_halves(x):
    n = x.shape[-1]
    lane = lax.broadcasted_iota(jnp.int32, x.shape, x.ndim - 1)
    up = pltpu.roll(x, n - HEAD_DIM // 2, x.ndim - 1)
    down = pltpu.roll(x, HEAD_DIM // 2, x.ndim - 1)
    return jnp.where(jnp.bitwise_and(lane, HEAD_DIM - 1) < HEAD_DIM // 2, up, down)


def _qk_norm_rope(x, gain, cos, sin_signed, head_mean):
    sq_hi, sq_lo = _split_bf16(x * x)
    ms = _dot(sq_hi, head_mean) + _dot(sq_lo, head_mean)
    xn = x * lax.rsqrt(ms + EPS) * gain
    return xn * cos + _swap_halves(xn) * sin_signed


def _hgrn_gate_terms(z, logits, layer):
    ex = jnp.exp(logits - jnp.max(logits, axis=0, keepdims=True))
    sm = ex / jnp.sum(ex, axis=0, keepdims=True)
    lb = jnp.zeros((1, HGRN_W), F32)
    for l in range(1, layer + 1):
        lb = lb + sm[l:l + 1, :]
    lb = jnp.clip(lb, 0.0, 1.0 - EPS)
    lb_floor = jnp.maximum(lb, LB_FLOOR)
    log_lb = jnp.log(lb_floor)
    one_m_lb = 1.0 - lb
    e = jnp.exp(-jnp.abs(z))
    one_p_e = 1.0 + e
    bterm = jnp.log(one_m_lb) + (jnp.minimum(z, 0.0) - jnp.log(one_p_e))
    log_f = jnp.maximum(log_lb, bterm) + jnp.log(1.0 + jnp.exp(-jnp.abs(log_lb - bterm)))
    sigmoid_neg = jnp.where(z > 0.0, e, 1.0) / one_p_e
    one_m_f = one_m_lb * sigmoid_neg - (lb_floor - lb)
    return log_f, one_m_f


def _inproj_kernel(x_ref, mod_ref, w_ref, cos_ref, sin_ref, qg_ref, kg_ref, hm_ref,
                   qa_ref, ka_ref, va_ref, qd_ref, kd_ref, vd_ref, qh_ref, fh_ref, ih_ref, gh_ref):
    x = x_ref[0]
    shift = mod_ref[0, 3:4, :]
    scale = mod_ref[0, 4:5, :]
    h = (x * _rms_scale(x) * (1.0 + scale) + shift).astype(BF16)

    def proj(off, width):
        return _dot(h, w_ref[:, off:off + width])

    def put_halves(ref, val):
        for half in range(N_HALVES):
            ref[0, half] = val[:, half * LANES:(half + 1) * LANES]

    qd_raw = proj(OFF_QD, SLAB)
    kd_raw = proj(OFF_KD, SLAB)
    qh_ref[0] = proj(OFF_QH, HGRN_W)
    cos = cos_ref[...]
    sin = sin_ref[...]
    hm = hm_ref[...]
    qd = _qk_norm_rope(qd_raw, qg_ref[...], cos, sin, hm)
    put_halves(qd_ref, qd * (HEAD_DIM ** -0.5))
    put_halves(kd_ref, _qk_norm_rope(kd_raw, kg_ref[...], cos, sin, hm))
    fh_ref[0] = proj(OFF_FH, HGRN_W)
    gh_ref[0] = proj(OFF_GH, HGRN_W)
    ih_ref[0] = proj(OFF_IH, HGRN_W).astype(BF16)
    qa_ref[0] = proj(OFF_QA, SLAB)
    ka_ref[0] = proj(OFF_KA, SLAB).astype(BF16)
    va_ref[0] = proj(OFF_VA, SLAB).astype(BF16)
    put_halves(vd_ref, proj(OFF_VD, SLAB))


def _rope_tables(s):
    half = HEAD_DIM // 2
    inv_freq = ROPE_THETA ** (-jnp.arange(half, dtype=F32) * 2.0 / HEAD_DIM)
    ang = jnp.arange(s, dtype=F32)[:, None] * inv_freq[None, :]
    cos, sin = jnp.cos(ang), jnp.sin(ang)
    cos_full = jnp.tile(jnp.concatenate([cos, cos], axis=-1), (1, N_HEADS))
    sin_signed = jnp.tile(jnp.concatenate([-sin, sin], axis=-1), (1, N_HEADS))
    return cos_full, sin_signed


def _inproj(x, mod_l, w_in, layer, q_gain, k_gain, cos, sin, tm=1024):
    b, s, d = x.shape
    qg = jnp.tile(q_gain.reshape(1, HEAD_DIM), (1, N_HEADS))
    kg = jnp.tile(k_gain.reshape(1, HEAD_DIM), (1, N_HEADS))
    head_id = np.arange(SLAB) // HEAD_DIM
    head_mean = jnp.asarray(np.where(head_id[:, None] == head_id[None, :], 1.0 / HEAD_DIM, 0.0), BF16)

    def tok(width):
        return pl.BlockSpec((1, tm, width), lambda bi, i: (bi, i, 0))

    def out(width, dt):
        return jax.ShapeDtypeStruct((b, s, width), dt)

    halves = pl.BlockSpec((1, N_HALVES, tm, LANES), lambda bi, i: (bi, 0, i, 0))
    halves_out = jax.ShapeDtypeStruct((b, N_HALVES, s, LANES), F32)

    return pl.pallas_call(
        _inproj_kernel,
        grid=(b, s // tm),
        in_specs=[
            tok(d),
            pl.BlockSpec((1, N_MOD, d), lambda bi, i: (bi, 0, 0)),
            _layer_spec(w_in.shape, layer),
            pl.BlockSpec((tm, SLAB), lambda bi, i: (i, 0)),
            pl.BlockSpec((tm, SLAB), lambda bi, i: (i, 0)),
            _const_spec((1, SLAB)),
            _const_spec((1, SLAB)),
            _const_spec((SLAB, SLAB)),
        ],
        out_specs=[tok(SLAB)] * 3 + [halves] * 3 + [tok(HGRN_W)] * 4,
        out_shape=[out(SLAB, F32), out(SLAB, BF16), out(SLAB, BF16),
                   halves_out, halves_out, halves_out,
                   out(HGRN_W, F32), out(HGRN_W, F32), out(HGRN_W, BF16), out(HGRN_W, F32)],
        compiler_params=_params(("arbitrary", "arbitrary")),
        name="mixer_inproj",
    )(x, mod_l, w_in, cos, sin, qg, kg, head_mean)


SB_QBLK = 256
SB_KBLK = 128


def _sb_kernel(q_ref, k_ref, v_ref, uu_ref, o_ref, acc_ref, run_ref, qm_ref, z_ref):
    i = pl.program_id(1)
    qblk, kblk = SB_QBLK, SB_KBLK
    heads = range(N_HEADS)
    lane = lax.broadcasted_iota(jnp.int32, (1, SLAB), 1)
    head_masks = [_head_of_lane(lane) == hd for hd in heads]
    qm_ref[...] = (q_ref[0] * (HEAD_DIM ** -0.5)).astype(BF16)
    acc_ref[...] = jnp.zeros_like(acc_ref)
    run_ref[...] = jnp.zeros_like(run_ref)

    def per_head_rows(x):
        return jnp.concatenate([jnp.where(head_masks[hd], x, jnp.zeros_like(x)) for hd in heads], axis=0)

    def scores(j):
        off = pl.multiple_of(jnp.maximum(j, 0) * kblk, kblk)
        z_all = _dot_nt(qm_ref[...], per_head_rows(k_ref[0, pl.ds(off, kblk), :]))
        return [z_all[:, hd * kblk:(hd + 1) * kblk] for hd in heads]

    def key_block(j, z, causal_below, first_row=0):
        off = pl.multiple_of(j * kblk, kblk)
        v = v_ref[0, pl.ds(off, kblk), :]
        uu = uu_ref[...]
        rows = slice(first_row, qblk)
        causal = None
        if causal_below is not None:
            shape = (qblk - first_row, kblk)
            col_minus_row = (lax.broadcasted_iota(jnp.int32, shape, 1)
                             - lax.broadcasted_iota(jnp.int32, shape, 0))
            causal = col_minus_row < causal_below + first_row
        zr = [z[hd][rows] for hd in heads]
        ls = [_log_sigmoid(zr[hd]) for hd in heads]
        lnb = [ls[hd] - zr[hd] for hd in heads]
        if causal is not None:
            lnb = [jnp.where(causal, x, 0.0) for x in lnb]
        tail = []
        for hd in heads:
            lnb_hi, lnb_lo = _split_bf16(lnb[hd])
            tail.append(_dot(jnp.concatenate([lnb_hi, lnb_lo], axis=1), uu))
        run = [run_ref[hd, rows, :] for hd in heads]
        w = [jnp.exp(ls[hd] + tail[hd] + run[hd]) for hd in heads]
        if causal is not None:
            w = [jnp.where(causal, x, 0.0) for x in w]
        pv = _dot(jnp.concatenate([w[hd].astype(BF16) for hd in heads], axis=1), per_head_rows(v))
        run_max = None
        for hd in heads:
            new_run = run[hd] + (tail[hd][:, 0:1] + lnb[hd][:, 0:1])
            run_ref[hd, rows, :] = new_run
            run_max = new_run if run_max is None else jnp.maximum(run_max, new_run)
        acc_ref[rows, :] += pv
        return jnp.max(run_max) > -SB_DEAD

    per_q = qblk // kblk
    top = i * per_q + per_q - 1
    z_now = scores(top)
    alive = None
    for d in range(per_q):
        z_next = scores(top - d - 1)
        if d == per_q - 1:
            for hd in heads:
                z_ref[hd] = z_next[hd]
        alive = key_block(top - d, z_now, (d + 1 - per_q) * kblk, first_row=(per_q - 1 - d) * kblk)
        z_now = z_next

    def cond(carry):
        j, alive = carry
        return jnp.logical_and(j >= 0, alive)

    def body(carry):
        j, _ = carry
        z = [z_ref[hd] for hd in heads]
        z_ahead = scores(j - 1)
        alive = key_block(j, z, None)
        for hd in heads:
            z_ref[hd] = z_ahead[hd]
        return j - 1, alive

    lax.while_loop(cond, body, (top - per_q, alive))
    o_ref[0] = acc_ref[...]


def _stick_breaking(q, k, v):
    b, s, _ = q.shape
    qblk, kblk = SB_QBLK, SB_KBLK
    idx = np.arange(kblk)
    u = (idx[:, None] > idx[None, :])
    uu = jnp.asarray(np.concatenate([u, u], axis=0), BF16)
    return pl.pallas_call(
        _sb_kernel,
        grid=(b, s // qblk),
        in_specs=[
            pl.BlockSpec((1, qblk, SLAB), lambda bi, i: (bi, i, 0)),
            pl.BlockSpec((1, s, SLAB), lambda bi, i: (bi, 0, 0), pipeline_mode=pl.Buffered(1)),
            pl.BlockSpec((1, s, SLAB), lambda bi, i: (bi, 0, 0), pipeline_mode=pl.Buffered(1)),
            _const_spec((2 * kblk, kblk)),
        ],
        out_specs=pl.BlockSpec((1, qblk, SLAB), lambda bi, i: (bi, i, 0)),
        out_shape=jax.ShapeDtypeStruct((b, s, SLAB), F32),
        scratch_shapes=[pltpu.VMEM((qblk, SLAB), F32),
                        pltpu.VMEM((N_HEADS, qblk, kblk), F32),
                        pltpu.VMEM((qblk, SLAB), BF16),
                        pltpu.VMEM((N_HEADS, qblk, kblk), F32)],
        compiler_params=_params(("arbitrary", "arbitrary")),
        name="stick_breaking",
    )(q, k, v, uu)


DIL_TILE = DIL_QBLK * max(r for _, r in DIL_PATTERNS)
DIL_UNROLL = 2
DIL_FOLD = 4


def _dil_kernel(q_ref, kp_ref, kc_ref, vp_ref, vc_ref, bias_ref, o_ref,
                k_all, v_all, q_fold, k_fold, v_fold, num_s, den_s, mx_s):
    n = pl.program_id(1)
    t_len = DIL_TILE
    qb = DIL_QBLK
    k_all[:, 0:t_len, :] = kp_ref[0].astype(BF16)
    k_all[:, t_len:2 * t_len, :] = kc_ref[0].astype(BF16)
    v_all[:, 0:t_len, :] = vp_ref[0].astype(BF16)
    v_all[:, t_len:2 * t_len, :] = vc_ref[0].astype(BF16)
    part = t_len // DIL_FOLD
    for half in range(N_HALVES):
        for f in range(DIL_FOLD):
            fold_rows = pl.ds(f, part, stride=DIL_FOLD)
            q_fold[half, f * part:(f + 1) * part, :] = q_ref[0, half, fold_rows, :]
            for dst, prev_ref, cur_ref in ((k_fold, kp_ref, kc_ref), (v_fold, vp_ref, vc_ref)):
                dst[half, 2 * f * part:(2 * f + 1) * part, :] = prev_ref[0, half, fold_rows, :]
                dst[half, (2 * f + 1) * part:(2 * f + 2) * part, :] = cur_ref[0, half, fold_rows, :]
    lane = lax.broadcasted_iota(jnp.int32, (1, LANES), 1)
    head_masks = [_head_of_lane(lane) == hh for hh in range(HEADS_PER_HALF)]

    for pi, (window, r) in enumerate(reversed(DIL_PATTERNS)):
        assert window // r == qb
        opening = pi == 0
        last = pi == len(DIL_PATTERNS) - 1
        shift = r.bit_length() - 1
        tiles = t_len // qb

        def rows_of(start, size, r=r):
            return pl.ds(start, size) if r == 1 else pl.ds(start, size, stride=r)

        def tile_group(it, carry, r=r, shift=shift, opening=opening, last=last, rows_of=rows_of):
            folded = r >= DIL_FOLD
            q_rows, k_rows, q_src, k_src, bias = [], [], [], [], []
            for u in range(DIL_UNROLL):
                idx = it * DIL_UNROLL + u
                t = jnp.right_shift(idx, shift)
                j = jnp.bitwise_and(idx, r - 1)
                q0 = t * (qb * r) + j
                has_prev = jnp.logical_or(n > 0, t > 0)
                bias.append(bias_ref[jnp.where(has_prev, 0, 1)])
                q_rows.append(rows_of(q0, qb))
                k_rows.append(rows_of(t_len + q0 - qb * r, 2 * qb))
                if folded:
                    part = t_len // DIL_FOLD
                    r_fold = r // DIL_FOLD
                    f = jnp.bitwise_and(j, DIL_FOLD - 1)
                    m0 = t * (qb * r_fold) + jnp.right_shift(j, DIL_FOLD.bit_length() - 1)

                    def fold_ds(start, size, r_fold=r_fold):
                        return pl.ds(start, size) if r_fold == 1 else pl.ds(start, size, stride=r_fold)

                    q_src.append(fold_ds(f * part + m0, qb))
                    k_src.append(fold_ds(2 * f * part + part + m0 - qb * r_fold, 2 * qb))
            slabs = [(u, half) for u in range(DIL_UNROLL) for half in range(N_HALVES)]
            units = [(si, hh) for si in range(len(slabs)) for hh in range(HEADS_PER_HALF)]
            if folded:
                qs = [q_fold[half, q_src[u], :] for u, half in slabs]
                kb = [k_fold[half, k_src[u], :].astype(BF16) for u, half in slabs]
                vb = [v_fold[half, k_src[u], :].astype(BF16) for u, half in slabs]
            else:
                qs = [q_ref[0, half, q_rows[u], :] for u, half in slabs]
                kb = [k_all[half, k_rows[u], :].astype(BF16) for u, half in slabs]
                vb = [v_all[half, k_rows[u], :].astype(BF16) for u, half in slabs]
            sc = [_dot_nt(jnp.where(head_masks[hh], qs[si], 0.0).astype(BF16), kb[si]) + bias[slabs[si][0]]
                  for si, hh in units]
            mx = [jnp.max(x, axis=-1, keepdims=True) for x in sc]
            p = [jnp.exp(x - m) for x, m in zip(sc, mx)]
            den = [jnp.sum(x, axis=-1, keepdims=True) for x in p]
            pv = [_dot(x.astype(BF16), vb[si]) for x, (si, _) in zip(p, units)]
            for si, (u, half) in enumerate(slabs):
                first = si * HEADS_PER_HALF
                num_t, den_t, mx_t = pv[first], den[first], mx[first]
                for hh in range(1, HEADS_PER_HALF):
                    hm = head_masks[hh]
                    num_t = jnp.where(hm, pv[first + hh], num_t)
                    den_t = jnp.where(hm, den[first + hh], den_t)
                    mx_t = jnp.where(hm, mx[first + hh], mx_t)
                rows = q_rows[u]
                if opening:
                    m_new, num_new, den_new = mx_t, num_t, den_t
                else:
                    m_old = mx_s[half, rows, :]
                    m_new = jnp.maximum(m_old, mx_t)
                    a_old = jnp.exp(m_old - m_new)
                    a_new = jnp.exp(mx_t - m_new)
                    num_new = num_s[half, rows, :] * a_old + num_t * a_new
                    den_new = den_s[half, rows, :] * a_old + den_t * a_new
                if last:
                    o_ref[0, half, rows, :] = num_new / den_new
                else:
                    mx_s[half, rows, :] = jnp.broadcast_to(m_new, (qb, LANES))
                    num_s[half, rows, :] = num_new
                    den_s[half, rows, :] = jnp.broadcast_to(den_new, (qb, LANES))
            return carry

        lax.fori_loop(0, tiles // DIL_UNROLL, tile_group, 0)


def _dilated(q, k, v):
    b, _, s, _ = q.shape
    qb = DIL_QBLK
    t_len = DIL_TILE
    a_idx = np.arange(qb)[:, None]
    k_idx = np.arange(2 * qb)[None, :]
    dist = a_idx + qb - k_idx
    band = (dist >= 0) & (dist <= qb)
    bias = np.stack([np.where(band, 0.0, NEG_BIG), np.where(band & (k_idx >= qb), 0.0, NEG_BIG)])
    cur = pl.BlockSpec((1, N_HALVES, t_len, LANES), lambda bi, n: (bi, 0, n, 0))
    prev = pl.BlockSpec((1, N_HALVES, t_len, LANES), lambda bi, n: (bi, 0, jnp.maximum(n - 1, 0), 0))
    stat = pltpu.VMEM((N_HALVES, t_len, LANES), F32)
    both = pltpu.VMEM((N_HALVES, 2 * t_len, LANES), F32)
    both_bf16 = pltpu.VMEM((N_HALVES, 2 * t_len, LANES), BF16)
    return pl.pallas_call(
        _dil_kernel,
        grid=(b, s // t_len),
        in_specs=[cur, prev, cur, prev, cur, _const_spec((2, qb, 2 * qb))],
        out_specs=cur,
        out_shape=jax.ShapeDtypeStruct((b, N_HALVES, s, LANES), F32),
        scratch_shapes=[both_bf16, both_bf16, stat, both, both, stat, stat, stat],
        compiler_params=_params(("arbitrary", "arbitrary")),
        name="dilated",
    )(q, k, k, v, v, jnp.asarray(bias, F32))


HGRN_CHUNK = 128
HGRN_LEVELS = HGRN_CHUNK.bit_length() - 1
HGRN_TILE = 1024
SUBLANES = 8


LOG2_E = 1.4426950408889634


def _log2_split_decay(g, log_f, half):
    c, w = g.shape
    block = 2 * half
    if block >= SUBLANES:
        g3 = g.reshape(c // block, block, w)
        upper = lax.broadcasted_iota(jnp.int32, (1, block, 1), 1) >= half
        return ((g3 - g3[:, half - 1:half, :]) * jnp.where(upper, LOG2_E, -LOG2_E)).reshape(c, w)
    pos = jnp.bitwise_and(lax.broadcasted_iota(jnp.int32, (c, 1), 0), block - 1)
    if half == 1:
        return jnp.where(pos == 1, log_f * LOG2_E, 0.0)
    assert half == 2
    nxt = pltpu.roll(log_f, c - 1, 0)
    prv = pltpu.roll(log_f, 1, 0)
    dist = jnp.where(pos == 0, nxt, jnp.where(pos == 1, 0.0, jnp.where(pos == 2, log_f, log_f + prv)))
    return dist * LOG2_E


def _hgrn_kernel(q_ref, f_ref, i_ref, g_ref, lbl_ref, ng_ref, tril_ref, lvl_ref, o_ref,
                 state_ref, *, layer):
    @pl.when(pl.program_id(1) == 0)
    def _():
        state_ref[...] = jnp.zeros_like(state_ref)

    c = HGRN_CHUNK
    chunks = range(HGRN_TILE // c)
    heads = range(HGRN_HEADS)
    lanes = [slice(hd * HGRN_D, (hd + 1) * HGRN_D) for hd in heads]
    rows = [slice(ci * c, (ci + 1) * c) for ci in chunks]
    norm_gain = ng_ref[...]
    tril = tril_ref[...]
    logits = lbl_ref[...]

    g_last, q_dec, k_dec, q_lvl, k_lvl = [], [], [], [], []
    for ci in chunks:
        zq = q_ref[0, rows[ci], :]
        log_f, one_m_f = _hgrn_gate_terms(f_ref[0, rows[ci], :], logits, layer)
        lf_hi, lf_lo = _split_bf16(log_f)
        g_c = _dot(tril, lf_hi) + _dot(tril, lf_lo)
        qq_c = zq * _sigmoid(zq)
        g_last.append(g_c[c - 1:c, :])
        q_dec.append((qq_c * jnp.exp(g_c)).astype(BF16))
        k_dec.append((one_m_f * jnp.exp(g_last[ci] - g_c)).astype(BF16))
        ql = [qq_c.astype(BF16)]
        kl = [one_m_f.astype(BF16)]
        for lv in range(HGRN_LEVELS):
            decay = jnp.exp2(_log2_split_decay(g_c, log_f, 1 << lv).astype(BF16))
            ql.append(ql[0] * decay)
            kl.append(kl[0] * decay)
        q_lvl.append(ql)
        k_lvl.append(kl)

    val = [i_ref[0, rows[ci], :] for ci in chunks]
    updates = [[_dot_tn(val[ci][:, lanes[hd]], k_dec[ci][:, lanes[hd]]) for hd in heads] for ci in chunks]
    states = [[state_ref[hd] for hd in heads]]
    for ci in chunks:
        states.append([states[ci][hd] * jnp.exp(g_last[ci][:, lanes[hd]]) + updates[ci][hd]
                       for hd in heads])
    for hd in heads:
        state_ref[hd] = states[-1][hd]
    o_inter = [[_dot_nt(q_dec[ci][:, lanes[hd]], states[ci][hd].astype(BF16)) for hd in heads]
               for ci in chunks]
    scores = [[[_dot_nt(q_lvl[ci][lv][:, lanes[hd]], k_lvl[ci][lv][:, lanes[hd]])
                for lv in range(HGRN_LEVELS + 1)] for hd in heads] for ci in chunks]

    lvl = lvl_ref[...]
    level_masks = [lvl == lv for lv in range(HGRN_LEVELS + 1)]
    for ci in chunks:
        outs = []
        for hd in heads:
            p = jnp.zeros((c, c), F32)
            for lv in range(HGRN_LEVELS + 1):
                p = jnp.where(level_masks[lv], scores[ci][hd][lv], p)
            o_h = _dot(p.astype(BF16), val[ci][:, lanes[hd]]) + o_inter[ci][hd]
            outs.append(o_h * _rms_scale(o_h) * norm_gain)
        zg = g_ref[0, rows[ci], :]
        o_ref[0, rows[ci], :] = jnp.concatenate(outs, axis=-1) * (zg * _sigmoid(zg))


def _hgrn_tables():
    c = HGRN_CHUNK
    idx = np.arange(c)
    tril = idx[:, None] >= idx[None, :]
    diff = idx[:, None] ^ idx[None, :]
    lvl = np.where(idx[:, None] > idx[None, :], np.floor(np.log2(np.maximum(diff, 1))) + 1, -1)
    lvl = np.where(idx[:, None] == idx[None, :], 0, lvl).astype(np.int32)
    return jnp.asarray(tril, BF16), jnp.asarray(lvl)


def _hgrn(qh, fh, ih, gh, lb_logits, norm_gain, layer):
    b, s, _ = qh.shape
    t = HGRN_TILE
    c = HGRN_CHUNK
    depth = lb_logits.shape[0]
    tril, lvl = _hgrn_tables()
    tok = pl.BlockSpec((1, t, HGRN_W), lambda bi, i: (bi, i, 0))
    return pl.pallas_call(
        functools.partial(_hgrn_kernel, layer=layer),
        grid=(b, s // t),
        in_specs=[tok, tok, tok, tok,
                  _const_spec((depth, HGRN_W)),
                  _const_spec((1, HGRN_D)),
                  _const_spec((c, c)),
                  _const_spec((c, c))],
        out_specs=tok,
        out_shape=jax.ShapeDtypeStruct((b, s, HGRN_W), F32),
        scratch_shapes=[pltpu.VMEM((HGRN_HEADS, HGRN_D, HGRN_D), F32)],
        compiler_params=_params(("arbitrary", "arbitrary")),
        name="hgrn2",
    )(qh, fh, ih, gh, lb_logits.astype(F32), norm_gain.reshape(1, HGRN_D), tril, lvl)


def _outproj_ffn_kernel(x_ref, mod_ref, oa_ref, od_ref, oh_ref, wo_ref, wg_ref, wu_ref, wd_ref, o_ref,
                        h_scr, acc_scr):
    y = _dot(oa_ref[0].astype(BF16), wo_ref[0:SLAB, :])
    for half in range(N_HALVES):
        lo = SLAB + half * LANES
        y += _dot(od_ref[0, half].astype(BF16), wo_ref[lo:lo + LANES, :])
    y += _dot(oh_ref[0].astype(BF16), wo_ref[2 * SLAB:, :])
    gate = mod_ref[0, 5:6, :]
    o_ref[0] = x_ref[0] + gate * y
    _ffn_half_step(o_ref, mod_ref, 6, wg_ref, wu_ref, wd_ref, h_scr, acc_scr)


def _outproj_ffn(x, mod_l, o_a, o_d, o_h, w_out, w_gate, w_up, w_down, layer, tm=512):
    b, s, d = x.shape

    def tok(width):
        return pl.BlockSpec((1, tm, width), lambda bi, i: (bi, i, 0))

    return pl.pallas_call(
        _outproj_ffn_kernel,
        grid=(b, s // tm),
        in_specs=[tok(d), pl.BlockSpec((1, N_MOD, d), lambda bi, i: (bi, 0, 0)), tok(SLAB),
                  pl.BlockSpec((1, N_HALVES, tm, LANES), lambda bi, i: (bi, 0, i, 0)),
                  tok(HGRN_W), _layer_spec(w_out.shape, layer),
                  _layer_spec(w_gate.shape, layer), _layer_spec(w_up.shape, layer),
                  _layer_spec(w_down.shape, layer)],
        out_specs=tok(d),
        out_shape=jax.ShapeDtypeStruct((b, s, d), F32),
        scratch_shapes=[pltpu.VMEM((tm, d), BF16), pltpu.VMEM((tm, d), F32)],
        compiler_params=_params(("arbitrary", "arbitrary")),
        name="outproj_ffn",
    )(x, mod_l, o_a, o_d, o_h, w_out, w_gate, w_up, w_down)


def kernel(x, c, w_mod, b_mod, ffn1_w_gate, ffn1_w_up, ffn1_w_down, w_in, w_out, q_norm_g, k_norm_g,
           hgrn_norm_g, hgrn_lb_logits, ffn2_w_gate, ffn2_w_up, ffn2_w_down):
    b, s, d = x.shape
    depth = w_mod.shape[0]
    assert s % DIL_TILE == 0 and s % HGRN_TILE == 0 and s % SB_QBLK == 0
    mod = _modulation(c, w_mod, b_mod).reshape(depth, b, N_MOD, d)
    cos, sin = _rope_tables(s)
    ffn1 = [w.astype(BF16) for w in (ffn1_w_gate, ffn1_w_up, ffn1_w_down)]
    ffn2 = [w.astype(BF16) for w in (ffn2_w_gate, ffn2_w_up, ffn2_w_down)]
    w_in_b = w_in.astype(BF16)
    w_out_b = w_out.astype(BF16)
    for l in range(depth):
        x = _ffn(x, mod[l], *ffn1, layer=l, row0=0)
        qa, ka, va, qd, kd, vd, qh, fh, ih, gh = _inproj(
            x, mod[l], w_in_b, l, q_norm_g[l], k_norm_g[l], cos, sin)
        o_a = _stick_breaking(qa, ka, va)
        o_d = _dilated(qd, kd, vd)
        o_h = _hgrn(qh, fh, ih, gh, hgrn_lb_logits, hgrn_norm_g[l], layer=l)
        x = _outproj_ffn(x, mod[l], o_a, o_d, o_h, w_out_b, *ffn2, layer=l)
    return x
```

```python
import functools

import jax
import jax.numpy as jnp
import numpy as np
from jax import lax
from jax.experimental import pallas as pl
from jax.experimental.pallas import tpu as pltpu

F32 = jnp.float32
BF16 = jnp.bfloat16

HEAD_DIM = 64
N_HEADS = 4
SLAB = N_HEADS * HEAD_DIM
LANES = 128
N_HALVES = SLAB // LANES
HEADS_PER_HALF = LANES // HEAD_DIM
HGRN_HEADS = 4
HGRN_D = 128
HGRN_W = HGRN_HEADS * HGRN_D
N_MOD = 9
EPS = 1e-6
LB_FLOOR = 1e-30
NEG_BIG = -1e30
HALF_STEP = 0.5
ROPE_THETA = 10000.0
DIL_PATTERNS = ((128, 1), (512, 4), (2048, 16))
DIL_QBLK = 128

OFF_QA, OFF_KA, OFF_VA = 0, 256, 512
OFF_QD, OFF_KD, OFF_VD = 768, 1024, 1280
OFF_QH, OFF_FH, OFF_IH, OFF_GH = 1536, 2048, 2560, 3072

VMEM_LIMIT = 56 * 1024 * 1024
SB_DEAD = 104.0


def _params(sem):
    return pltpu.CompilerParams(dimension_semantics=sem, vmem_limit_bytes=VMEM_LIMIT)


def _const_spec(shape):
    nd = len(shape)
    return pl.BlockSpec(shape, lambda *_: (0,) * nd, pipeline_mode=pl.Buffered(1))


def _split_bf16(x):
    hi = x.astype(BF16)
    lo = (x - hi.astype(F32)).astype(BF16)
    return hi, lo


def _dot(a, b):
    return jnp.dot(a, b, preferred_element_type=F32)


def _dot_nt(a, b):
    return lax.dot_general(a, b, (((1,), (1,)), ((), ())), preferred_element_type=F32)


def _dot_tn(a, b):
    return lax.dot_general(a, b, (((0,), (0,)), ((), ())), preferred_element_type=F32)


def _sigmoid(x):
    return 1.0 / (1.0 + jnp.exp(-x))


def _log_sigmoid(x):
    return jnp.minimum(x, 0.0) - jnp.log(1.0 + jnp.exp(-jnp.abs(x)))


def _rms_scale(x):
    return lax.rsqrt(jnp.mean(x * x, axis=-1, keepdims=True) + EPS)


def _head_of_lane(lane):
    return jnp.right_shift(lane, HEAD_DIM.bit_length() - 1)


def _mod_kernel(c_ref, w_ref, b_ref, o_ref):
    c = c_ref[...]
    sc = c * _sigmoid(c)
    sc_hi, sc_lo = _split_bf16(sc)
    w = w_ref[0]
    w_hi, w_lo = _split_bf16(w)
    o_ref[0] = _dot(sc_hi, w_hi) + (_dot(sc_hi, w_lo) + _dot(sc_lo, w_hi)) + b_ref[0]


def _modulation(c, w_mod, b_mod):
    depth, d, n = w_mod.shape
    rows = 16
    nb = c.shape[0]
    assert nb <= rows
    c = jnp.pad(c, ((0, rows - nb), (0, 0)))
    b = rows
    tn = 1152
    out = pl.pallas_call(
        _mod_kernel,
        grid=(depth, n // tn),
        in_specs=[
            pl.BlockSpec((b, d), lambda l, j: (0, 0)),
            pl.BlockSpec((1, d, tn), lambda l, j: (l, 0, j)),
            pl.BlockSpec((1, 1, tn), lambda l, j: (l, 0, j)),
        ],
        out_specs=pl.BlockSpec((1, b, tn), lambda l, j: (l, 0, j)),
        out_shape=jax.ShapeDtypeStruct((depth, b, n), F32),
        compiler_params=_params(("arbitrary", "arbitrary")),
        name="adaln_mod",
    )(c, w_mod, b_mod.reshape(depth, 1, n))
    return out[:, :nb]


FFN_CHUNK = 256


def _ffn_half_step(xo_ref, mod_ref, row0, wg_ref, wu_ref, wd_ref, h_scr, acc_scr):
    shift = mod_ref[0, row0:row0 + 1, :]
    scale = mod_ref[0, row0 + 1:row0 + 2, :]
    gate = mod_ref[0, row0 + 2:row0 + 3, :]
    x = xo_ref[0]
    h_scr[...] = (x * _rms_scale(x) * (1.0 + scale) + shift).astype(BF16)
    for c in range(wg_ref.shape[1] // FFN_CHUNK):
        cols = slice(c * FFN_CHUNK, (c + 1) * FFN_CHUNK)
        g = _dot(h_scr[...], wg_ref[:, cols])
        u = _dot(h_scr[...], wu_ref[:, cols])
        a = (g * _sigmoid(g) * u).astype(BF16)
        y = _dot(a, wd_ref[cols, :])
        if c == 0:
            acc_scr[...] = y
        else:
            acc_scr[...] += y
    xo_ref[0] = xo_ref[0] + (HALF_STEP * gate) * acc_scr[...]


def _ffn_kernel(x_ref, mod_ref, wg_ref, wu_ref, wd_ref, o_ref, h_scr, acc_scr, *, row0):
    o_ref[0] = x_ref[0]
    _ffn_half_step(o_ref, mod_ref, row0, wg_ref, wu_ref, wd_ref, h_scr, acc_scr)


def _layer_spec(shape, layer):
    nd = len(shape)
    return pl.BlockSpec((None,) + tuple(shape[1:]), lambda *_: (layer,) + (0,) * (nd - 1),
                        pipeline_mode=pl.Buffered(1))


def _ffn(x, mod_l, w_gate, w_up, w_down, layer, row0, tm=1024):
    b, s, d = x.shape
    return pl.pallas_call(
        functools.partial(_ffn_kernel, row0=row0),
        grid=(b, s // tm),
        in_specs=[
            pl.BlockSpec((1, tm, d), lambda bi, i: (bi, i, 0)),
            pl.BlockSpec((1, N_MOD, d), lambda bi, i: (bi, 0, 0)),
            _layer_spec(w_gate.shape, layer),
            _layer_spec(w_up.shape, layer),
            _layer_spec(w_down.shape, layer),
        ],
        out_specs=pl.BlockSpec((1, tm, d), lambda bi, i: (bi, i, 0)),
        out_shape=jax.ShapeDtypeStruct((b, s, d), F32),
        scratch_shapes=[pltpu.VMEM((tm, d), BF16), pltpu.VMEM((tm, d), F32)],
        compiler_params=_params(("arbitrary", "arbitrary")),
        name="ffn",
    )(x, mod_l, w_gate, w_up, w_down)


def _swap_halves(x):
    n = x.shape[-1]
    lane = lax.broadcasted_iota(jnp.int32, x.shape, x.ndim - 1)
    up = pltpu.roll(x, n - HEAD_DIM // 2, x.ndim - 1)
    down = pltpu.roll(x, HEAD_DIM // 2, x.ndim - 1)
    return jnp.where(jnp.bitwise_and(lane, HEAD_DIM - 1) < HEAD_DIM // 2, up, down)


def _qk_norm_rope(x, gain, cos, sin_signed, head_mean):
    sq_hi, sq_lo = _split_bf16(x * x)
    ms = _dot(sq_hi, head_mean) + _dot(sq_lo, head_mean)
    xn = x * lax.rsqrt(ms + EPS) * gain
    return xn * cos + _swap_halves(xn) * sin_signed


def _hgrn_gate_terms(z, logits, layer):
    ex = jnp.exp(logits - jnp.max(logits, axis=0, keepdims=True))
    sm = ex / jnp.sum(ex, axis=0, keepdims=True)
    lb = jnp.zeros_like(sm[0:1, :])
    for l in range(1, layer + 1):
        lb = lb + sm[l:l + 1, :]
    lb = jnp.clip(lb, 0.0, 1.0 - EPS)
    lb_floor = jnp.maximum(lb, LB_FLOOR)
    log_lb = jnp.log(lb_floor)
    one_m_lb = 1.0 - lb
    e = jnp.exp(-jnp.abs(z))
    one_p_e = 1.0 + e
    bterm = jnp.log(one_m_lb) + (jnp.minimum(z, 0.0) - jnp.log(one_p_e))
    log_f = jnp.maximum(log_lb, bterm) + jnp.log(1.0 + jnp.exp(-jnp.abs(log_lb - bterm)))
    sigmoid_neg = jnp.where(z > 0.0, e, 1.0) / one_p_e
    one_m_f = one_m_lb * sigmoid_neg - (lb_floor - lb)
    return log_f, one_m_f


def _inproj_kernel(x_ref, mod_ref, w_ref, cos_ref, sin_ref, qg_ref, kg_ref, hm_ref,
                   qa_ref, ka_ref, va_ref, qd_ref, kd_ref, vd_ref, qh_ref, fh_ref, ih_ref, gh_ref):
    x = x_ref[0]
    shift = mod_ref[0, 3:4, :]
    scale = mod_ref[0, 4:5, :]
    h = (x * _rms_scale(x) * (1.0 + scale) + shift).astype(BF16)

    def proj(off, width):
        return _dot(h, w_ref[:, off:off + width])

    def put_halves(ref, val):
        for half in range(N_HALVES):
            ref[0, half] = val[:, half * LANES:(half + 1) * LANES]

    qd_raw = proj(OFF_QD, SLAB)
    kd_raw = proj(OFF_KD, SLAB)
    qh_ref[0] = proj(OFF_QH, HGRN_W)
    cos = cos_ref[...]
    sin = sin_ref[...]
    hm = hm_ref[...]
    qd = _qk_norm_rope(qd_raw, qg_ref[...], cos, sin, hm)
    put_halves(qd_ref, qd * (HEAD_DIM ** -0.5))
    put_halves(kd_ref, _qk_norm_rope(kd_raw, kg_ref[...], cos, sin, hm))
    fh_ref[0] = proj(OFF_FH, HGRN_W)
    gh_ref[0] = proj(OFF_GH, HGRN_W)
    ih_ref[0] = proj(OFF_IH, HGRN_W).astype(BF16)
    qa_ref[0] = proj(OFF_QA, SLAB)
    ka_ref[0] = proj(OFF_KA, SLAB).astype(BF16)
    va_ref[0] = proj(OFF_VA, SLAB).astype(BF16)
    put_halves(vd_ref, proj(OFF_VD, SLAB))


def _rope_tables(s):
    half = HEAD_DIM // 2
    inv_freq = ROPE_THETA ** (-jnp.arange(half, dtype=F32) * 2.0 / HEAD_DIM)
    ang = jnp.arange(s, dtype=F32)[:, None] * inv_freq[None, :]
    cos, sin = jnp.cos(ang), jnp.sin(ang)
    cos_full = jnp.tile(jnp.concatenate([cos, cos], axis=-1), (1, N_HEADS))
    sin_signed = jnp.tile(jnp.concatenate([-sin, sin], axis=-1), (1, N_HEADS))
    return cos_full, sin_signed


def _inproj(x, mod_l, w_in, layer, q_gain, k_gain, cos, sin, tm=1024):
    b, s, d = x.shape
    qg = jnp.tile(q_gain.reshape(1, HEAD_DIM), (1, N_HEADS))
    kg = jnp.tile(k_gain.reshape(1, HEAD_DIM), (1, N_HEADS))
    head_id = np.arange(SLAB) // HEAD_DIM
    head_mean = jnp.asarray(np.where(head_id[:, None] == head_id[None, :], 1.0 / HEAD_DIM, 0.0), BF16)

    def tok(width):
        return pl.BlockSpec((1, tm, width), lambda bi, i: (bi, i, 0))

    def out(width, dt):
        return jax.ShapeDtypeStruct((b, s, width), dt)

    halves = pl.BlockSpec((1, N_HALVES, tm, LANES), lambda bi, i: (bi, 0, i, 0))
    halves_out = jax.ShapeDtypeStruct((b, N_HALVES, s, LANES), F32)

    return pl.pallas_call(
        _inproj_kernel,
        grid=(b, s // tm),
        in_specs=[
            tok(d),
            pl.BlockSpec((1, N_MOD, d), lambda bi, i: (bi, 0, 0)),
            _layer_spec(w_in.shape, layer),
            pl.BlockSpec((tm, SLAB), lambda bi, i: (i, 0)),
            pl.BlockSpec((tm, SLAB), lambda bi, i: (i, 0)),
            _const_spec((1, SLAB)),
            _const_spec((1, SLAB)),
            _const_spec((SLAB, SLAB)),
        ],
        out_specs=[tok(SLAB)] * 3 + [halves] * 3 + [tok(HGRN_W)] * 4,
        out_shape=[out(SLAB, F32), out(SLAB, BF16), out(SLAB, BF16),
                   halves_out, halves_out, halves_out,
                   out(HGRN_W, F32), out(HGRN_W, F32), out(HGRN_W, BF16), out(HGRN_W, F32)],
        compiler_params=_params(("arbitrary", "arbitrary")),
        name="mixer_inproj",
    )(x, mod_l, w_in, cos, sin, qg, kg, head_mean)


SB_QBLK = 256
SB_KBLK = 128


def _sb_kernel(q_ref, k_ref, v_ref, uu_ref, o_ref, acc_ref, run_ref, qm_ref, z_ref):
    i = pl.program_id(1)
    qblk, kblk = SB_QBLK, SB_KBLK
    heads = range(N_HEADS)
    lane = lax.broadcasted_iota(jnp.int32, (1, SLAB), 1)
    head_masks = [_head_of_lane(lane) == hd for hd in heads]
    qm_ref[...] = (q_ref[0] * (HEAD_DIM ** -0.5)).astype(BF16)
    acc_ref[...] = jnp.zeros_like(acc_ref)
    run_ref[...] = jnp.zeros_like(run_ref)

    def per_head_rows(x):
        return jnp.concatenate([jnp.where(head_masks[hd], x, jnp.zeros_like(x)) for hd in heads], axis=0)

    def scores(j, rows=slice(0, SB_QBLK)):
        off = pl.multiple_of(jnp.maximum(j, 0) * kblk, kblk)
        z_all = _dot_nt(qm_ref[rows, :], per_head_rows(k_ref[0, pl.ds(off, kblk), :]))
        return [z_all[:, hd * kblk:(hd + 1) * kblk] for hd in heads]

    def key_block(j, z, causal_below, rows=slice(0, SB_QBLK)):
        off = pl.multiple_of(j * kblk, kblk)
        v = v_ref[0, pl.ds(off, kblk), :]
        uu = uu_ref[...]
        causal = None
        if causal_below is not None:
            shape = (rows.stop - rows.start, kblk)
            col_minus_row = (lax.broadcasted_iota(jnp.int32, shape, 1)
                             - lax.broadcasted_iota(jnp.int32, shape, 0))
            causal = col_minus_row < causal_below + rows.start
        ls = [_log_sigmoid(z[hd]) for hd in heads]
        lnb = [ls[hd] - z[hd] for hd in heads]
        if causal is not None:
            lnb = [jnp.where(causal, x, 0.0) for x in lnb]
        tail = []
        for hd in heads:
            lnb_hi, lnb_lo = _split_bf16(lnb[hd])
            tail.append(_dot(jnp.concatenate([lnb_hi, lnb_lo], axis=1), uu))
        run = [run_ref[hd, rows, :] for hd in heads]
        w = [jnp.exp(ls[hd] + tail[hd] + run[hd]) for hd in heads]
        if causal is not None:
            w = [jnp.where(causal, x, 0.0) for x in w]
        pv = _dot(jnp.concatenate([w[hd].astype(BF16) for hd in heads], axis=1), per_head_rows(v))
        run_max = None
        for hd in heads:
            new_run = run[hd] + (tail[hd][:, 0:1] + lnb[hd][:, 0:1])
            run_ref[hd, rows, :] = new_run
            run_max = new_run if run_max is None else jnp.maximum(run_max, new_run)
        acc_ref[rows, :] += pv
        return run_max

    def alive_in(run_max):
        return jnp.max(run_max) > -SB_DEAD

    per_q = qblk // kblk
    upper, lower = slice(0, qblk // 2), slice(qblk // 2, qblk)
    top = i * per_q + per_q - 1
    z_now = scores(top)
    run_max = None
    for d in range(per_q):
        z_next = scores(top - d - 1)
        if d == per_q - 1:
            for hd in heads:
                z_ref[hd] = z_next[hd]
        rows = slice((per_q - 1 - d) * kblk, qblk)
        run_max = key_block(top - d, [x[rows] for x in z_now], (d + 1 - per_q) * kblk, rows)
        z_now = z_next

    def both_alive(carry):
        j, up, low = carry
        return jnp.logical_and(j >= 0, jnp.logical_and(up, low))

    def all_rows(carry):
        j, _, _ = carry
        z = [z_ref[hd] for hd in heads]
        z_ahead = scores(j - 1)
        run_max = key_block(j, z, None)
        for hd in heads:
            z_ref[hd] = z_ahead[hd]
        return j - 1, alive_in(run_max[upper]), alive_in(run_max[lower])

    j_split, up_alive, low_alive = lax.while_loop(
        both_alive, all_rows, (top - per_q, alive_in(run_max[upper]), alive_in(run_max[lower])))

    def one_half(rows, alive):
        def cond(carry):
            j, alive = carry
            return jnp.logical_and(j >= 0, alive)

        def body(carry):
            j, _ = carry
            z = [z_ref[hd, rows, :] for hd in heads]
            z_ahead = scores(j - 1, rows)
            run_max = key_block(j, z, None, rows)
            for hd in heads:
                z_ref[hd, rows, :] = z_ahead[hd]
            return j - 1, alive_in(run_max)

        lax.while_loop(cond, body, (j_split, alive))

    one_half(upper, up_alive)
    one_half(lower, low_alive)
    o_ref[0] = acc_ref[...]


def _stick_breaking(q, k, v):
    b, s, _ = q.shape
    qblk, kblk = SB_QBLK, SB_KBLK
    idx = np.arange(kblk)
    u = (idx[:, None] > idx[None, :])
    uu = jnp.asarray(np.concatenate([u, u], axis=0), BF16)
    return pl.pallas_call(
        _sb_kernel,
        grid=(b, s // qblk),
        in_specs=[
            pl.BlockSpec((1, qblk, SLAB), lambda bi, i: (bi, i, 0)),
            pl.BlockSpec((1, s, SLAB), lambda bi, i: (bi, 0, 0), pipeline_mode=pl.Buffered(1)),
            pl.BlockSpec((1, s, SLAB), lambda bi, i: (bi, 0, 0), pipeline_mode=pl.Buffered(1)),
            _const_spec((2 * kblk, kblk)),
        ],
        out_specs=pl.BlockSpec((1, qblk, SLAB), lambda bi, i: (bi, i, 0)),
        out_shape=jax.ShapeDtypeStruct((b, s, SLAB), F32),
        scratch_shapes=[pltpu.VMEM((qblk, SLAB), F32),
                        pltpu.VMEM((N_HEADS, qblk, kblk), F32),
                        pltpu.VMEM((qblk, SLAB), BF16),
                        pltpu.VMEM((N_HEADS, qblk, kblk), F32)],
        compiler_params=_params(("arbitrary", "arbitrary")),
        name="stick_breaking",
    )(q, k, v, uu)


DIL_TILE = DIL_QBLK * max(r for _, r in DIL_PATTERNS)
DIL_UNROLL = 2
DIL_FOLD = 4


def _dil_kernel(q_ref, kp_ref, kc_ref, vp_ref, vc_ref, bias_ref, o_ref,
                k_all, v_all, q_fold, k_fold, v_fold, num_s, den_s, mx_s):
    n = pl.program_id(1)
    t_len = DIL_TILE
    qb = DIL_QBLK
    k_all[:, 0:t_len, :] = kp_ref[0].astype(BF16)
    k_all[:, t_len:2 * t_len, :] = kc_ref[0].astype(BF16)
    v_all[:, 0:t_len, :] = vp_ref[0].astype(BF16)
    v_all[:, t_len:2 * t_len, :] = vc_ref[0].astype(BF16)
    part = t_len // DIL_FOLD
    for half in range(N_HALVES):
        for f in range(DIL_FOLD):
            fold_rows = pl.ds(f, part, stride=DIL_FOLD)
            q_fold[half, f * part:(f + 1) * part, :] = q_ref[0, half, fold_rows, :]
            for dst, prev_ref, cur_ref in ((k_fold, kp_ref, kc_ref), (v_fold, vp_ref, vc_ref)):
                dst[half, 2 * f * part:(2 * f + 1) * part, :] = prev_ref[0, half, fold_rows, :]
                dst[half, (2 * f + 1) * part:(2 * f + 2) * part, :] = cur_ref[0, half, fold_rows, :]
    lane = lax.broadcasted_iota(jnp.int32, (1, LANES), 1)
    head_masks = [_head_of_lane(lane) == hh for hh in range(HEADS_PER_HALF)]

    for pi, (window, r) in enumerate(reversed(DIL_PATTERNS)):
        assert window // r == qb
        opening = pi == 0
        last = pi == len(DIL_PATTERNS) - 1
        shift = r.bit_length() - 1
        tiles = t_len // qb

        def rows_of(start, size, r=r):
            return pl.ds(start, size) if r == 1 else pl.ds(start, size, stride=r)

        def tile_group(it, carry, r=r, shift=shift, opening=opening, last=last, rows_of=rows_of):
            folded = r >= DIL_FOLD
            q_rows, k_rows, q_src, k_src, bias = [], [], [], [], []
            for u in range(DIL_UNROLL):
                idx = it * DIL_UNROLL + u
                t = jnp.right_shift(idx, shift)
                j = jnp.bitwise_and(idx, r - 1)
                q0 = t * (qb * r) + j
                has_prev = jnp.logical_or(n > 0, t > 0)
                bias.append(bias_ref[jnp.where(has_prev, 0, 1)])
                q_rows.append(rows_of(q0, qb))
                k_rows.append(rows_of(t_len + q0 - qb * r, 2 * qb))
                if folded:
                    part = t_len // DIL_FOLD
                    r_fold = r // DIL_FOLD
                    f = jnp.bitwise_and(j, DIL_FOLD - 1)
                    m0 = t * (qb * r_fold) + jnp.right_shift(j, DIL_FOLD.bit_length() - 1)

                    def fold_ds(start, size, r_fold=r_fold):
                        return pl.ds(start, size) if r_fold == 1 else pl.ds(start, size, stride=r_fold)

                    q_src.append(fold_ds(f * part + m0, qb))
                    k_src.append(fold_ds(2 * f * part + part + m0 - qb * r_fold, 2 * qb))
            slabs = [(u, half) for u in range(DIL_UNROLL) for half in range(N_HALVES)]
            units = [(si, hh) for si in range(len(slabs)) for hh in range(HEADS_PER_HALF)]
            if folded:
                qs = [q_fold[half, q_src[u], :] for u, half in slabs]
                kb = [k_fold[half, k_src[u], :].astype(BF16) for u, half in slabs]
                vb = [v_fold[half, k_src[u], :].astype(BF16) for u, half in slabs]
            else:
                qs = [q_ref[0, half, q_rows[u], :] for u, half in slabs]
                kb = [k_all[half, k_rows[u], :].astype(BF16) for u, half in slabs]
                vb = [v_all[half, k_rows[u], :].astype(BF16) for u, half in slabs]
            sc = [_dot_nt(jnp.where(head_masks[hh], qs[si], 0.0).astype(BF16), kb[si]) + bias[slabs[si][0]]
                  for si, hh in units]
            mx = [jnp.max(x, axis=-1, keepdims=True) for x in sc]
            p = [jnp.exp(x - m) for x, m in zip(sc, mx)]
            den = [jnp.sum(x, axis=-1, keepdims=True) for x in p]
            pv = [_dot(x.astype(BF16), vb[si]) for x, (si, _) in zip(p, units)]
            for si, (u, half) in enumerate(slabs):
                first = si * HEADS_PER_HALF
                num_t, den_t, mx_t = pv[first], den[first], mx[first]
                for hh in range(1, HEADS_PER_HALF):
                    hm = head_masks[hh]
                    num_t = jnp.where(hm, pv[first + hh], num_t)
                    den_t = jnp.where(hm, den[first + hh], den_t)
                    mx_t = jnp.where(hm, mx[first + hh], mx_t)
                rows = q_rows[u]
                if opening:
                    m_new, num_new, den_new = mx_t, num_t, den_t
                else:
                    m_old = mx_s[half, rows, :]
                    m_new = jnp.maximum(m_old, mx_t)
                    a_old = jnp.exp(m_old - m_new)
                    a_new = jnp.exp(mx_t - m_new)
                    num_new = num_s[half, rows, :] * a_old + num_t * a_new
                    den_new = den_s[half, rows, :] * a_old + den_t * a_new
                if last:
                    o_ref[0, half, rows, :] = num_new / den_new
                else:
                    mx_s[half, rows, :] = jnp.broadcast_to(m_new, (qb, LANES))
                    num_s[half, rows, :] = num_new
                    den_s[half, rows, :] = jnp.broadcast_to(den_new, (qb, LANES))
            return carry

        lax.fori_loop(0, tiles // DIL_UNROLL, tile_group, 0)


def _dilated(q, k, v):
    b, _, s, _ = q.shape
    qb = DIL_QBLK
    t_len = DIL_TILE
    a_idx = np.arange(qb)[:, None]
    k_idx = np.arange(2 * qb)[None, :]
    dist = a_idx + qb - k_idx
    band = (dist >= 0) & (dist <= qb)
    bias = np.stack([np.where(band, 0.0, NEG_BIG), np.where(band & (k_idx >= qb), 0.0, NEG_BIG)])
    cur = pl.BlockSpec((1, N_HALVES, t_len, LANES), lambda bi, n: (bi, 0, n, 0))
    prev = pl.BlockSpec((1, N_HALVES, t_len, LANES), lambda bi, n: (bi, 0, jnp.maximum(n - 1, 0), 0))
    stat = pltpu.VMEM((N_HALVES, t_len, LANES), F32)
    both = pltpu.VMEM((N_HALVES, 2 * t_len, LANES), F32)
    both_bf16 = pltpu.VMEM((N_HALVES, 2 * t_len, LANES), BF16)
    return pl.pallas_call(
        _dil_kernel,
        grid=(b, s // t_len),
        in_specs=[cur, prev, cur, prev, cur, _const_spec((2, qb, 2 * qb))],
        out_specs=cur,
        out_shape=jax.ShapeDtypeStruct((b, N_HALVES, s, LANES), F32),
        scratch_shapes=[both_bf16, both_bf16, stat, both, both, stat, stat, stat],
        compiler_params=_params(("arbitrary", "arbitrary")),
        name="dilated",
    )(q, k, k, v, v, jnp.asarray(bias, F32))


HGRN_CHUNK = 128
HGRN_LEVELS = HGRN_CHUNK.bit_length() - 1
HGRN_TILE = 1024
SUBLANES = 8


LOG2_E = 1.4426950408889634


def _log2_split_decay(g, log_f, half):
    c, w = g.shape
    block = 2 * half
    if block >= SUBLANES:
        g3 = g.reshape(c // block, block, w)
        upper = lax.broadcasted_iota(jnp.int32, (1, block, 1), 1) >= half
        return ((g3 - g3[:, half - 1:half, :]) * jnp.where(upper, LOG2_E, -LOG2_E)).reshape(c, w)
    pos = jnp.bitwise_and(lax.broadcasted_iota(jnp.int32, (c, 1), 0), block - 1)
    if half == 1:
        return jnp.where(pos == 1, log_f * LOG2_E, 0.0)
    assert half == 2
    nxt = pltpu.roll(log_f, c - 1, 0)
    prv = pltpu.roll(log_f, 1, 0)
    dist = jnp.where(pos == 0, nxt, jnp.where(pos == 1, 0.0, jnp.where(pos == 2, log_f, log_f + prv)))
    return dist * LOG2_E


def _hgrn_kernel(q_ref, f_ref, i_ref, g_ref, lbl_ref, ng_ref, tril_ref, lvl_ref, o_ref,
                 state_ref, *, layer):
    @pl.when(pl.program_id(1) == 0)
    def _():
        state_ref[...] = jnp.zeros_like(state_ref)

    norm_gain = ng_ref[...]
    tril = tril_ref[...]
    lvl = lvl_ref[...]
    _hgrn_tile(q_ref.at[0], f_ref.at[0], i_ref.at[0], g_ref.at[0], lbl_ref[...], layer,
               norm_gain, tril, lvl, state_ref, o_ref.at[0])


def _hgrn_tile(q_ref, f_ref, i_ref, g_ref, logits, layer, norm_gain, tril, lvl, state_ref, o_ref):
    c = HGRN_CHUNK
    chunks = range(q_ref.shape[0] // c)
    heads = range(q_ref.shape[1] // HGRN_D)
    lanes = [slice(hd * HGRN_D, (hd + 1) * HGRN_D) for hd in heads]
    rows = [slice(ci * c, (ci + 1) * c) for ci in chunks]

    g_last, q_dec, k_dec, q_lvl, k_lvl = [], [], [], [], []
    for ci in chunks:
        zq = q_ref[rows[ci], :]
        log_f, one_m_f = _hgrn_gate_terms(f_ref[rows[ci], :], logits, layer)
        lf_hi, lf_lo = _split_bf16(log_f)
        g_c = _dot(tril, lf_hi) + _dot(tril, lf_lo)
        qq_c = zq * _sigmoid(zq)
        g_last.append(g_c[c - 1:c, :])
        q_dec.append((qq_c * jnp.exp(g_c)).astype(BF16))
        k_dec.append((one_m_f * jnp.exp(g_last[ci] - g_c)).astype(BF16))
        ql = [qq_c.astype(BF16)]
        kl = [one_m_f.astype(BF16)]
        for lv in range(HGRN_LEVELS):
            decay = jnp.exp2(_log2_split_decay(g_c, log_f, 1 << lv).astype(BF16))
            ql.append(ql[0] * decay)
            kl.append(kl[0] * decay)
        q_lvl.append(ql)
        k_lvl.append(kl)

    val = [i_ref[rows[ci], :] for ci in chunks]
    updates = [[_dot_tn(val[ci][:, lanes[hd]], k_dec[ci][:, lanes[hd]]) for hd in heads] for ci in chunks]
    states = [[state_ref[hd] for hd in heads]]
    for ci in chunks:
        states.append([states[ci][hd] * jnp.exp(g_last[ci][:, lanes[hd]]) + updates[ci][hd]
                       for hd in heads])
    for hd in heads:
        state_ref[hd] = states[-1][hd]
    o_inter = [[_dot_nt(q_dec[ci][:, lanes[hd]], states[ci][hd].astype(BF16)) for hd in heads]
               for ci in chunks]
    scores = [[[_dot_nt(q_lvl[ci][lv][:, lanes[hd]], k_lvl[ci][lv][:, lanes[hd]])
                for lv in range(HGRN_LEVELS + 1)] for hd in heads] for ci in chunks]

    level_masks = [lvl == lv for lv in range(HGRN_LEVELS + 1)]
    for ci in chunks:
        outs = []
        for hd in heads:
            p = jnp.zeros((c, c), F32)
            for lv in range(HGRN_LEVELS + 1):
                p = jnp.where(level_masks[lv], scores[ci][hd][lv], p)
            o_h = _dot(p.astype(BF16), val[ci][:, lanes[hd]]) + o_inter[ci][hd]
            outs.append(o_h * _rms_scale(o_h) * norm_gain)
        zg = g_ref[rows[ci], :]
        o_ref[rows[ci], :] = jnp.concatenate(outs, axis=-1) * (zg * _sigmoid(zg))


def _hgrn_tables():
    c = HGRN_CHUNK
    idx = np.arange(c)
    tril = idx[:, None] >= idx[None, :]
    diff = idx[:, None] ^ idx[None, :]
    lvl = np.where(idx[:, None] > idx[None, :], np.floor(np.log2(np.maximum(diff, 1))) + 1, -1)
    lvl = np.where(idx[:, None] == idx[None, :], 0, lvl).astype(np.int32)
    return jnp.asarray(tril, BF16), jnp.asarray(lvl)


def _hgrn(qh, fh, ih, gh, lb_logits, norm_gain, layer):
    b, s, _ = qh.shape
    t = HGRN_TILE
    c = HGRN_CHUNK
    depth = lb_logits.shape[0]
    tril, lvl = _hgrn_tables()
    tok = pl.BlockSpec((1, t, HGRN_W), lambda bi, i: (bi, i, 0))
    return pl.pallas_call(
        functools.partial(_hgrn_kernel, layer=layer),
        grid=(b, s // t),
        in_specs=[tok, tok, tok, tok,
                  _const_spec((depth, HGRN_W)),
                  _const_spec((1, HGRN_D)),
                  _const_spec((c, c)),
                  _const_spec((c, c))],
        out_specs=tok,
        out_shape=jax.ShapeDtypeStruct((b, s, HGRN_W), F32),
        scratch_shapes=[pltpu.VMEM((HGRN_HEADS, HGRN_D, HGRN_D), F32)],
        compiler_params=_params(("arbitrary", "arbitrary")),
        name="hgrn2",
    )(qh, fh, ih, gh, lb_logits.astype(F32), norm_gain.reshape(1, HGRN_D), tril, lvl)


def _outproj_ffn_kernel(x_ref, mod_ref, oa_ref, od_ref, oh_ref, wo_ref, wg_ref, wu_ref, wd_ref, o_ref,
                        h_scr, acc_scr):
    y = _dot(oa_ref[0].astype(BF16), wo_ref[0:SLAB, :])
    for half in range(N_HALVES):
        lo = SLAB + half * LANES
        y += _dot(od_ref[0, half].astype(BF16), wo_ref[lo:lo + LANES, :])
    y += _dot(oh_ref[0].astype(BF16), wo_ref[2 * SLAB:, :])
    gate = mod_ref[0, 5:6, :]
    o_ref[0] = x_ref[0] + gate * y
    _ffn_half_step(o_ref, mod_ref, 6, wg_ref, wu_ref, wd_ref, h_scr, acc_scr)


def _outproj_ffn(x, mod_l, o_a, o_d, o_h, w_out, w_gate, w_up, w_down, layer, tm=512):
    b, s, d = x.shape

    def tok(width):
        return pl.BlockSpec((1, tm, width), lambda bi, i: (bi, i, 0))

    return pl.pallas_call(
        _outproj_ffn_kernel,
        grid=(b, s // tm),
        in_specs=[tok(d), pl.BlockSpec((1, N_MOD, d), lambda bi, i: (bi, 0, 0)), tok(SLAB),
                  pl.BlockSpec((1, N_HALVES, tm, LANES), lambda bi, i: (bi, 0, i, 0)),
                  tok(HGRN_W), _layer_spec(w_out.shape, layer),
                  _layer_spec(w_gate.shape, layer), _layer_spec(w_up.shape, layer),
                  _layer_spec(w_down.shape, layer)],
        out_specs=tok(d),
        out_shape=jax.ShapeDtypeStruct((b, s, d), F32),
        scratch_shapes=[pltpu.VMEM((tm, d), BF16), pltpu.VMEM((tm, d), F32)],
        compiler_params=_params(("arbitrary", "arbitrary")),
        name="outproj_ffn",
    )(x, mod_l, o_a, o_d, o_h, w_out, w_gate, w_up, w_down)


def kernel(x, c, w_mod, b_mod, ffn1_w_gate, ffn1_w_up, ffn1_w_down, w_in, w_out, q_norm_g, k_norm_g,
           hgrn_norm_g, hgrn_lb_logits, ffn2_w_gate, ffn2_w_up, ffn2_w_down):
    b, s, d = x.shape
    depth = w_mod.shape[0]
    assert s % DIL_TILE == 0 and s % HGRN_TILE == 0 and s % SB_QBLK == 0
    mod = _modulation(c, w_mod, b_mod).reshape(depth, b, N_MOD, d)
    cos, sin = _rope_tables(s)
    ffn1 = [w.astype(BF16) for w in (ffn1_w_gate, ffn1_w_up, ffn1_w_down)]
    ffn2 = [w.astype(BF16) for w in (ffn2_w_gate, ffn2_w_up, ffn2_w_down)]
    w_in_b = w_in.astype(BF16)
    w_out_b = w_out.astype(BF16)
    for l in range(depth):
        x = _ffn(x, mod[l], *ffn1, layer=l, row0=0)
        qa, ka, va, qd, kd, vd, qh, fh, ih, gh = _inproj(
            x, mod[l], w_in_b, l, q_norm_g[l], k_norm_g[l], cos, sin)
        o_a = _stick_breaking(qa, ka, va)
        o_d = _dilated(qd, kd, vd)
        o_h = _hgrn(qh, fh, ih, gh, hgrn_lb_logits, hgrn_norm_g[l], layer=l)
        x = _outproj_ffn(x, mod[l], o_a, o_d, o_h, w_out_b, *ffn2, layer=l)
    return x
```

```python
import functools

import jax
import jax.numpy as jnp
import numpy as np
from jax import lax
from jax.experimental import pallas as pl
from jax.experimental.pallas import tpu as pltpu

F32 = jnp.float32
BF16 = jnp.bfloat16

HEAD_DIM = 64
N_HEADS = 4
SLAB = N_HEADS * HEAD_DIM
LANES = 128
N_HALVES = SLAB // LANES
HEADS_PER_HALF = LANES // HEAD_DIM
HGRN_HEADS = 4
HGRN_D = 128
HGRN_W = HGRN_HEADS * HGRN_D
N_MOD = 9
EPS = 1e-6
LB_FLOOR = 1e-30
NEG_BIG = -1e30
HALF_STEP = 0.5
ROPE_THETA = 10000.0
DIL_PATTERNS = ((128, 1), (512, 4), (2048, 16))
DIL_QBLK = 128

OFF_QA, OFF_KA, OFF_VA = 0, 256, 512
OFF_QD, OFF_KD, OFF_VD = 768, 1024, 1280
OFF_QH, OFF_FH, OFF_IH, OFF_GH = 1536, 2048, 2560, 3072

VMEM_LIMIT = 56 * 1024 * 1024
SB_DEAD = 104.0


def _params(sem):
    return pltpu.CompilerParams(dimension_semantics=sem, vmem_limit_bytes=VMEM_LIMIT)


def _const_spec(shape):
    nd = len(shape)
    return pl.BlockSpec(shape, lambda *_: (0,) * nd, pipeline_mode=pl.Buffered(1))


def _split_bf16(x):
    hi = x.astype(BF16)
    lo = (x - hi.astype(F32)).astype(BF16)
    return hi, lo


def _dot(a, b):
    return jnp.dot(a, b, preferred_element_type=F32)


def _dot_nt(a, b):
    return lax.dot_general(a, b, (((1,), (1,)), ((), ())), preferred_element_type=F32)


def _dot_tn(a, b):
    return lax.dot_general(a, b, (((0,), (0,)), ((), ())), preferred_element_type=F32)


def _sigmoid(x):
    return 1.0 / (1.0 + jnp.exp(-x))


def _log_sigmoid(x):
    return jnp.minimum(x, 0.0) - jnp.log(1.0 + jnp.exp(-jnp.abs(x)))


def _rms_scale(x):
    return lax.rsqrt(jnp.mean(x * x, axis=-1, keepdims=True) + EPS)


def _head_of_lane(lane):
    return jnp.right_shift(lane, HEAD_DIM.bit_length() - 1)


def _mod_kernel(c_ref, w_ref, b_ref, o_ref):
    c = c_ref[...]
    sc = c * _sigmoid(c)
    sc_hi, sc_lo = _split_bf16(sc)
    w = w_ref[0]
    w_hi, w_lo = _split_bf16(w)
    o_ref[0] = _dot(sc_hi, w_hi) + (_dot(sc_hi, w_lo) + _dot(sc_lo, w_hi)) + b_ref[0]


def _modulation(c, w_mod, b_mod):
    depth, d, n = w_mod.shape
    rows = 16
    nb = c.shape[0]
    assert nb <= rows
    c = jnp.pad(c, ((0, rows - nb), (0, 0)))
    b = rows
    tn = 1152
    out = pl.pallas_call(
        _mod_kernel,
        grid=(depth, n // tn),
        in_specs=[
            pl.BlockSpec((b, d), lambda l, j: (0, 0)),
            pl.BlockSpec((1, d, tn), lambda l, j: (l, 0, j)),
            pl.BlockSpec((1, 1, tn), lambda l, j: (l, 0, j)),
        ],
        out_specs=pl.BlockSpec((1, b, tn), lambda l, j: (l, 0, j)),
        out_shape=jax.ShapeDtypeStruct((depth, b, n), F32),
        compiler_params=_params(("arbitrary", "arbitrary")),
        name="adaln_mod",
    )(c, w_mod, b_mod.reshape(depth, 1, n))
    return out[:, :nb]


FFN_CHUNK = 256


def _ffn_half_step(xo_ref, mod_ref, row0, wg_ref, wu_ref, wd_ref, h_scr, acc_scr):
    shift = mod_ref[0, row0:row0 + 1, :]
    scale = mod_ref[0, row0 + 1:row0 + 2, :]
    gate = mod_ref[0, row0 + 2:row0 + 3, :]
    x = xo_ref[0]
    h_scr[...] = (x * _rms_scale(x) * (1.0 + scale) + shift).astype(BF16)
    for c in range(wg_ref.shape[1] // FFN_CHUNK):
        cols = slice(c * FFN_CHUNK, (c + 1) * FFN_CHUNK)
        g = _dot(h_scr[...], wg_ref[:, cols])
        u = _dot(h_scr[...], wu_ref[:, cols])
        a = (g * _sigmoid(g) * u).astype(BF16)
        y = _dot(a, wd_ref[cols, :])
        if c == 0:
            acc_scr[...] = y
        else:
            acc_scr[...] += y
    xo_ref[0] = xo_ref[0] + (HALF_STEP * gate) * acc_scr[...]


def _ffn_kernel(x_ref, mod_ref, wg_ref, wu_ref, wd_ref, o_ref, h_scr, acc_scr, *, row0):
    o_ref[0] = x_ref[0]
    _ffn_half_step(o_ref, mod_ref, row0, wg_ref, wu_ref, wd_ref, h_scr, acc_scr)


def _layer_spec(shape, layer):
    nd = len(shape)
    return pl.BlockSpec((None,) + tuple(shape[1:]), lambda *_: (layer,) + (0,) * (nd - 1),
                        pipeline_mode=pl.Buffered(1))


def _ffn(x, mod_l, w_gate, w_up, w_down, layer, row0, tm=1024):
    b, s, d = x.shape
    return pl.pallas_call(
        functools.partial(_ffn_kernel, row0=row0),
        grid=(b, s // tm),
        in_specs=[
            pl.BlockSpec((1, tm, d), lambda bi, i: (bi, i, 0)),
            pl.BlockSpec((1, N_MOD, d), lambda bi, i: (bi, 0, 0)),
            _layer_spec(w_gate.shape, layer),
            _layer_spec(w_up.shape, layer),
            _layer_spec(w_down.shape, layer),
        ],
        out_specs=pl.BlockSpec((1, tm, d), lambda bi, i: (bi, i, 0)),
        out_shape=jax.ShapeDtypeStruct((b, s, d), F32),
        scratch_shapes=[pltpu.VMEM((tm, d), BF16), pltpu.VMEM((tm, d), F32)],
        compiler_params=_params(("arbitrary", "arbitrary")),
        name="ffn",
    )(x, mod_l, w_gate, w_up, w_down)


def _swap_halves(x):
    n = x.shape[-1]
    lane = lax.broadcasted_iota(jnp.int32, x.shape, x.ndim - 1)
    up = pltpu.roll(x, n - HEAD_DIM // 2, x.ndim - 1)
    down = pltpu.roll(x, HEAD_DIM // 2, x.ndim - 1)
    return jnp.where(jnp.bitwise_and(lane, HEAD_DIM - 1) < HEAD_DIM // 2, up, down)


def _qk_norm_rope(x, gain, cos, sin_signed, head_mean):
    sq_hi, sq_lo = _split_bf16(x * x)
    ms = _dot(sq_hi, head_mean) + _dot(sq_lo, head_mean)
    xn = x * lax.rsqrt(ms + EPS) * gain
    return xn * cos + _swap_halves(xn) * sin_signed


def _hgrn_gate_terms(z, logits, layer):
    ex = jnp.exp(logits - jnp.max(logits, axis=0, keepdims=True))
    sm = ex / jnp.sum(ex, axis=0, keepdims=True)
    lb = jnp.zeros_like(sm[0:1, :])
    for l in range(1, layer + 1):
        lb = lb + sm[l:l + 1, :]
    lb = jnp.clip(lb, 0.0, 1.0 - EPS)
    lb_floor = jnp.maximum(lb, LB_FLOOR)
    log_lb = jnp.log(lb_floor)
    one_m_lb = 1.0 - lb
    e = jnp.exp(-jnp.abs(z))
    one_p_e = 1.0 + e
    bterm = jnp.log(one_m_lb) + (jnp.minimum(z, 0.0) - jnp.log(one_p_e))
    log_f = jnp.maximum(log_lb, bterm) + jnp.log(1.0 + jnp.exp(-jnp.abs(log_lb - bterm)))
    sigmoid_neg = jnp.where(z > 0.0, e, 1.0) / one_p_e
    one_m_f = one_m_lb * sigmoid_neg - (lb_floor - lb)
    return log_f, one_m_f


def _inproj_kernel(x_ref, mod_ref, w_ref, cos_ref, sin_ref, qg_ref, kg_ref, hm_ref,
                   qa_ref, ka_ref, va_ref, qd_ref, kd_ref, vd_ref, qh_ref, fh_ref, ih_ref, gh_ref):
    x = x_ref[0]
    shift = mod_ref[0, 3:4, :]
    scale = mod_ref[0, 4:5, :]
    h = (x * _rms_scale(x) * (1.0 + scale) + shift).astype(BF16)

    def proj(off, width):
        return _dot(h, w_ref[:, off:off + width])

    def put_halves(ref, val):
        for half in range(N_HALVES):
            ref[0, half] = val[:, half * LANES:(half + 1) * LANES]

    qd_raw = proj(OFF_QD, SLAB)
    kd_raw = proj(OFF_KD, SLAB)
    qh_ref[0] = proj(OFF_QH, HGRN_W)
    cos = cos_ref[...]
    sin = sin_ref[...]
    hm = hm_ref[...]
    qd = _qk_norm_rope(qd_raw, qg_ref[...], cos, sin, hm)
    put_halves(qd_ref, qd * (HEAD_DIM ** -0.5))
    put_halves(kd_ref, _qk_norm_rope(kd_raw, kg_ref[...], cos, sin, hm))
    fh_ref[0] = proj(OFF_FH, HGRN_W)
    gh_ref[0] = proj(OFF_GH, HGRN_W)
    ih_ref[0] = proj(OFF_IH, HGRN_W).astype(BF16)
    qa_ref[0] = proj(OFF_QA, SLAB)
    ka_ref[0] = proj(OFF_KA, SLAB).astype(BF16)
    va_ref[0] = proj(OFF_VA, SLAB).astype(BF16)
    put_halves(vd_ref, proj(OFF_VD, SLAB))


def _rope_tables(s):
    half = HEAD_DIM // 2
    inv_freq = ROPE_THETA ** (-jnp.arange(half, dtype=F32) * 2.0 / HEAD_DIM)
    ang = jnp.arange(s, dtype=F32)[:, None] * inv_freq[None, :]
    cos, sin = jnp.cos(ang), jnp.sin(ang)
    cos_full = jnp.tile(jnp.concatenate([cos, cos], axis=-1), (1, N_HEADS))
    sin_signed = jnp.tile(jnp.concatenate([-sin, sin], axis=-1), (1, N_HEADS))
    return cos_full, sin_signed


def _inproj(x, mod_l, w_in, layer, q_gain, k_gain, cos, sin, tm=1024):
    b, s, d = x.shape
    qg = jnp.tile(q_gain.reshape(1, HEAD_DIM), (1, N_HEADS))
    kg = jnp.tile(k_gain.reshape(1, HEAD_DIM), (1, N_HEADS))
    head_id = np.arange(SLAB) // HEAD_DIM
    head_mean = jnp.asarray(np.where(head_id[:, None] == head_id[None, :], 1.0 / HEAD_DIM, 0.0), BF16)

    def tok(width):
        return pl.BlockSpec((1, tm, width), lambda bi, i: (bi, i, 0))

    def out(width, dt):
        return jax.ShapeDtypeStruct((b, s, width), dt)

    halves = pl.BlockSpec((1, N_HALVES, tm, LANES), lambda bi, i: (bi, 0, i, 0))
    halves_out = jax.ShapeDtypeStruct((b, N_HALVES, s, LANES), F32)

    return pl.pallas_call(
        _inproj_kernel,
        grid=(b, s // tm),
        in_specs=[
            tok(d),
            pl.BlockSpec((1, N_MOD, d), lambda bi, i: (bi, 0, 0)),
            _layer_spec(w_in.shape, layer),
            pl.BlockSpec((tm, SLAB), lambda bi, i: (i, 0)),
            pl.BlockSpec((tm, SLAB), lambda bi, i: (i, 0)),
            _const_spec((1, SLAB)),
            _const_spec((1, SLAB)),
            _const_spec((SLAB, SLAB)),
        ],
        out_specs=[tok(SLAB)] * 3 + [halves] * 3 + [tok(HGRN_W)] * 4,
        out_shape=[out(SLAB, F32), out(SLAB, BF16), out(SLAB, BF16),
                   halves_out, halves_out, halves_out,
                   out(HGRN_W, F32), out(HGRN_W, F32), out(HGRN_W, BF16), out(HGRN_W, F32)],
        compiler_params=_params(("arbitrary", "arbitrary")),
        name="mixer_inproj",
    )(x, mod_l, w_in, cos, sin, qg, kg, head_mean)


SB_QBLK = 256
SB_KBLK = 128
SB_MIN_ROWS = 64


def _sb_kernel(q_ref, k_ref, v_ref, uu_ref, o_ref, acc_ref, run_ref, qm_ref, z_ref):
    i = pl.program_id(1)
    qblk, kblk = SB_QBLK, SB_KBLK
    heads = range(N_HEADS)
    lane = lax.broadcasted_iota(jnp.int32, (1, SLAB), 1)
    head_masks = [_head_of_lane(lane) == hd for hd in heads]
    qm_ref[...] = (q_ref[0] * (HEAD_DIM ** -0.5)).astype(BF16)
    acc_ref[...] = jnp.zeros_like(acc_ref)
    run_ref[...] = jnp.zeros_like(run_ref)

    def per_head_rows(x):
        return jnp.concatenate([jnp.where(head_masks[hd], x, jnp.zeros_like(x)) for hd in heads], axis=0)

    def scores(j, rows=slice(0, SB_QBLK)):
        off = pl.multiple_of(jnp.maximum(j, 0) * kblk, kblk)
        z_all = _dot_nt(qm_ref[rows, :], per_head_rows(k_ref[0, pl.ds(off, kblk), :]))
        return [z_all[:, hd * kblk:(hd + 1) * kblk] for hd in heads]

    def key_block(j, z, causal_below, rows=slice(0, SB_QBLK)):
        off = pl.multiple_of(j * kblk, kblk)
        v = v_ref[0, pl.ds(off, kblk), :]
        uu = uu_ref[...]
        causal = None
        if causal_below is not None:
            shape = (rows.stop - rows.start, kblk)
            col_minus_row = (lax.broadcasted_iota(jnp.int32, shape, 1)
                             - lax.broadcasted_iota(jnp.int32, shape, 0))
            causal = col_minus_row < causal_below + rows.start
        ls = [_log_sigmoid(z[hd]) for hd in heads]
        lnb = [ls[hd] - z[hd] for hd in heads]
        if causal is not None:
            lnb = [jnp.where(causal, x, 0.0) for x in lnb]
        tail = []
        for hd in heads:
            lnb_hi, lnb_lo = _split_bf16(lnb[hd])
            tail.append(_dot(jnp.concatenate([lnb_hi, lnb_lo], axis=1), uu))
        run = [run_ref[hd, rows, :] for hd in heads]
        w = [jnp.exp(ls[hd] + tail[hd] + run[hd]) for hd in heads]
        if causal is not None:
            w = [jnp.where(causal, x, 0.0) for x in w]
        pv = _dot(jnp.concatenate([w[hd].astype(BF16) for hd in heads], axis=1), per_head_rows(v))
        run_max = None
        for hd in heads:
            new_run = run[hd] + (tail[hd][:, 0:1] + lnb[hd][:, 0:1])
            run_ref[hd, rows, :] = new_run
            run_max = new_run if run_max is None else jnp.maximum(run_max, new_run)
        acc_ref[rows, :] += pv
        return run_max

    def alive_in(run_max):
        return jnp.max(run_max) > -SB_DEAD

    per_q = qblk // kblk
    upper, lower = slice(0, qblk // 2), slice(qblk // 2, qblk)
    top = i * per_q + per_q - 1
    z_now = scores(top)
    run_max = None
    for d in range(per_q):
        z_next = scores(top - d - 1)
        if d == per_q - 1:
            for hd in heads:
                z_ref[hd] = z_next[hd]
        rows = slice((per_q - 1 - d) * kblk, qblk)
        run_max = key_block(top - d, [x[rows] for x in z_now], (d + 1 - per_q) * kblk, rows)
        z_now = z_next

    def far_blocks(rows, j0, first_alive, second_alive):
        n_rows = rows.stop - rows.start
        split = n_rows > SB_MIN_ROWS
        mid = n_rows // 2

        def cond(carry):
            j, a, b = carry
            return jnp.logical_and(j >= 0, jnp.logical_and(a, b) if split else a)

        def body(carry):
            j, _, _ = carry
            z = [z_ref[hd, rows, :] for hd in heads]
            z_ahead = scores(j - 1, rows)
            run_max = key_block(j, z, None, rows)
            for hd in heads:
                z_ref[hd, rows, :] = z_ahead[hd]
            if split:
                return j - 1, alive_in(run_max[:mid]), alive_in(run_max[mid:])
            alive = alive_in(run_max)
            return j - 1, alive, alive

        j_end, a, b = lax.while_loop(cond, body, (j0, first_alive, second_alive))
        if split:
            for sub, alive in ((slice(rows.start, rows.start + mid), a), (slice(rows.start + mid, rows.stop), b)):
                far_blocks(sub, j_end, alive, alive)

    far_blocks(slice(0, qblk), top - per_q, alive_in(run_max[upper]), alive_in(run_max[lower]))
    o_ref[0] = acc_ref[...]


def _stick_breaking(q, k, v):
    b, s, _ = q.shape
    qblk, kblk = SB_QBLK, SB_KBLK
    idx = np.arange(kblk)
    u = (idx[:, None] > idx[None, :])
    uu = jnp.asarray(np.concatenate([u, u], axis=0), BF16)
    return pl.pallas_call(
        _sb_kernel,
        grid=(b, s // qblk),
        in_specs=[
            pl.BlockSpec((1, qblk, SLAB), lambda bi, i: (bi, i, 0)),
            pl.BlockSpec((1, s, SLAB), lambda bi, i: (bi, 0, 0), pipeline_mode=pl.Buffered(1)),
            pl.BlockSpec((1, s, SLAB), lambda bi, i: (bi, 0, 0), pipeline_mode=pl.Buffered(1)),
            _const_spec((2 * kblk, kblk)),
        ],
        out_specs=pl.BlockSpec((1, qblk, SLAB), lambda bi, i: (bi, i, 0)),
        out_shape=jax.ShapeDtypeStruct((b, s, SLAB), F32),
        scratch_shapes=[pltpu.VMEM((qblk, SLAB), F32),
                        pltpu.VMEM((N_HEADS, qblk, kblk), F32),
                        pltpu.VMEM((qblk, SLAB), BF16),
                        pltpu.VMEM((N_HEADS, qblk, kblk), F32)],
        compiler_params=_params(("arbitrary", "arbitrary")),
        name="stick_breaking",
    )(q, k, v, uu)


DIL_TILE = DIL_QBLK * max(r for _, r in DIL_PATTERNS)
DIL_UNROLL = 2
DIL_FOLD = 4


def _dil_kernel(q_ref, kp_ref, kc_ref, vp_ref, vc_ref, bias_ref, o_ref,
                k_all, v_all, q_fold, k_fold, v_fold, num_s, den_s, mx_s):
    n = pl.program_id(1)
    t_len = DIL_TILE
    qb = DIL_QBLK
    k_all[:, 0:t_len, :] = kp_ref[0].astype(BF16)
    k_all[:, t_len:2 * t_len, :] = kc_ref[0].astype(BF16)
    v_all[:, 0:t_len, :] = vp_ref[0].astype(BF16)
    v_all[:, t_len:2 * t_len, :] = vc_ref[0].astype(BF16)
    part = t_len // DIL_FOLD
    for half in range(N_HALVES):
        for f in range(DIL_FOLD):
            fold_rows = pl.ds(f, part, stride=DIL_FOLD)
            q_fold[half, f * part:(f + 1) * part, :] = q_ref[0, half, fold_rows, :]
            for dst, prev_ref, cur_ref in ((k_fold, kp_ref, kc_ref), (v_fold, vp_ref, vc_ref)):
                dst[half, 2 * f * part:(2 * f + 1) * part, :] = prev_ref[0, half, fold_rows, :]
                dst[half, (2 * f + 1) * part:(2 * f + 2) * part, :] = cur_ref[0, half, fold_rows, :]
    lane = lax.broadcasted_iota(jnp.int32, (1, LANES), 1)
    head_masks = [_head_of_lane(lane) == hh for hh in range(HEADS_PER_HALF)]

    for pi, (window, r) in enumerate(reversed(DIL_PATTERNS)):
        assert window // r == qb
        opening = pi == 0
        last = pi == len(DIL_PATTERNS) - 1
        shift = r.bit_length() - 1
        tiles = t_len // qb

        def rows_of(start, size, r=r):
            return pl.ds(start, size) if r == 1 else pl.ds(start, size, stride=r)

        def tile_group(it, carry, r=r, shift=shift, opening=opening, last=last, rows_of=rows_of):
            folded = r >= DIL_FOLD
            q_rows, k_rows, q_src, k_src, bias = [], [], [], [], []
            for u in range(DIL_UNROLL):
                idx = it * DIL_UNROLL + u
                t = jnp.right_shift(idx, shift)
                j = jnp.bitwise_and(idx, r - 1)
                q0 = t * (qb * r) + j
                has_prev = jnp.logical_or(n > 0, t > 0)
                bias.append(bias_ref[jnp.where(has_prev, 0, 1)])
                q_rows.append(rows_of(q0, qb))
                k_rows.append(rows_of(t_len + q0 - qb * r, 2 * qb))
                if folded:
                    part = t_len // DIL_FOLD
                    r_fold = r // DIL_FOLD
                    f = jnp.bitwise_and(j, DIL_FOLD - 1)
                    m0 = t * (qb * r_fold) + jnp.right_shift(j, DIL_FOLD.bit_length() - 1)

                    def fold_ds(start, size, r_fold=r_fold):
                        return pl.ds(start, size) if r_fold == 1 else pl.ds(start, size, stride=r_fold)

                    q_src.append(fold_ds(f * part + m0, qb))
                    k_src.append(fold_ds(2 * f * part + part + m0 - qb * r_fold, 2 * qb))
            slabs = [(u, half) for u in range(DIL_UNROLL) for half in range(N_HALVES)]
            units = [(si, hh) for si in range(len(slabs)) for hh in range(HEADS_PER_HALF)]
            if folded:
                qs = [q_fold[half, q_src[u], :] for u, half in slabs]
                kb = [k_fold[half, k_src[u], :].astype(BF16) for u, half in slabs]
                vb = [v_fold[half, k_src[u], :].astype(BF16) for u, half in slabs]
            else:
                qs = [q_ref[0, half, q_rows[u], :] for u, half in slabs]
                kb = [k_all[half, k_rows[u], :].astype(BF16) for u, half in slabs]
                vb = [v_all[half, k_rows[u], :].astype(BF16) for u, half in slabs]
            sc = [_dot_nt(jnp.where(head_masks[hh], qs[si], 0.0).astype(BF16), kb[si]) + bias[slabs[si][0]]
                  for si, hh in units]
            mx = [jnp.max(x, axis=-1, keepdims=True) for x in sc]
            p = [jnp.exp(x - m) for x, m in zip(sc, mx)]
            den = [jnp.sum(x, axis=-1, keepdims=True) for x in p]
            pv = [_dot(x.astype(BF16), vb[si]) for x, (si, _) in zip(p, units)]
            for si, (u, half) in enumerate(slabs):
                first = si * HEADS_PER_HALF
                num_t, den_t, mx_t = pv[first], den[first], mx[first]
                for hh in range(1, HEADS_PER_HALF):
                    hm = head_masks[hh]
                    num_t = jnp.where(hm, pv[first + hh], num_t)
                    den_t = jnp.where(hm, den[first + hh], den_t)
                    mx_t = jnp.where(hm, mx[first + hh], mx_t)
                rows = q_rows[u]
                if opening:
                    m_new, num_new, den_new = mx_t, num_t, den_t
                else:
                    m_old = mx_s[half, rows, :]
                    m_new = jnp.maximum(m_old, mx_t)
                    a_old = jnp.exp(m_old - m_new)
                    a_new = jnp.exp(mx_t - m_new)
                    num_new = num_s[half, rows, :] * a_old + num_t * a_new
                    den_new = den_s[half, rows, :] * a_old + den_t * a_new
                if last:
                    o_ref[0, half, rows, :] = num_new / den_new
                else:
                    mx_s[half, rows, :] = jnp.broadcast_to(m_new, (qb, LANES))
                    num_s[half, rows, :] = num_new
                    den_s[half, rows, :] = jnp.broadcast_to(den_new, (qb, LANES))
            return carry

        lax.fori_loop(0, tiles // DIL_UNROLL, tile_group, 0)


def _dilated(q, k, v):
    b, _, s, _ = q.shape
    qb = DIL_QBLK
    t_len = DIL_TILE
    a_idx = np.arange(qb)[:, None]
    k_idx = np.arange(2 * qb)[None, :]
    dist = a_idx + qb - k_idx
    band = (dist >= 0) & (dist <= qb)
    bias = np.stack([np.where(band, 0.0, NEG_BIG), np.where(band & (k_idx >= qb), 0.0, NEG_BIG)])
    cur = pl.BlockSpec((1, N_HALVES, t_len, LANES), lambda bi, n: (bi, 0, n, 0))
    prev = pl.BlockSpec((1, N_HALVES, t_len, LANES), lambda bi, n: (bi, 0, jnp.maximum(n - 1, 0), 0))
    stat = pltpu.VMEM((N_HALVES, t_len, LANES), F32)
    both = pltpu.VMEM((N_HALVES, 2 * t_len, LANES), F32)
    both_bf16 = pltpu.VMEM((N_HALVES, 2 * t_len, LANES), BF16)
    return pl.pallas_call(
        _dil_kernel,
        grid=(b, s // t_len),
        in_specs=[cur, prev, cur, prev, cur, _const_spec((2, qb, 2 * qb))],
        out_specs=cur,
        out_shape=jax.ShapeDtypeStruct((b, N_HALVES, s, LANES), F32),
        scratch_shapes=[both_bf16, both_bf16, stat, both, both, stat, stat, stat],
        compiler_params=_params(("arbitrary", "arbitrary")),
        name="dilated",
    )(q, k, k, v, v, jnp.asarray(bias, F32))


HGRN_CHUNK = 128
HGRN_LEVELS = HGRN_CHUNK.bit_length() - 1
HGRN_TILE = 1024
SUBLANES = 8


LOG2_E = 1.4426950408889634


def _log2_split_decay(g, log_f, half):
    c, w = g.shape
    block = 2 * half
    if block >= SUBLANES:
        g3 = g.reshape(c // block, block, w)
        upper = lax.broadcasted_iota(jnp.int32, (1, block, 1), 1) >= half
        return ((g3 - g3[:, half - 1:half, :]) * jnp.where(upper, LOG2_E, -LOG2_E)).reshape(c, w)
    pos = jnp.bitwise_and(lax.broadcasted_iota(jnp.int32, (c, 1), 0), block - 1)
    if half == 1:
        return jnp.where(pos == 1, log_f * LOG2_E, 0.0)
    assert half == 2
    nxt = pltpu.roll(log_f, c - 1, 0)
    prv = pltpu.roll(log_f, 1, 0)
    dist = jnp.where(pos == 0, nxt, jnp.where(pos == 1, 0.0, jnp.where(pos == 2, log_f, log_f + prv)))
    return dist * LOG2_E


def _hgrn_kernel(q_ref, f_ref, i_ref, g_ref, lbl_ref, ng_ref, tril_ref, lvl_ref, o_ref,
                 state_ref, *, layer):
    @pl.when(pl.program_id(1) == 0)
    def _():
        state_ref[...] = jnp.zeros_like(state_ref)

    norm_gain = ng_ref[...]
    tril = tril_ref[...]
    lvl = lvl_ref[...]
    _hgrn_tile(q_ref.at[0], f_ref.at[0], i_ref.at[0], g_ref.at[0], lbl_ref[...], layer,
               norm_gain, tril, lvl, state_ref, o_ref.at[0])


def _hgrn_tile(q_ref, f_ref, i_ref, g_ref, logits, layer, norm_gain, tril, lvl, state_ref, o_ref):
    c = HGRN_CHUNK
    chunks = range(q_ref.shape[0] // c)
    heads = range(q_ref.shape[1] // HGRN_D)
    lanes = [slice(hd * HGRN_D, (hd + 1) * HGRN_D) for hd in heads]
    rows = [slice(ci * c, (ci + 1) * c) for ci in chunks]

    g_last, q_dec, k_dec, q_lvl, k_lvl = [], [], [], [], []
    for ci in chunks:
        zq = q_ref[rows[ci], :]
        log_f, one_m_f = _hgrn_gate_terms(f_ref[rows[ci], :], logits, layer)
        lf_hi, lf_lo = _split_bf16(log_f)
        g_c = _dot(tril, lf_hi) + _dot(tril, lf_lo)
        qq_c = zq * _sigmoid(zq)
        g_last.append(g_c[c - 1:c, :])
        q_dec.append((qq_c * jnp.exp(g_c)).astype(BF16))
        k_dec.append((one_m_f * jnp.exp(g_last[ci] - g_c)).astype(BF16))
        ql = [qq_c.astype(BF16)]
        kl = [one_m_f.astype(BF16)]
        for lv in range(HGRN_LEVELS):
            decay = jnp.exp2(_log2_split_decay(g_c, log_f, 1 << lv).astype(BF16))
            ql.append(ql[0] * decay)
            kl.append(kl[0] * decay)
        q_lvl.append(ql)
        k_lvl.append(kl)

    val = [i_ref[rows[ci], :] for ci in chunks]
    updates = [[_dot_tn(val[ci][:, lanes[hd]], k_dec[ci][:, lanes[hd]]) for hd in heads] for ci in chunks]
    states = [[state_ref[hd] for hd in heads]]
    for ci in chunks:
        states.append([states[ci][hd] * jnp.exp(g_last[ci][:, lanes[hd]]) + updates[ci][hd]
                       for hd in heads])
    for hd in heads:
        state_ref[hd] = states[-1][hd]
    o_inter = [[_dot_nt(q_dec[ci][:, lanes[hd]], states[ci][hd].astype(BF16)) for hd in heads]
               for ci in chunks]
    scores = [[[_dot_nt(q_lvl[ci][lv][:, lanes[hd]], k_lvl[ci][lv][:, lanes[hd]])
                for lv in range(HGRN_LEVELS + 1)] for hd in heads] for ci in chunks]

    level_masks = [lvl == lv for lv in range(HGRN_LEVELS + 1)]
    for ci in chunks:
        outs = []
        for hd in heads:
            p = jnp.zeros((c, c), F32)
            for lv in range(HGRN_LEVELS + 1):
                p = jnp.where(level_masks[lv], scores[ci][hd][lv], p)
            o_h = _dot(p.astype(BF16), val[ci][:, lanes[hd]]) + o_inter[ci][hd]
            outs.append(o_h * _rms_scale(o_h) * norm_gain)
        zg = g_ref[rows[ci], :]
        o_ref[rows[ci], :] = jnp.concatenate(outs, axis=-1) * (zg * _sigmoid(zg))


def _hgrn_tables():
    c = HGRN_CHUNK
    idx = np.arange(c)
    tril = idx[:, None] >= idx[None, :]
    diff = idx[:, None] ^ idx[None, :]
    lvl = np.where(idx[:, None] > idx[None, :], np.floor(np.log2(np.maximum(diff, 1))) + 1, -1)
    lvl = np.where(idx[:, None] == idx[None, :], 0, lvl).astype(np.int32)
    return jnp.asarray(tril, BF16), jnp.asarray(lvl)


def _hgrn(qh, fh, ih, gh, lb_logits, norm_gain, layer):
    b, s, _ = qh.shape
    t = HGRN_TILE
    c = HGRN_CHUNK
    depth = lb_logits.shape[0]
    tril, lvl = _hgrn_tables()
    tok = pl.BlockSpec((1, t, HGRN_W), lambda bi, i: (bi, i, 0))
    return pl.pallas_call(
        functools.partial(_hgrn_kernel, layer=layer),
        grid=(b, s // t),
        in_specs=[tok, tok, tok, tok,
                  _const_spec((depth, HGRN_W)),
                  _const_spec((1, HGRN_D)),
                  _const_spec((c, c)),
                  _const_spec((c, c))],
        out_specs=tok,
        out_shape=jax.ShapeDtypeStruct((b, s, HGRN_W), F32),
        scratch_shapes=[pltpu.VMEM((HGRN_HEADS, HGRN_D, HGRN_D), F32)],
        compiler_params=_params(("arbitrary", "arbitrary")),
        name="hgrn2",
    )(qh, fh, ih, gh, lb_logits.astype(F32), norm_gain.reshape(1, HGRN_D), tril, lvl)


def _outproj_ffn_kernel(x_ref, mod_ref, oa_ref, od_ref, oh_ref, wo_ref, wg_ref, wu_ref, wd_ref, o_ref,
                        h_scr, acc_scr):
    y = _dot(oa_ref[0].astype(BF16), wo_ref[0:SLAB, :])
    for half in range(N_HALVES):
        lo = SLAB + half * LANES
        y += _dot(od_ref[0, half].astype(BF16), wo_ref[lo:lo + LANES, :])
    y += _dot(oh_ref[0].astype(BF16), wo_ref[2 * SLAB:, :])
    gate = mod_ref[0, 5:6, :]
    o_ref[0] = x_ref[0] + gate * y
    _ffn_half_step(o_ref, mod_ref, 6, wg_ref, wu_ref, wd_ref, h_scr, acc_scr)


def _outproj_ffn(x, mod_l, o_a, o_d, o_h, w_out, w_gate, w_up, w_down, layer, tm=512):
    b, s, d = x.shape

    def tok(width):
        return pl.BlockSpec((1, tm, width), lambda bi, i: (bi, i, 0))

    return pl.pallas_call(
        _outproj_ffn_kernel,
        grid=(b, s // tm),
        in_specs=[tok(d), pl.BlockSpec((1, N_MOD, d), lambda bi, i: (bi, 0, 0)), tok(SLAB),
                  pl.BlockSpec((1, N_HALVES, tm, LANES), lambda bi, i: (bi, 0, i, 0)),
                  tok(HGRN_W), _layer_spec(w_out.shape, layer),
                  _layer_spec(w_gate.shape, layer), _layer_spec(w_up.shape, layer),
                  _layer_spec(w_down.shape, layer)],
        out_specs=tok(d),
        out_shape=jax.ShapeDtypeStruct((b, s, d), F32),
        scratch_shapes=[pltpu.VMEM((tm, d), BF16), pltpu.VMEM((tm, d), F32)],
        compiler_params=_params(("arbitrary", "arbitrary")),
        name="outproj_ffn",
    )(x, mod_l, o_a, o_d, o_h, w_out, w_gate, w_up, w_down)


def kernel(x, c, w_mod, b_mod, ffn1_w_gate, ffn1_w_up, ffn1_w_down, w_in, w_out, q_norm_g, k_norm_g,
           hgrn_norm_g, hgrn_lb_logits, ffn2_w_gate, ffn2_w_up, ffn2_w_down):
    b, s, d = x.shape
    depth = w_mod.shape[0]
    assert s % DIL_TILE == 0 and s % HGRN_TILE == 0 and s % SB_QBLK == 0
    mod = _modulation(c, w_mod, b_mod).reshape(depth, b, N_MOD, d)
    cos, sin = _rope_tables(s)
    ffn1 = [w.astype(BF16) for w in (ffn1_w_gate, ffn1_w_up, ffn1_w_down)]
    ffn2 = [w.astype(BF16) for w in (ffn2_w_gate, ffn2_w_up, ffn2_w_down)]
    w_in_b = w_in.astype(BF16)
    w_out_b = w_out.astype(BF16)
    for l in range(depth):
        x = _ffn(x, mod[l], *ffn1, layer=l, row0=0)
        qa, ka, va, qd, kd, vd, qh, fh, ih, gh = _inproj(
            x, mod[l], w_in_b, l, q_norm_g[l], k_norm_g[l], cos, sin)
        o_a = _stick_breaking(qa, ka, va)
        o_d = _dilated(qd, kd, vd)
        o_h = _hgrn(qh, fh, ih, gh, hgrn_lb_logits, hgrn_norm_g[l], layer=l)
        x = _outproj_ffn(x, mod[l], o_a, o_d, o_h, w_out_b, *ffn2, layer=l)
    return x
```
